```python
import math
import jax
import jax.numpy as jnp
from jax import lax
import numpy as np


D_MODEL = 1024
BATCH = 2
SEQ = 8192
DEPTH = 2

N_BRANCH = 4
SC_WIDTH = D_MODEL // 4
SC_GROUPS = 4
SC_K = 3
LRU_WIDTH = D_MODEL // 4
LRU_HEADS = 4
LRU_HEAD_DIM = LRU_WIDTH // LRU_HEADS
LRU_CONV_K = 4
LRU_C = 8.0
SG_WIDTH = D_MODEL // 4
SG_GROUPS = 4
SG_GROUP_DIM = SG_WIDTH // SG_GROUPS
SG_CHUNK = 128
RET_WIDTH = D_MODEL // 4
RET_HEADS = 4
RET_HEAD_DIM = RET_WIDTH // RET_HEADS
RET_CHUNK = 128
ROPE_BASE = 10000.0
N_GROUPS = 4
EXPERTS_PER_GROUP = 8
N_EXPERTS = N_GROUPS * EXPERTS_PER_GROUP
TOP_K = 2
D_EXPERT = D_MODEL // 4
LN_EPS = 1e-5
DEEPNORM_ALPHA = (2.0 * DEPTH) ** 0.25
DEEPNORM_BETA = (8.0 * DEPTH) ** -0.25

kernel_name = 'hybrid_gated_parallel_mixers_hier_moe_deepnorm'


def _standardize(x):
    x32 = x.astype(jnp.float32)
    mu = jnp.mean(x32, axis=-1, keepdims=True)
    var = jnp.mean(jnp.square(x32 - mu), axis=-1, keepdims=True)
    return (x32 - mu) * lax.rsqrt(var + LN_EPS)


def _layer_norm(x, g, b):
    return (_standardize(x) * g.astype(jnp.float32) + b.astype(jnp.float32)).astype(x.dtype)


def _causal_depthwise_conv(x, w, b):
    k = w.shape[0]
    c = x.shape[-1]
    y = lax.conv_general_dilated(x, w[:, None, :].astype(x.dtype), window_strides=(1,),
                                 padding=[(k - 1, 0)], dimension_numbers=('NWC', 'WIO', 'NWC'),
                                 feature_group_count=c)
    return y + b.astype(x.dtype)


def _rope_tables(positions):
    half = RET_HEAD_DIM // 2
    inv_freq = ROPE_BASE ** (-jnp.arange(half, dtype=jnp.float32) / half)
    ang = positions.astype(jnp.float32)[..., None] * inv_freq
    return jnp.cos(ang)[:, :, None, :], jnp.sin(ang)[:, :, None, :]


def _apply_rope(t, cos, sin):
    t1, t2 = jnp.split(t.astype(jnp.float32), 2, axis=-1)
    return jnp.concatenate([t1 * cos - t2 * sin, t2 * cos + t1 * sin], axis=-1)


def _short_conv_mixer(b_gate, c_gate, xa, conv_w, conv_b):
    return b_gate * _causal_depthwise_conv(c_gate * xa, conv_w, conv_b)


def _rg_lru_mixer(xb, conv_w, conv_b, w_r, b_r, w_i, b_i, lam):
    bsz, seq, width = xb.shape
    xc = _causal_depthwise_conv(xb, conv_w, conv_b)
    xh = xc.reshape(bsz, seq, LRU_HEADS, LRU_HEAD_DIM)
    r = jax.nn.sigmoid(jnp.einsum('bshd,hde->bshe', xh, w_r).reshape(bsz, seq, width) + b_r).astype(jnp.float32)
    i = jax.nn.sigmoid(jnp.einsum('bshd,hde->bshe', xh, w_i).reshape(bsz, seq, width) + b_i).astype(jnp.float32)
    log_a = -LRU_C * r * jax.nn.softplus(-lam.astype(jnp.float32))
    a = jnp.exp(log_a)
    u = jnp.sqrt(-jnp.expm1(2.0 * log_a)) * (i * xc.astype(jnp.float32))

    def combine(left, right):
        a_l, h_l = left
        a_r, h_r = right
        return a_l * a_r, a_r * h_l + h_r

    _, h = lax.associative_scan(combine, (a, u), axis=1)
    return h.astype(xb.dtype)


def _chunked_spatial_gating(zu, zv, norm_g, w_s, b_s):
    bsz, seq, width = zu.shape
    n_chunks = seq // SG_CHUNK
    u = jax.nn.gelu(zu)
    v = jax.nn.gelu(zv).reshape(bsz, seq, SG_GROUPS, SG_GROUP_DIM)
    v = _standardize(v) * norm_g.astype(jnp.float32).reshape(SG_GROUPS, SG_GROUP_DIM)
    v = v.reshape(bsz, n_chunks, SG_CHUNK, SG_GROUPS, SG_GROUP_DIM)
    causal = jnp.tril(jnp.ones((SG_CHUNK, SG_CHUNK), jnp.float32))
    sv = jnp.einsum('gts,bcsgd->bctgd', w_s.astype(jnp.float32) * causal, v) \
        + b_s.astype(jnp.float32).T[:, :, None]
    return u * sv.reshape(bsz, seq, width).astype(zu.dtype)


def _retention(zq, zk, zv, zg, norm_g, cos, sin):
    bsz, seq, width = zq.shape
    n_chunks = seq // RET_CHUNK
    shp = (bsz, seq, RET_HEADS, RET_HEAD_DIM)
    cshp = (bsz, n_chunks, RET_CHUNK, RET_HEADS, RET_HEAD_DIM)
    q = _apply_rope(zq.reshape(shp), cos, sin).reshape(cshp)
    k = (_apply_rope(zk.reshape(shp), cos, sin) * RET_HEAD_DIM ** -0.5).reshape(cshp)
    v = zv.astype(jnp.float32).reshape(cshp)
    log_gamma = jnp.log1p(-jnp.exp2(-5.0 - jnp.arange(RET_HEADS, dtype=jnp.float32)))
    pos = jnp.arange(RET_CHUNK, dtype=jnp.float32)
    diff = pos[:, None] - pos[None, :]
    decay = jnp.where(diff >= 0, jnp.exp(jnp.maximum(diff, 0.0) * log_gamma[:, None, None]), 0.0)
    scores = jnp.einsum('bcnhd,bcmhd->bchnm', q, k) * decay
    inner = jnp.einsum('bchnm,bcmhe->bcnhe', scores, v)
    k_decay = jnp.exp((RET_CHUNK - 1.0 - pos)[:, None] * log_gamma)
    kv = jnp.einsum('bcmhd,bcmhe,mh->bchde', k, v, k_decay)
    chunk_decay = jnp.exp(RET_CHUNK * log_gamma)[None, :, None, None]

    def step(state, kv_c):
        return chunk_decay * state + kv_c, state

    init = jnp.zeros((bsz, RET_HEADS, RET_HEAD_DIM, RET_HEAD_DIM), jnp.float32)
    _, prev = lax.scan(step, init, jnp.moveaxis(kv, 1, 0))
    prev = jnp.moveaxis(prev, 0, 1)
    q_decay = jnp.exp((pos + 1.0)[:, None] * log_gamma)
    cross = jnp.einsum('bcnhd,bchde,nh->bcnhe', q, prev, q_decay)
    o = (inner + cross).reshape(shp)
    o = (_standardize(o) * norm_g.astype(jnp.float32).reshape(RET_HEADS, RET_HEAD_DIM)).reshape(bsz, seq, width)
    return (jax.nn.silu(zg.astype(jnp.float32)) * o).astype(zq.dtype)


def _hier_moe(h, wg, bg, we, be, w_gate, w_up, w_down):
    bsz, seq, d = h.shape
    ht = h.reshape(-1, d)
    g_prob = jax.nn.softmax((ht @ wg + bg).astype(jnp.float32), axis=-1)
    g_top_p, g_idx = lax.top_k(g_prob, 1)
    e_logits = (ht @ we + be).astype(jnp.float32).reshape(-1, N_GROUPS, EXPERTS_PER_GROUP)
    sel = jnp.take_along_axis(e_logits, g_idx[:, :, None], axis=1)[:, 0]
    e_prob = jax.nn.softmax(sel, axis=-1)
    top_p, top_i = lax.top_k(e_prob, TOP_K)
    top_p = top_p / jnp.sum(top_p, axis=-1, keepdims=True)
    weights = g_top_p * top_p
    expert_id = g_idx * EXPERTS_PER_GROUP + top_i
    combine = jnp.sum(jax.nn.one_hot(expert_id, N_EXPERTS, dtype=jnp.float32) * weights[..., None], axis=1)
    out = jnp.zeros(ht.shape, jnp.float32)
    for e in range(N_EXPERTS):
        hid = jax.nn.silu(ht @ w_gate[e]) * (ht @ w_up[e])
        out = out + combine[:, e:e + 1] * (hid @ w_down[e]).astype(jnp.float32)
    return out.reshape(bsz, seq, d).astype(h.dtype)


def setup_inputs(seed: int = 0) -> dict:
    key = jax.random.key(seed)
    k = jax.random.split(key, 32)
    L, D = DEPTH, D_MODEL
    f32 = jnp.float32

    def nrm(i, shape, scale):
        return jax.random.normal(k[i], shape, f32) * scale

    n_in = 3 * SC_WIDTH + LRU_WIDTH + 2 * SG_WIDTH + 4 * RET_WIDTH + N_BRANCH * D
    u = jax.random.uniform(k[10], (L, LRU_WIDTH), f32, 0.9, 0.999)
    a0 = u ** (1.0 / LRU_C)
    return {
        'x': nrm(0, (BATCH, SEQ, D), 1.0),
        'positions': jnp.broadcast_to(jnp.arange(SEQ, dtype=jnp.int32)[None, :], (BATCH, SEQ)),
        'w_in': nrm(1, (L, D, n_in), D ** -0.5),
        'sc_conv_w': nrm(2, (L, SC_K, SC_WIDTH), SC_K ** -0.5),
        'sc_conv_b': nrm(3, (L, SC_WIDTH), 0.01),
        'lru_conv_w': nrm(4, (L, LRU_CONV_K, LRU_WIDTH), LRU_CONV_K ** -0.5),
        'lru_conv_b': nrm(5, (L, LRU_WIDTH), 0.01),
        'lru_w_r': nrm(6, (L, LRU_HEADS, LRU_HEAD_DIM, LRU_HEAD_DIM), LRU_HEAD_DIM ** -0.5),
        'lru_b_r': nrm(7, (L, LRU_WIDTH), 0.01),
        'lru_w_i': nrm(8, (L, LRU_HEADS, LRU_HEAD_DIM, LRU_HEAD_DIM), LRU_HEAD_DIM ** -0.5),
        'lru_b_i': nrm(9, (L, LRU_WIDTH), 0.01),
        'lru_lambda': jnp.log(a0) - jnp.log1p(-a0),
        'sg_norm_g': 1.0 + nrm(11, (L, SG_WIDTH), 0.02),
        'sg_w_s': nrm(12, (L, SG_GROUPS, SG_CHUNK, SG_CHUNK), SG_CHUNK ** -0.5),
        'sg_b_s': 1.0 + nrm(13, (L, SG_GROUPS, SG_CHUNK), 0.02),
        'ret_norm_g': 1.0 + nrm(14, (L, RET_WIDTH), 0.02),
        'branch_proj': nrm(15, (L, N_BRANCH, SC_WIDTH, D), SC_WIDTH ** -0.5 * DEEPNORM_BETA),
        'w_out': nrm(16, (L, D, D), D ** -0.5 * DEEPNORM_BETA),
        'ln_mix_g': 1.0 + nrm(17, (L, D), 0.02),
        'ln_mix_b': nrm(18, (L, D), 0.02),
        'router_group_w': nrm(19, (L, D, N_GROUPS), D ** -0.5),
        'router_group_b': nrm(20, (L, N_GROUPS), 0.01),
        'router_expert_w': nrm(21, (L, D, N_EXPERTS), D ** -0.5),
        'router_expert_b': nrm(22, (L, N_EXPERTS), 0.01),
        'exp_w_gate': nrm(23, (L, N_EXPERTS, D, D_EXPERT), D ** -0.5),
        'exp_w_up': nrm(24, (L, N_EXPERTS, D, D_EXPERT), D ** -0.5),
        'exp_w_down': nrm(25, (L, N_EXPERTS, D_EXPERT, D), D_EXPERT ** -0.5 * DEEPNORM_BETA),
        'ln_ffn_g': 1.0 + nrm(26, (L, D), 0.02),
        'ln_ffn_b': nrm(27, (L, D), 0.02),
    }


def reference(x, positions, w_in, sc_conv_w, sc_conv_b, lru_conv_w, lru_conv_b, lru_w_r, lru_b_r,
              lru_w_i, lru_b_i, lru_lambda, sg_norm_g, sg_w_s, sg_b_s, ret_norm_g, branch_proj, w_out,
              ln_mix_g, ln_mix_b, router_group_w, router_group_b, router_expert_w, router_expert_b,
              exp_w_gate, exp_w_up, exp_w_down, ln_ffn_g, ln_ffn_b):
    dt = x.dtype
    cos, sin = _rope_tables(positions)
    widths = [SC_WIDTH] * 3 + [LRU_WIDTH] + [SG_WIDTH] * 2 + [RET_WIDTH] * 4 + [D_MODEL] * N_BRANCH
    split_at = [int(s) for s in np.cumsum(widths)[:-1]]
    h = x
    for l in range(DEPTH):
        z = h @ w_in[l]
        (sc_b, sc_c, sc_x, lru_x, sg_u, sg_v, r_q, r_k, r_v, r_g, *gate_logits) = jnp.split(z, split_at, axis=-1)
        branches = (
            _short_conv_mixer(sc_b, sc_c, sc_x, sc_conv_w[l], sc_conv_b[l]),
            _rg_lru_mixer(lru_x, lru_conv_w[l], lru_conv_b[l], lru_w_r[l], lru_b_r[l],
                          lru_w_i[l], lru_b_i[l], lru_lambda[l]),
            _chunked_spatial_gating(sg_u, sg_v, sg_norm_g[l], sg_w_s[l], sg_b_s[l]),
            _retention(r_q, r_k, r_v, r_g, ret_norm_g[l], cos, sin),
        )
        merged = jnp.zeros(h.shape, dt)
        for b_idx in range(N_BRANCH):
            merged = merged + (jax.nn.sigmoid(gate_logits[b_idx]) * (branches[b_idx] @ branch_proj[l, b_idx])).astype(dt)
        mix = (merged @ w_out[l]).astype(dt)
        h = _layer_norm(DEEPNORM_ALPHA * h + mix, ln_mix_g[l], ln_mix_b[l])
        ffn = _hier_moe(h, router_group_w[l], router_group_b[l], router_expert_w[l], router_expert_b[l],
                        exp_w_gate[l], exp_w_up[l], exp_w_down[l])
        h = _layer_norm(DEEPNORM_ALPHA * h + ffn, ln_ffn_g[l], ln_ffn_b[l])
    return h
```

```python
import functools

import jax
import jax.numpy as jnp
from jax import lax
from jax.experimental import pallas as pl
from jax.experimental.pallas import tpu as pltpu

F32 = jnp.float32
BF16 = jnp.bfloat16

LANES = 128
SUBLANES = 8
VMEM_LIMIT_BYTES = 56 * 1024 * 1024

D_MODEL = 1024
WIDTH = D_MODEL // 4
N_HEADS = 4
HEAD_DIM = WIDTH // N_HEADS
CHUNK = 128
SC_K = 3
LRU_K = 4
LRU_C = 8.0
ROPE_BASE = 10000.0
N_GROUPS = 4
EXPERTS_PER_GROUP = 8
N_EXPERTS = N_GROUPS * EXPERTS_PER_GROUP
D_EXPERT = D_MODEL // 4
LN_EPS = 1e-5
DEPTH = 2
ALPHA = (2.0 * DEPTH) ** 0.25

COL_SC_B, COL_SC_C, COL_SC_X, COL_LRU, COL_SG_U, COL_SG_V, COL_Q, COL_K, COL_V, COL_G = (
    i * WIDTH for i in range(10))
COL_GATES = 10 * WIDTH
N_IN = COL_GATES + 4 * D_MODEL

TS = 256
SCAN_PAD = TS // 2
CONV_PAD = SUBLANES
TB_ROUTE = 512
TB_ROWS = 256
TM = 256


def _sigmoid(x):
    return 0.5 * jnp.tanh(0.5 * x) + 0.5


def _dot(a, b):
    return jnp.dot(a, b, preferred_element_type=F32)


def _layer_norm_rows(y, g, b):
    mu = jnp.mean(y, axis=-1, keepdims=True)
    yc = y - mu
    var = jnp.mean(yc * yc, axis=-1, keepdims=True)
    return yc * lax.rsqrt(var + LN_EPS) * g + b


def _rope_table_kernel(pos_ref, freq_ref, cos_ref, sin_ref):
    ang = pos_ref[...].astype(F32) * freq_ref[...]
    lane = lax.broadcasted_iota(jnp.int32, ang.shape, 1)
    first_half = (lane % HEAD_DIM) < (HEAD_DIM // 2)
    cos_ref[...] = jnp.cos(ang)
    s = jnp.sin(ang)
    sin_ref[...] = jnp.where(first_half, -s, s)


def _rope_tables(positions):
    t = positions.size
    half = HEAD_DIM // 2
    inv_freq = ROPE_BASE ** (-jnp.arange(half, dtype=F32) / half)
    freq_row = jnp.tile(inv_freq, LANES // half)[None, :]
    rows = 1024
    return pl.pallas_call(
        _rope_table_kernel,
        out_shape=(jax.ShapeDtypeStruct((t, LANES), F32), jax.ShapeDtypeStruct((t, LANES), F32)),
        grid=(t // rows,),
        in_specs=[pl.BlockSpec((rows, 1), lambda i: (i, 0)),
                  pl.BlockSpec((1, LANES), lambda i: (0, 0))],
        out_specs=(pl.BlockSpec((rows, LANES), lambda i: (i, 0)),
                   pl.BlockSpec((rows, LANES), lambda i: (i, 0))),
        name="rope_tables",
    )(positions.reshape(t, 1), freq_row)


def _mixer_kernel(h_ref, cos_ref, sin_ref, w_in_ref, scw_ref, scb_ref, lcw_ref, lcb_ref,
                  wri_ref, bri_ref, lam_ref, sgg_ref, ws_ref, bs_ref, rg_ref,
                  dec_ref, qd_ref, kd_ref, cd_ref, avg_ref, bp_ref, wo_ref, lng_ref, lnb_ref,
                  out_ref, ubuf, xbuf, hcar, sbd, abuf, hbuf):
    step = pl.program_id(1)

    @pl.when(step == 0)
    def _():
        ubuf[0:CONV_PAD, :] = jnp.zeros((CONV_PAD, WIDTH), F32)
        xbuf[0:CONV_PAD, :] = jnp.zeros((CONV_PAD, WIDTH), F32)
        hcar[...] = jnp.zeros_like(hcar)
        sbd[...] = jnp.zeros_like(sbd)
        abuf[0:SCAN_PAD, :] = jnp.ones((SCAN_PAD, WIDTH), F32)
        hbuf[0:SCAN_PAD, :] = jnp.zeros((SCAN_PAD, WIDTH), F32)

    h = h_ref[...]
    hb = h.astype(BF16)

    def proj(col, width=WIDTH):
        return _dot(hb, w_in_ref[:, col:col + width])

    lane = lax.broadcasted_iota(jnp.int32, (CHUNK, WIDTH), 1)
    head_of_lane = lane // HEAD_DIM
    avg = avg_ref[...]

    def group_standardize(x):
        mean = _dot(x.astype(BF16), avg)
        xc = x - mean
        var = _dot((xc * xc).astype(BF16), avg)
        return xc * lax.rsqrt(var + LN_EPS)

    sc_b, sc_c, sc_x = proj(COL_SC_B), proj(COL_SC_C), proj(COL_SC_X)
    u = sc_c * sc_x
    ubuf[CONV_PAD:CONV_PAD + TS, :] = u
    conv = scw_ref[2:3, :] * u + scb_ref[...]
    for j in range(SC_K - 1):
        back = SC_K - 1 - j
        conv = conv + scw_ref[j:j + 1, :] * ubuf[CONV_PAD - back:CONV_PAD - back + TS, :]
    ubuf[0:CONV_PAD, :] = ubuf[TS:TS + CONV_PAD, :]
    branch_a = sc_b * conv

    lx = proj(COL_LRU)
    xbuf[CONV_PAD:CONV_PAD + TS, :] = lx
    xc = lcw_ref[LRU_K - 1:LRU_K, :] * lx + lcb_ref[...]
    for j in range(LRU_K - 1):
        back = LRU_K - 1 - j
        xc = xc + lcw_ref[j:j + 1, :] * xbuf[CONV_PAD - back:CONV_PAD - back + TS, :]
    xbuf[0:CONV_PAD, :] = xbuf[TS:TS + CONV_PAD, :]
    ri = _dot(xc.astype(BF16), wri_ref[...]) + bri_ref[...]
    r = _sigmoid(ri[:, :WIDTH])
    ig = _sigmoid(ri[:, WIDTH:])
    neg_lam = -lam_ref[...]
    softplus = jnp.maximum(neg_lam, 0.0) + jnp.log1p(jnp.exp(-jnp.abs(neg_lam)))
    log_a = (-LRU_C) * r * softplus
    a = jnp.exp(log_a)
    th = jnp.tanh(log_a)
    uu = jnp.sqrt((-2.0) * th / (1.0 - th)) * (ig * xc)
    row = lax.broadcasted_iota(jnp.int32, (TS, WIDTH), 0)
    uu = uu + jnp.where(row == 0, a * hcar[...], 0.0)
    abuf[SCAN_PAD:SCAN_PAD + TS, :] = a
    hbuf[SCAN_PAD:SCAN_PAD + TS, :] = uu
    shift = 1
    while shift < TS:
        a_cur = abuf[SCAN_PAD:SCAN_PAD + TS, :]
        h_prev = hbuf[SCAN_PAD - shift:SCAN_PAD - shift + TS, :]
        h_new = a_cur * h_prev + hbuf[SCAN_PAD:SCAN_PAD + TS, :]
        if shift * 2 < TS:
            a_new = a_cur * abuf[SCAN_PAD - shift:SCAN_PAD - shift + TS, :]
            abuf[SCAN_PAD:SCAN_PAD + TS, :] = a_new
        hbuf[SCAN_PAD:SCAN_PAD + TS, :] = h_new
        shift *= 2
    branch_b = hbuf[SCAN_PAD:SCAN_PAD + TS, :]
    hcar[...] = hbuf[SCAN_PAD + TS - 1:SCAN_PAD + TS, :]

    gu = jax.nn.gelu(proj(COL_SG_U))
    gv = jax.nn.gelu(proj(COL_SG_V))
    vn = (group_standardize(gv) * sgg_ref[...]).astype(BF16)
    trow = lax.broadcasted_iota(jnp.int32, (CHUNK, CHUNK), 0)
    tcol = lax.broadcasted_iota(jnp.int32, (CHUNK, CHUNK), 1)
    w_causal = [jnp.where(trow >= tcol, ws_ref[g], 0.0).astype(BF16) for g in range(N_HEADS)]
    sv_chunks = []
    for c in range(TS // CHUNK):
        vch = vn[c * CHUNK:(c + 1) * CHUNK, :]
        sv = bs_ref[...]
        for g in range(N_HEADS):
            sv = sv + jnp.where(head_of_lane == g, _dot(w_causal[g], vch), 0.0)
        sv_chunks.append(sv)
    branch_c = gu * jnp.concatenate(sv_chunks, axis=0)

    cos = jnp.concatenate([cos_ref[...]] * (WIDTH // LANES), axis=1)
    sin = jnp.concatenate([sin_ref[...]] * (WIDTH // LANES), axis=1)
    lane_ts = lax.broadcasted_iota(jnp.int32, (TS, WIDTH), 1)
    first_half = (lane_ts % HEAD_DIM) < (HEAD_DIM // 2)

    def rope(t):
        swapped = jnp.where(first_half,
                            pltpu.roll(t, WIDTH - HEAD_DIM // 2, 1),
                            pltpu.roll(t, HEAD_DIM // 2, 1))
        return t * cos + swapped * sin

    q = rope(proj(COL_Q))
    k = rope(proj(COL_K)) * (HEAD_DIM ** -0.5)
    v = proj(COL_V)
    zg = proj(COL_G)
    bd_mask = (lax.broadcasted_iota(jnp.int32, (WIDTH, WIDTH), 0) // HEAD_DIM
               == lax.broadcasted_iota(jnp.int32, (WIDTH, WIDTH), 1) // HEAD_DIM)
    o_chunks = []
    for c in range(TS // CHUNK):
        sl = slice(c * CHUNK, (c + 1) * CHUNK)
        qc, kc, vc = q[sl, :], k[sl, :], v[sl, :]
        qcb = qc.astype(BF16)
        kstack = jnp.concatenate(
            [jnp.where(head_of_lane == hh, kc, 0.0) for hh in range(N_HEADS)], axis=0).astype(BF16)
        vstack = jnp.concatenate(
            [jnp.where(head_of_lane == hh, vc, 0.0) for hh in range(N_HEADS)], axis=0).astype(BF16)
        scores = lax.dot_general(qcb, kstack, (((1,), (1,)), ((), ())),
                                 preferred_element_type=F32)
        inner = _dot((scores * dec_ref[...]).astype(BF16), vstack)
        state = sbd[...]
        cross = _dot(qcb, state.astype(BF16)) * qd_ref[...]
        o_chunks.append(inner + cross)
        kdec = (kc * kd_ref[...]).astype(BF16)
        kv = lax.dot_general(kdec, vc.astype(BF16), (((0,), (0,)), ((), ())),
                             preferred_element_type=F32)
        sbd[...] = cd_ref[...] * state + jnp.where(bd_mask, kv, 0.0)
    o = jnp.concatenate(o_chunks, axis=0)
    branch_d = (zg * _sigmoid(zg)) * (group_standardize(o) * rg_ref[...])

    merged = jnp.zeros((TS, D_MODEL), F32)
    for b_idx, branch in enumerate((branch_a, branch_b, branch_c, branch_d)):
        gate = _sigmoid(proj(COL_GATES + b_idx * D_MODEL, D_MODEL))
        merged = merged + gate * _dot(branch.astype(BF16), bp_ref[b_idx])
    mix = _dot(merged.astype(BF16), wo_ref[...])
    out_ref[...] = _layer_norm_rows(ALPHA * h + mix, lng_ref[...], lnb_ref[...])


def _const_spec(shape):
    nd = len(shape)
    return pl.BlockSpec(shape, lambda b, i, _nd=nd: (0,) * _nd, pipeline_mode=pl.Buffered(1))


def _mixer_layer(h, cos_t, sin_t, consts, batch, seq):
    t = batch * seq
    steps = seq // TS
    row_map = lambda b, i: (b * steps + i, 0)
    in_specs = [pl.BlockSpec((TS, D_MODEL), row_map),
                pl.BlockSpec((TS, LANES), row_map),
                pl.BlockSpec((TS, LANES), row_map)]
    in_specs += [_const_spec(c.shape) for c in consts]
    return pl.pallas_call(
        _mixer_kernel,
        out_shape=jax.ShapeDtypeStruct((t, D_MODEL), F32),
        grid=(batch, steps),
        in_specs=in_specs,
        out_specs=pl.BlockSpec((TS, D_MODEL), row_map),
        scratch_shapes=[
            pltpu.VMEM((CONV_PAD + TS, WIDTH), F32),
            pltpu.VMEM((CONV_PAD + TS, WIDTH), F32),
            pltpu.VMEM((1, WIDTH), F32),
            pltpu.VMEM((WIDTH, WIDTH), F32),
            pltpu.VMEM((SCAN_PAD + TS, WIDTH), F32),
            pltpu.VMEM((SCAN_PAD + TS, WIDTH), F32),
        ],
        compiler_params=pltpu.CompilerParams(
            dimension_semantics=("arbitrary", "arbitrary"),
            vmem_limit_bytes=VMEM_LIMIT_BYTES),
        name="mixer_layer",
    )(h, cos_t, sin_t, *consts)


def _block_diag(w):
    heads, d, e = w.shape
    eye = jnp.eye(heads, dtype=w.dtype)
    return (eye[:, None, :, None] * w[:, :, None, :]).reshape(heads * d, heads * e)


def _retention_tables():
    log_gamma = jnp.log1p(-jnp.exp2(-5.0 - jnp.arange(N_HEADS, dtype=F32)))
    pos = jnp.arange(CHUNK, dtype=F32)
    diff = pos[:, None] - pos[None, :]
    decay = jnp.where(diff >= 0, jnp.exp(jnp.maximum(diff, 0.0) * log_gamma[:, None, None]), 0.0)
    dec_all = jnp.transpose(decay, (1, 0, 2)).reshape(CHUNK, N_HEADS * CHUNK)
    q_decay = jnp.exp((pos + 1.0)[:, None] * log_gamma)
    k_decay = jnp.exp((CHUNK - 1.0 - pos)[:, None] * log_gamma)
    qd_tab = jnp.repeat(q_decay, HEAD_DIM, axis=1)
    kd_tab = jnp.repeat(k_decay, HEAD_DIM, axis=1)
    chunk_decay = jnp.repeat(jnp.exp(CHUNK * log_gamma), HEAD_DIM)
    cd_tab = jnp.broadcast_to(chunk_decay[:, None], (WIDTH, WIDTH))
    return dec_all, qd_tab, kd_tab, cd_tab


def _router_kernel(h_ref, whi_ref, wlo_ref, b_ref, meta_ref, wts_ref, cnt_ref, carry):
    step = pl.program_id(0)

    @pl.when(step == 0)
    def _():
        carry[...] = jnp.zeros_like(carry)

    h = h_ref[...]
    h_hi = h.astype(BF16)
    h_lo = (h - h_hi.astype(F32)).astype(BF16)
    whi = whi_ref[...]
    logits = _dot(h_hi, whi) + _dot(h_lo, whi) + _dot(h_hi, wlo_ref[...]) + b_ref[...]
    lane = lax.broadcasted_iota(jnp.int32, logits.shape, 1)
    big = jnp.int32(2 ** 30)
    neg_inf = F32(-jnp.inf)

    is_group = lane < N_GROUPS
    gl = jnp.where(is_group, logits, neg_inf)
    gmax = jnp.max(gl, axis=-1, keepdims=True)
    gsum = jnp.sum(jnp.where(is_group, jnp.exp(gl - gmax), 0.0), axis=-1, keepdims=True)
    g_top_p = 1.0 / gsum
    g_idx = jnp.min(jnp.where(gl == gmax, lane, big), axis=-1, keepdims=True)

    e_lane = lane - N_GROUPS
    in_group = (e_lane >= g_idx * EXPERTS_PER_GROUP) & (e_lane < (g_idx + 1) * EXPERTS_PER_GROUP)
    el = jnp.where(in_group, logits, neg_inf)
    m1 = jnp.max(el, axis=-1, keepdims=True)
    i1 = jnp.min(jnp.where(el == m1, lane, big), axis=-1, keepdims=True)
    el2 = jnp.where(lane == i1, neg_inf, el)
    m2 = jnp.max(el2, axis=-1, keepdims=True)
    i2 = jnp.min(jnp.where(el2 == m2, lane, big), axis=-1, keepdims=True)
    esum = jnp.sum(jnp.where(in_group, jnp.exp(el - m1), 0.0), axis=-1, keepdims=True)
    p1 = 1.0 / esum
    p2 = jnp.exp(m2 - m1) / esum
    psum = p1 + p2
    w1 = g_top_p * (p1 / psum)
    w2 = g_top_p * (p2 / psum)
    e1 = i1 - N_GROUPS
    e2 = i2 - N_GROUPS

    onehot = (lane == e1).astype(F32) + (lane == e2).astype(F32)
    tb = h.shape[0]
    r_i = lax.broadcasted_iota(jnp.int32, (tb, tb), 0)
    c_i = lax.broadcasted_iota(jnp.int32, (tb, tb), 1)
    strict_lower = jnp.where(r_i > c_i, 1.0, 0.0).astype(BF16)
    before = _dot(strict_lower, onehot.astype(BF16)) + carry[...]
    rank1 = jnp.sum(jnp.where(lane == e1, before, 0.0), axis=-1, keepdims=True)
    rank2 = jnp.sum(jnp.where(lane == e2, before, 0.0), axis=-1, keepdims=True)
    carry[...] = carry[...] + jnp.sum(onehot, axis=0, keepdims=True)

    meta = jnp.where(lane == 0, e1, 0) + jnp.where(lane == 1, e2, 0)
    meta = meta + jnp.where(lane == 2, rank1.astype(jnp.int32), 0)
    meta = meta + jnp.where(lane == 3, rank2.astype(jnp.int32), 0)
    meta_ref[...] = meta
    wts_ref[...] = jnp.where(lane == 0, w1, 0.0) + jnp.where(lane == 1, w2, 0.0)
    cnt_ref[...] = jnp.broadcast_to(carry[...], cnt_ref.shape).astype(jnp.int32)


def _router(h, whi, wlo, bias):
    t = h.shape[0]
    return pl.pallas_call(
        _router_kernel,
        out_shape=(jax.ShapeDtypeStruct((t, LANES), jnp.int32),
                   jax.ShapeDtypeStruct((t, LANES), F32),
                   jax.ShapeDtypeStruct((SUBLANES, LANES), jnp.int32)),
        grid=(t // TB_ROUTE,),
        in_specs=[pl.BlockSpec((TB_ROUTE, D_MODEL), lambda i: (i, 0)),
                  pl.BlockSpec((D_MODEL, LANES), lambda i: (0, 0)),
                  pl.BlockSpec((D_MODEL, LANES), lambda i: (0, 0)),
                  pl.BlockSpec((1, LANES), lambda i: (0, 0))],
        out_specs=(pl.BlockSpec((TB_ROUTE, LANES), lambda i: (i, 0)),
                   pl.BlockSpec((TB_ROUTE, LANES), lambda i: (i, 0)),
                   pl.BlockSpec((SUBLANES, LANES), lambda i: (0, 0))),
        scratch_shapes=[pltpu.VMEM((1, LANES), F32)],
        compiler_params=pltpu.CompilerParams(dimension_semantics=("arbitrary",)),
        name="router",
    )(h, whi, wlo, bias)


def _dispatch_kernel(pos1_ref, pos2_ref, h_ref, xs_in_ref, xs_ref, sem):
    del xs_in_ref

    def row_copy(r, pos_ref):
        return pltpu.make_async_copy(h_ref.at[pl.ds(r, 1), :],
                                     xs_ref.at[pl.ds(pos_ref[r], 1), :], sem)

    def issue(r, _):
        row_copy(r, pos1_ref).start()
        row_copy(r, pos2_ref).start()
        return 0

    lax.fori_loop(0, TB_ROWS, issue, 0)

    def drain(r, _):
        row_copy(r, pos1_ref).wait()
        row_copy(r, pos2_ref).wait()
        return 0

    lax.fori_loop(0, TB_ROWS, drain, 0)


def _dispatch(h, pos1, pos2, n_rows):
    t = h.shape[0]
    xs0 = jnp.zeros((n_rows, D_MODEL), F32)
    smem_rows = pl.BlockSpec((TB_ROWS,), lambda i: (i,), memory_space=pltpu.SMEM)
    return pl.pallas_call(
        _dispatch_kernel,
        out_shape=jax.ShapeDtypeStruct((n_rows, D_MODEL), F32),
        grid=(t // TB_ROWS,),
        in_specs=[smem_rows, smem_rows,
                  pl.BlockSpec((TB_ROWS, D_MODEL), lambda i: (i, 0)),
                  pl.BlockSpec(memory_space=pl.ANY)],
        out_specs=pl.BlockSpec(memory_space=pl.ANY),
        scratch_shapes=[pltpu.SemaphoreType.DMA],
        input_output_aliases={3: 0},
        compiler_params=pltpu.CompilerParams(dimension_semantics=("arbitrary",)),
        name="moe_dispatch",
    )(pos1, pos2, h, xs0)


def _expert_kernel(te_ref, nvalid_ref, x_ref, wgu_ref, wd_ref, y_ref):
    del te_ref
    live = pl.program_id(0) < nvalid_ref[0]

    @pl.when(live)
    def _():
        gu = _dot(x_ref[...].astype(BF16), wgu_ref[0])
        gate, up = gu[:, :D_EXPERT], gu[:, D_EXPERT:]
        hid = (gate * _sigmoid(gate)) * up
        y_ref[...] = _dot(hid.astype(BF16), wd_ref[0])

    @pl.when(jnp.logical_not(live))
    def _():
        y_ref[...] = jnp.zeros_like(y_ref)


def _experts(xs, tile_expert, n_valid, wgu, wd):
    n_rows = xs.shape[0]
    grid_spec = pltpu.PrefetchScalarGridSpec(
        num_scalar_prefetch=2,
        grid=(n_rows // TM,),
        in_specs=[pl.BlockSpec((TM, D_MODEL), lambda j, te, nv: (j, 0)),
                  pl.BlockSpec((1, D_MODEL, 2 * D_EXPERT), lambda j, te, nv: (te[j], 0, 0)),
                  pl.BlockSpec((1, D_EXPERT, D_MODEL), lambda j, te, nv: (te[j], 0, 0))],
        out_specs=pl.BlockSpec((TM, D_MODEL), lambda j, te, nv: (j, 0)),
    )
    return pl.pallas_call(
        _expert_kernel,
        out_shape=jax.ShapeDtypeStruct((n_rows, D_MODEL), F32),
        grid_spec=grid_spec,
        compiler_params=pltpu.CompilerParams(dimension_semantics=("arbitrary",)),
        name="moe_experts",
    )(tile_expert, n_valid, xs, wgu, wd)


def _combine_kernel(pos1_ref, pos2_ref, h_ref, wts_ref, g_ref, b_ref, y_ref, out_ref,
                    buf1, buf2, sem):
    def row_copies(r):
        return (pltpu.make_async_copy(y_ref.at[pl.ds(pos1_ref[r], 1), :],
                                      buf1.at[pl.ds(r, 1), :], sem),
                pltpu.make_async_copy(y_ref.at[pl.ds(pos2_ref[r], 1), :],
                                      buf2.at[pl.ds(r, 1), :], sem))

    def issue(r, _):
        c1, c2 = row_copies(r)
        c1.start()
        c2.start()
        return 0

    lax.fori_loop(0, TB_ROWS, issue, 0)

    def drain(r, _):
        c1, c2 = row_copies(r)
        c1.wait()
        c2.wait()
        return 0

    lax.fori_loop(0, TB_ROWS, drain, 0)

    wts = wts_ref[...]
    ffn = wts[:, 0:1] * buf1[...] + wts[:, 1:2] * buf2[...]
    out_ref[...] = _layer_norm_rows(ALPHA * h_ref[...] + ffn, g_ref[...], b_ref[...])


def _combine(h, wts, pos1, pos2, y, ln_g, ln_b):
    t = h.shape[0]
    smem_rows = pl.BlockSpec((TB_ROWS,), lambda i: (i,), memory_space=pltpu.SMEM)
    return pl.pallas_call(
        _combine_kernel,
        out_shape=jax.ShapeDtypeStruct((t, D_MODEL), F32),
        grid=(t // TB_ROWS,),
        in_specs=[smem_rows, smem_rows,
                  pl.BlockSpec((TB_ROWS, D_MODEL), lambda i: (i, 0)),
                  pl.BlockSpec((TB_ROWS, LANES), lambda i: (i, 0)),
                  pl.BlockSpec((1, D_MODEL), lambda i: (0, 0)),
                  pl.BlockSpec((1, D_MODEL), lambda i: (0, 0)),
                  pl.BlockSpec(memory_space=pl.ANY)],
        out_specs=pl.BlockSpec((TB_ROWS, D_MODEL), lambda i: (i, 0)),
        scratch_shapes=[pltpu.VMEM((TB_ROWS, D_MODEL), F32),
                        pltpu.VMEM((TB_ROWS, D_MODEL), F32),
                        pltpu.SemaphoreType.DMA],
        compiler_params=pltpu.CompilerParams(dimension_semantics=("arbitrary",)),
        name="moe_combine",
    )(pos1, pos2, h, wts, ln_g, ln_b, y)


def _moe_layer(h, wg, bg, we, be, w_gate, w_up, w_down, ln_g, ln_b):
    t = h.shape[0]
    n_tiles = (2 * t) // TM + N_EXPERTS
    n_rows = n_tiles * TM

    w_router = jnp.zeros((D_MODEL, LANES), F32)
    w_router = w_router.at[:, :N_GROUPS].set(wg).at[:, N_GROUPS:N_GROUPS + N_EXPERTS].set(we)
    b_router = jnp.zeros((1, LANES), F32)
    b_router = b_router.at[0, :N_GROUPS].set(bg).at[0, N_GROUPS:N_GROUPS + N_EXPERTS].set(be)
    w_hi = w_router.astype(BF16)
    w_lo = (w_router - w_hi.astype(F32)).astype(BF16)

    meta, wts, counts = _router(h, w_hi, w_lo, b_router)

    cnt = counts[0, :N_EXPERTS]
    padded = ((cnt + TM - 1) // TM) * TM
    ends = jnp.cumsum(padded)
    offs = ends - padded
    e1, e2, r1, r2 = meta[:, 0], meta[:, 1], meta[:, 2], meta[:, 3]
    pos1 = offs[e1] + r1
    pos2 = offs[e2] + r2
    tile_start = jnp.arange(n_tiles, dtype=jnp.int32) * TM
    n_valid = (ends[-1] // TM).astype(jnp.int32)
    tile_expert = jnp.sum((tile_start[:, None] >= ends[None, :]).astype(jnp.int32), axis=1)
    last_expert = tile_expert[jnp.maximum(n_valid - 1, 0)]
    tile_expert = jnp.where(tile_start < ends[-1], tile_expert, last_expert)
    tile_expert = jnp.clip(tile_expert, 0, N_EXPERTS - 1)

    xs = _dispatch(h, pos1, pos2, n_rows)
    wgu = jnp.concatenate([w_gate, w_up], axis=-1).astype(BF16)
    y = _experts(xs, tile_expert, n_valid.reshape(1), wgu, w_down.astype(BF16))
    return _combine(h, wts, pos1, pos2, y, ln_g, ln_b)


def kernel(x, positions, w_in, sc_conv_w, sc_conv_b, lru_conv_w, lru_conv_b, lru_w_r, lru_b_r,
           lru_w_i, lru_b_i, lru_lambda, sg_norm_g, sg_w_s, sg_b_s, ret_norm_g, branch_proj, w_out,
           ln_mix_g, ln_mix_b, router_group_w, router_group_b, router_expert_w, router_expert_b,
           exp_w_gate, exp_w_up, exp_w_down, ln_ffn_g, ln_ffn_b):
    batch, seq, d = x.shape
    assert d == D_MODEL and seq % TS == 0 and w_in.shape[-1] == N_IN
    depth = w_in.shape[0]
    t = batch * seq
    cos_t, sin_t = _rope_tables(positions)
    dec_all, qd_tab, kd_tab, cd_tab = _retention_tables()
    avg = _block_diag(jnp.full((N_HEADS, HEAD_DIM, HEAD_DIM), 1.0 / HEAD_DIM, F32)).astype(BF16)

    h = x.reshape(t, d)
    for l in range(depth):
        w_ri = jnp.concatenate([_block_diag(lru_w_r[l]), _block_diag(lru_w_i[l])], axis=1).astype(BF16)
        b_ri = jnp.concatenate([lru_b_r[l], lru_b_i[l]])[None, :]
        bs_tab = jnp.repeat(sg_b_s[l].T, HEAD_DIM, axis=1)
        consts = (
            w_in[l].astype(BF16), sc_conv_w[l], sc_conv_b[l][None, :], lru_conv_w[l], lru_conv_b[l][None, :],
            w_ri, b_ri, lru_lambda[l][None, :], sg_norm_g[l][None, :], sg_w_s[l], bs_tab,
            ret_norm_g[l][None, :], dec_all, qd_tab, kd_tab, cd_tab, avg,
            branch_proj[l].astype(BF16), w_out[l].astype(BF16), ln_mix_g[l][None, :], ln_mix_b[l][None, :],
        )
        h = _mixer_layer(h, cos_t, sin_t, consts, batch, seq)
        h = _moe_layer(h, router_group_w[l], router_group_b[l], router_expert_w[l], router_expert_b[l],
                       exp_w_gate[l], exp_w_up[l], exp_w_down[l], ln_ffn_g[l][None, :], ln_ffn_b[l][None, :])
    return h.reshape(batch, seq, d)
```

```python
import functools

import jax
import jax.numpy as jnp
from jax import lax
from jax.experimental import pallas as pl
from jax.experimental.pallas import tpu as pltpu

F32 = jnp.float32
BF16 = jnp.bfloat16

LANES = 128
SUBLANES = 8
VMEM_LIMIT_BYTES = 56 * 1024 * 1024

D_MODEL = 1024
WIDTH = D_MODEL // 4
N_HEADS = 4
HEAD_DIM = WIDTH // N_HEADS
CHUNK = 128
SC_K = 3
LRU_K = 4
LRU_C = 8.0
ROPE_BASE = 10000.0
N_GROUPS = 4
EXPERTS_PER_GROUP = 8
N_EXPERTS = N_GROUPS * EXPERTS_PER_GROUP
D_EXPERT = D_MODEL // 4
LN_EPS = 1e-5
DEPTH = 2
ALPHA = (2.0 * DEPTH) ** 0.25

COL_SC_B, COL_SC_C, COL_SC_X, COL_LRU, COL_SG_U, COL_SG_V, COL_Q, COL_K, COL_V, COL_G = (
    i * WIDTH for i in range(10))
COL_GATES = 10 * WIDTH
N_IN = COL_GATES + 4 * D_MODEL

TS = 256
SCAN_PAD = TS // 2
CONV_PAD = SUBLANES
TB_ROUTE = 512
TB_PERM = 2048
TM = 128
PAIRS_PER_GROUP = EXPERTS_PER_GROUP * (EXPERTS_PER_GROUP - 1) // 2
N_CLASSES = N_GROUPS * PAIRS_PER_GROUP
assert N_CLASSES <= LANES


def _sigmoid(x):
    return 0.5 * jnp.tanh(0.5 * x) + 0.5


def _dot(a, b):
    return jnp.dot(a, b, preferred_element_type=F32)


def _layer_norm_rows(y, g, b):
    mu = jnp.mean(y, axis=-1, keepdims=True)
    yc = y - mu
    var = jnp.mean(yc * yc, axis=-1, keepdims=True)
    return yc * lax.rsqrt(var + LN_EPS) * g + b


def _rope_table_kernel(pos_ref, freq_ref, cos_ref, sin_ref):
    ang = pos_ref[...].astype(F32) * freq_ref[...]
    lane = lax.broadcasted_iota(jnp.int32, ang.shape, 1)
    first_half = (lane % HEAD_DIM) < (HEAD_DIM // 2)
    cos_ref[...] = jnp.cos(ang)
    s = jnp.sin(ang)
    sin_ref[...] = jnp.where(first_half, -s, s)


def _rope_tables(positions):
    t = positions.size
    half = HEAD_DIM // 2
    inv_freq = ROPE_BASE ** (-jnp.arange(half, dtype=F32) / half)
    freq_row = jnp.tile(inv_freq, LANES // half)[None, :]
    rows = 1024
    return pl.pallas_call(
        _rope_table_kernel,
        out_shape=(jax.ShapeDtypeStruct((t, LANES), F32), jax.ShapeDtypeStruct((t, LANES), F32)),
        grid=(t // rows,),
        in_specs=[pl.BlockSpec((rows, 1), lambda i: (i, 0)),
                  pl.BlockSpec((1, LANES), lambda i: (0, 0))],
        out_specs=(pl.BlockSpec((rows, LANES), lambda i: (i, 0)),
                   pl.BlockSpec((rows, LANES), lambda i: (i, 0))),
        name="rope_tables",
    )(positions.reshape(t, 1), freq_row)


def _mixer_kernel(h_ref, cos_ref, sin_ref, w_in_ref, scw_ref, scb_ref, lcw_ref, lcb_ref,
                  wri_ref, bri_ref, lam_ref, sgg_ref, ws_ref, bs_ref, rg_ref,
                  dec_ref, qd_ref, kd_ref, cd_ref, avg_ref, bp_ref, wo_ref, lng_ref, lnb_ref,
                  out_ref, ubuf, xbuf, hcar, sbd, abuf, hbuf):
    step = pl.program_id(1)

    @pl.when(step == 0)
    def _():
        ubuf[0:CONV_PAD, :] = jnp.zeros((CONV_PAD, WIDTH), F32)
        xbuf[0:CONV_PAD, :] = jnp.zeros((CONV_PAD, WIDTH), F32)
        hcar[...] = jnp.zeros_like(hcar)
        sbd[...] = jnp.zeros_like(sbd)
        abuf[0:SCAN_PAD, :] = jnp.ones((SCAN_PAD, WIDTH), F32)
        hbuf[0:SCAN_PAD, :] = jnp.zeros((SCAN_PAD, WIDTH), F32)

    h = h_ref[...]
    hb = h.astype(BF16)

    def proj(col, width=WIDTH):
        return _dot(hb, w_in_ref[:, col:col + width])

    lane = lax.broadcasted_iota(jnp.int32, (CHUNK, WIDTH), 1)
    head_of_lane = lane // HEAD_DIM
    avg = avg_ref[...]

    def group_standardize(x):
        mean = _dot(x.astype(BF16), avg)
        xc = x - mean
        var = _dot((xc * xc).astype(BF16), avg)
        return xc * lax.rsqrt(var + LN_EPS)

    sc_b, sc_c, sc_x = proj(COL_SC_B), proj(COL_SC_C), proj(COL_SC_X)
    u = sc_c * sc_x
    ubuf[CONV_PAD:CONV_PAD + TS, :] = u
    conv = scw_ref[2:3, :] * u + scb_ref[...]
    for j in range(SC_K - 1):
        back = SC_K - 1 - j
        conv = conv + scw_ref[j:j + 1, :] * ubuf[CONV_PAD - back:CONV_PAD - back + TS, :]
    ubuf[0:CONV_PAD, :] = ubuf[TS:TS + CONV_PAD, :]
    branch_a = sc_b * conv

    lx = proj(COL_LRU)
    xbuf[CONV_PAD:CONV_PAD + TS, :] = lx
    xc = lcw_ref[LRU_K - 1:LRU_K, :] * lx + lcb_ref[...]
    for j in range(LRU_K - 1):
        back = LRU_K - 1 - j
        xc = xc + lcw_ref[j:j + 1, :] * xbuf[CONV_PAD - back:CONV_PAD - back + TS, :]
    xbuf[0:CONV_PAD, :] = xbuf[TS:TS + CONV_PAD, :]
    ri = _dot(xc.astype(BF16), wri_ref[...]) + bri_ref[...]
    r = _sigmoid(ri[:, :WIDTH])
    ig = _sigmoid(ri[:, WIDTH:])
    neg_lam = -lam_ref[...]
    softplus = jnp.maximum(neg_lam, 0.0) + jnp.log1p(jnp.exp(-jnp.abs(neg_lam)))
    log_a = (-LRU_C) * r * softplus
    a = jnp.exp(log_a)
    th = jnp.tanh(log_a)
    uu = jnp.sqrt((-2.0) * th / (1.0 - th)) * (ig * xc)
    row = lax.broadcasted_iota(jnp.int32, (TS, WIDTH), 0)
    uu = uu + jnp.where(row == 0, a * hcar[...], 0.0)
    abuf[SCAN_PAD:SCAN_PAD + TS, :] = a
    hbuf[SCAN_PAD:SCAN_PAD + TS, :] = uu
    shift = 1
    while shift < TS:
        a_cur = abuf[SCAN_PAD:SCAN_PAD + TS, :]
        h_prev = hbuf[SCAN_PAD - shift:SCAN_PAD - shift + TS, :]
        h_new = a_cur * h_prev + hbuf[SCAN_PAD:SCAN_PAD + TS, :]
        if shift * 2 < TS:
            a_new = a_cur * abuf[SCAN_PAD - shift:SCAN_PAD - shift + TS, :]
            abuf[SCAN_PAD:SCAN_PAD + TS, :] = a_new
        hbuf[SCAN_PAD:SCAN_PAD + TS, :] = h_new
        shift *= 2
    branch_b = hbuf[SCAN_PAD:SCAN_PAD + TS, :]
    hcar[...] = hbuf[SCAN_PAD + TS - 1:SCAN_PAD + TS, :]

    gu = jax.nn.gelu(proj(COL_SG_U))
    gv = jax.nn.gelu(proj(COL_SG_V))
    vn = (group_standardize(gv) * sgg_ref[...]).astype(BF16)
    trow = lax.broadcasted_iota(jnp.int32, (CHUNK, CHUNK), 0)
    tcol = lax.broadcasted_iota(jnp.int32, (CHUNK, CHUNK), 1)
    w_causal = [jnp.where(trow >= tcol, ws_ref[g], 0.0).astype(BF16) for g in range(N_HEADS)]
    sv_chunks = []
    for c in range(TS // CHUNK):
        vch = vn[c * CHUNK:(c + 1) * CHUNK, :]
        sv = bs_ref[...]
        for g in range(N_HEADS):
            sv = sv + jnp.where(head_of_lane == g, _dot(w_causal[g], vch), 0.0)
        sv_chunks.append(sv)
    branch_c = gu * jnp.concatenate(sv_chunks, axis=0)

    cos = jnp.concatenate([cos_ref[...]] * (WIDTH // LANES), axis=1)
    sin = jnp.concatenate([sin_ref[...]] * (WIDTH // LANES), axis=1)
    lane_ts = lax.broadcasted_iota(jnp.int32, (TS, WIDTH), 1)
    first_half = (lane_ts % HEAD_DIM) < (HEAD_DIM // 2)

    def rope(t):
        swapped = jnp.where(first_half,
                            pltpu.roll(t, WIDTH - HEAD_DIM // 2, 1),
                            pltpu.roll(t, HEAD_DIM // 2, 1))
        return t * cos + swapped * sin

    q = rope(proj(COL_Q))
    k = rope(proj(COL_K)) * (HEAD_DIM ** -0.5)
    v = proj(COL_V)
    zg = proj(COL_G)
    bd_mask = (lax.broadcasted_iota(jnp.int32, (WIDTH, WIDTH), 0) // HEAD_DIM
               == lax.broadcasted_iota(jnp.int32, (WIDTH, WIDTH), 1) // HEAD_DIM)
    o_chunks = []
    for c in range(TS // CHUNK):
        sl = slice(c * CHUNK, (c + 1) * CHUNK)
        qc, kc, vc = q[sl, :], k[sl, :], v[sl, :]
        qcb = qc.astype(BF16)
        kstack = jnp.concatenate(
            [jnp.where(head_of_lane == hh, kc, 0.0) for hh in range(N_HEADS)], axis=0).astype(BF16)
        vstack = jnp.concatenate(
            [jnp.where(head_of_lane == hh, vc, 0.0) for hh in range(N_HEADS)], axis=0).astype(BF16)
        scores = lax.dot_general(qcb, kstack, (((1,), (1,)), ((), ())),
                                 preferred_element_type=F32)
        inner = _dot((scores * dec_ref[...]).astype(BF16), vstack)
        state = sbd[...]
        cross = _dot(qcb, state.astype(BF16)) * qd_ref[...]
        o_chunks.append(inner + cross)
        kdec = (kc * kd_ref[...]).astype(BF16)
        kv = lax.dot_general(kdec, vc.astype(BF16), (((0,), (0,)), ((), ())),
                             preferred_element_type=F32)
        sbd[...] = cd_ref[...] * state + jnp.where(bd_mask, kv, 0.0)
    o = jnp.concatenate(o_chunks, axis=0)
    branch_d = (zg * _sigmoid(zg)) * (group_standardize(o) * rg_ref[...])

    merged = jnp.zeros((TS, D_MODEL), F32)
    for b_idx, branch in enumerate((branch_a, branch_b, branch_c, branch_d)):
        gate = _sigmoid(proj(COL_GATES + b_idx * D_MODEL, D_MODEL))
        merged = merged + gate * _dot(branch.astype(BF16), bp_ref[b_idx])
    mix = _dot(merged.astype(BF16), wo_ref[...])
    out_ref[...] = _layer_norm_rows(ALPHA * h + mix, lng_ref[...], lnb_ref[...])


def _const_spec(shape):
    nd = len(shape)
    return pl.BlockSpec(shape, lambda b, i, _nd=nd: (0,) * _nd, pipeline_mode=pl.Buffered(1))


def _mixer_layer(h, cos_t, sin_t, consts, batch, seq):
    t = batch * seq
    steps = seq // TS
    row_map = lambda b, i: (b * steps + i, 0)
    in_specs = [pl.BlockSpec((TS, D_MODEL), row_map),
                pl.BlockSpec((TS, LANES), row_map),
                pl.BlockSpec((TS, LANES), row_map)]
    in_specs += [_const_spec(c.shape) for c in consts]
    return pl.pallas_call(
        _mixer_kernel,
        out_shape=jax.ShapeDtypeStruct((t, D_MODEL), F32),
        grid=(batch, steps),
        in_specs=in_specs,
        out_specs=pl.BlockSpec((TS, D_MODEL), row_map),
        scratch_shapes=[
            pltpu.VMEM((CONV_PAD + TS, WIDTH), F32),
            pltpu.VMEM((CONV_PAD + TS, WIDTH), F32),
            pltpu.VMEM((1, WIDTH), F32),
            pltpu.VMEM((WIDTH, WIDTH), F32),
            pltpu.VMEM((SCAN_PAD + TS, WIDTH), F32),
            pltpu.VMEM((SCAN_PAD + TS, WIDTH), F32),
        ],
        compiler_params=pltpu.CompilerParams(
            dimension_semantics=("arbitrary", "arbitrary"),
            vmem_limit_bytes=VMEM_LIMIT_BYTES),
        name="mixer_layer",
    )(h, cos_t, sin_t, *consts)


def _block_diag(w):
    heads, d, e = w.shape
    eye = jnp.eye(heads, dtype=w.dtype)
    return (eye[:, None, :, None] * w[:, :, None, :]).reshape(heads * d, heads * e)


def _retention_tables():
    log_gamma = jnp.log1p(-jnp.exp2(-5.0 - jnp.arange(N_HEADS, dtype=F32)))
    pos = jnp.arange(CHUNK, dtype=F32)
    diff = pos[:, None] - pos[None, :]
    decay = jnp.where(diff >= 0, jnp.exp(jnp.maximum(diff, 0.0) * log_gamma[:, None, None]), 0.0)
    dec_all = jnp.transpose(decay, (1, 0, 2)).reshape(CHUNK, N_HEADS * CHUNK)
    q_decay = jnp.exp((pos + 1.0)[:, None] * log_gamma)
    k_decay = jnp.exp((CHUNK - 1.0 - pos)[:, None] * log_gamma)
    qd_tab = jnp.repeat(q_decay, HEAD_DIM, axis=1)
    kd_tab = jnp.repeat(k_decay, HEAD_DIM, axis=1)
    chunk_decay = jnp.repeat(jnp.exp(CHUNK * log_gamma), HEAD_DIM)
    cd_tab = jnp.broadcast_to(chunk_decay[:, None], (WIDTH, WIDTH))
    return dec_all, qd_tab, kd_tab, cd_tab


def _router_kernel(h_ref, whi_ref, wlo_ref, b_ref, meta_ref, cnt_ref, carry):
    step = pl.program_id(0)

    @pl.when(step == 0)
    def _():
        carry[...] = jnp.zeros_like(carry)

    h = h_ref[...]
    h_hi = h.astype(BF16)
    h_lo = (h - h_hi.astype(F32)).astype(BF16)
    whi = whi_ref[...]
    logits = _dot(h_hi, whi) + _dot(h_lo, whi) + _dot(h_hi, wlo_ref[...]) + b_ref[...]
    lane = lax.broadcasted_iota(jnp.int32, logits.shape, 1)
    big = jnp.int32(2 ** 30)
    neg_inf = F32(-jnp.inf)

    gl = jnp.where(lane < N_GROUPS, logits, neg_inf)
    gmax = jnp.max(gl, axis=-1, keepdims=True)
    g_idx = jnp.min(jnp.where(gl == gmax, lane, big), axis=-1, keepdims=True)

    e_lane = lane - N_GROUPS
    in_group = (e_lane >= g_idx * EXPERTS_PER_GROUP) & (e_lane < (g_idx + 1) * EXPERTS_PER_GROUP)
    el = jnp.where(in_group, logits, neg_inf)
    m1 = jnp.max(el, axis=-1, keepdims=True)
    i1 = jnp.min(jnp.where(el == m1, lane, big), axis=-1, keepdims=True)
    el2 = jnp.where(lane == i1, neg_inf, el)
    m2 = jnp.max(el2, axis=-1, keepdims=True)
    i2 = jnp.min(jnp.where(el2 == m2, lane, big), axis=-1, keepdims=True)
    base = (g_idx * EXPERTS_PER_GROUP + N_GROUPS).astype(F32)
    lo = jnp.minimum(i1, i2).astype(F32) - base
    hi = jnp.maximum(i1, i2).astype(F32) - base
    pair = lo * (2.0 * EXPERTS_PER_GROUP - 1.0 - lo) * 0.5 + (hi - lo - 1.0)
    cls = (g_idx.astype(F32) * PAIRS_PER_GROUP + pair).astype(jnp.int32)

    onehot = jnp.where(lane == cls, 1.0, 0.0)
    tb = h.shape[0]
    r_i = lax.broadcasted_iota(jnp.int32, (tb, tb), 0)
    c_i = lax.broadcasted_iota(jnp.int32, (tb, tb), 1)
    strict_lower = jnp.where(r_i > c_i, 1.0, 0.0).astype(BF16)
    before = _dot(strict_lower, onehot.astype(BF16)) + carry[...]
    rank = jnp.sum(jnp.where(lane == cls, before, 0.0), axis=-1, keepdims=True)
    carry[...] = carry[...] + jnp.sum(onehot, axis=0, keepdims=True)

    meta = jnp.where(lane == 0, cls.astype(F32), 0.0) + jnp.where(lane == 1, rank, 0.0)
    meta_ref[...] = jnp.transpose(meta)[0:SUBLANES, :].astype(jnp.int32)
    cnt_ref[...] = jnp.broadcast_to(carry[...], cnt_ref.shape).astype(jnp.int32)


def _router(h, t, whi, wlo, bias):
    return pl.pallas_call(
        _router_kernel,
        out_shape=(jax.ShapeDtypeStruct((SUBLANES, t), jnp.int32),
                   jax.ShapeDtypeStruct((SUBLANES, LANES), jnp.int32)),
        grid=(t // TB_ROUTE,),
        in_specs=[pl.BlockSpec((TB_ROUTE, D_MODEL), lambda i: (i, 0)),
                  pl.BlockSpec((D_MODEL, LANES), lambda i: (0, 0)),
                  pl.BlockSpec((D_MODEL, LANES), lambda i: (0, 0)),
                  pl.BlockSpec((1, LANES), lambda i: (0, 0))],
        out_specs=(pl.BlockSpec((SUBLANES, TB_ROUTE), lambda i: (0, i)),
                   pl.BlockSpec((SUBLANES, LANES), lambda i: (0, 0))),
        scratch_shapes=[pltpu.VMEM((1, LANES), F32)],
        compiler_params=pltpu.CompilerParams(dimension_semantics=("arbitrary",)),
        name="router",
    )(h, whi, wlo, bias)


def _invperm_kernel(offs_ref, cls_ref, rank_ref, src_ref, dst_ref, *, n_tokens, n_rows):
    step = pl.program_id(0)

    @pl.when(step == 0)
    def _():
        def init(p, _):
            src_ref[p] = 0
            dst_ref[p] = n_tokens + (p % (2 * TM))
            return 0

        lax.fori_loop(0, n_rows, init, 0, unroll=8)

    def place(r, _):
        p = offs_ref[cls_ref[r]] + rank_ref[r]
        tok = step * TB_PERM + r
        src_ref[p] = tok
        dst_ref[p] = tok
        return 0

    lax.fori_loop(0, TB_PERM, place, 0, unroll=8)


def _invperm(offs, cls, rank, n_rows):
    t = cls.shape[0]
    smem_blk = pl.BlockSpec((TB_PERM,), lambda i, offs: (i,), memory_space=pltpu.SMEM)
    smem_all = pl.BlockSpec((n_rows,), lambda i, offs: (0,), memory_space=pltpu.SMEM)
    return pl.pallas_call(
        functools.partial(_invperm_kernel, n_tokens=t, n_rows=n_rows),
        out_shape=(jax.ShapeDtypeStruct((n_rows,), jnp.int32),
                   jax.ShapeDtypeStruct((n_rows,), jnp.int32)),
        grid_spec=pltpu.PrefetchScalarGridSpec(
            num_scalar_prefetch=1,
            grid=(t // TB_PERM,),
            in_specs=[smem_blk, smem_blk],
            out_specs=(smem_all, smem_all)),
        compiler_params=pltpu.CompilerParams(dimension_semantics=("arbitrary",)),
        name="moe_invperm",
    )(offs, cls, rank)


def _expert_kernel(ea_ref, eb_ref, nused_ref, src_cur_ref, src_nxt_ref, dst_ref,
                   h_hbm, wr_ref, br_ref, wgua_ref, wda_ref, wgub_ref, wdb_ref, lng_ref, lnb_ref,
                   out_hbm, xbuf, ybuf, gsem, ssem, *, n_tokens):
    j = pl.program_id(0)
    n_used = nused_ref[0]
    slot = j % 2
    other = 1 - slot

    def gather(idx_ref, r, s):
        return pltpu.make_async_copy(h_hbm.at[pl.ds(idx_ref[r], 1), :],
                                     xbuf.at[s, pl.ds(r, 1), :], gsem.at[s])

    def gather_done(r, s):
        return pltpu.make_async_copy(h_hbm.at[pl.ds(0, 1), :],
                                     xbuf.at[s, pl.ds(r, 1), :], gsem.at[s])

    def scatter(r, s):
        return pltpu.make_async_copy(ybuf.at[s, pl.ds(r, 1), :],
                                     out_hbm.at[pl.ds(dst_ref[r], 1), :], ssem.at[s])

    def scatter_done(r, s):
        return pltpu.make_async_copy(ybuf.at[s, pl.ds(r, 1), :],
                                     out_hbm.at[pl.ds(0, 1), :], ssem.at[s])

    def each_row(fn):
        def body(r, _):
            fn(r)
            return 0
        lax.fori_loop(0, TM, body, 0)

    @pl.when(j < n_used)
    def _():
        @pl.when(j == 0)
        def _():
            ybuf[...] = jnp.zeros_like(ybuf)
            each_row(lambda r: gather(src_cur_ref, r, 0).start())

        for r in range(TM):
            gather_done(r, slot).wait()
        for r in range(TM):
            gather(src_nxt_ref, r, other).start()

        x = xbuf[slot]
        xb = x.astype(BF16)

        logits = _dot(xb, wr_ref[...]) + br_ref[...]
        lane = lax.broadcasted_iota(jnp.int32, logits.shape, 1)
        ea, eb = ea_ref[j], eb_ref[j]
        group = ea // EXPERTS_PER_GROUP
        gl = jnp.where(lane < N_GROUPS, logits, F32(-jnp.inf))
        gmax = jnp.max(gl, axis=-1, keepdims=True)
        gsum = jnp.sum(jnp.where(lane < N_GROUPS, jnp.exp(gl - gmax), 0.0), axis=-1, keepdims=True)
        lg = jnp.sum(jnp.where(lane == group, logits, 0.0), axis=-1, keepdims=True)
        g_top_p = jnp.exp(lg - gmax) / gsum
        la = jnp.sum(jnp.where(lane == ea + N_GROUPS, logits, 0.0), axis=-1, keepdims=True)
        lb = jnp.sum(jnp.where(lane == eb + N_GROUPS, logits, 0.0), axis=-1, keepdims=True)
        w_a = g_top_p / (1.0 + jnp.exp(lb - la))
        w_b = g_top_p / (1.0 + jnp.exp(la - lb))

        def expert(wgu_ref, wd_ref):
            gu = _dot(xb, wgu_ref[0])
            gate, up = gu[:, :D_EXPERT], gu[:, D_EXPERT:]
            hid = (gate * _sigmoid(gate)) * up
            return _dot(hid.astype(BF16), wd_ref[0])

        ffn = w_a * expert(wgua_ref, wda_ref) + w_b * expert(wgub_ref, wdb_ref)
        res = _layer_norm_rows(ALPHA * x + ffn, lng_ref[...], lnb_ref[...])

        @pl.when(j >= 2)
        def _():
            for r in range(TM):
                scatter_done(r, slot).wait()

        ybuf[slot] = res
        each_row(lambda r: scatter(r, slot).start())

        @pl.when(j == n_used - 1)
        def _():
            for r in range(TM):
                gather_done(r, other).wait()

            @pl.when(j >= 1)
            def _():
                for r in range(TM):
                    scatter_done(r, other).wait()

            for r in range(TM):
                scatter_done(r, slot).wait()

            for s in range(2):
                fill = pltpu.make_async_copy(
                    ybuf.at[s], out_hbm.at[pl.ds(n_tokens + s * TM, TM), :], ssem.at[s])
                fill.start()
                fill.wait()


def _experts(h, t, src, dst, tile_ea, tile_eb, n_used, w_router, b_router, wgu, wd, ln_g, ln_b):
    n_tiles = src.shape[0] // TM - 1
    smem_tile = lambda off: pl.BlockSpec((TM,), lambda j, ea, eb, nu, _o=off: (j + _o,),
                                         memory_space=pltpu.SMEM)
    const = lambda shape: pl.BlockSpec(shape, lambda j, ea, eb, nu: (0,) * len(shape))
    wgu_spec = lambda which: pl.BlockSpec(
        (1, D_MODEL, 2 * D_EXPERT), lambda j, ea, eb, nu, _w=which: ((ea, eb)[_w][j], 0, 0))
    wd_spec = lambda which: pl.BlockSpec(
        (1, D_EXPERT, D_MODEL), lambda j, ea, eb, nu, _w=which: ((ea, eb)[_w][j], 0, 0))
    grid_spec = pltpu.PrefetchScalarGridSpec(
        num_scalar_prefetch=3,
        grid=(n_tiles,),
        in_specs=[smem_tile(0), smem_tile(1), smem_tile(0),
                  pl.BlockSpec(memory_space=pl.ANY),
                  const((D_MODEL, LANES)), const((1, LANES)),
                  wgu_spec(0), wd_spec(0), wgu_spec(1), wd_spec(1),
                  const((1, D_MODEL)), const((1, D_MODEL))],
        out_specs=pl.BlockSpec(memory_space=pl.ANY),
        scratch_shapes=[pltpu.VMEM((2, TM, D_MODEL), F32),
                        pltpu.VMEM((2, TM, D_MODEL), F32),
                        pltpu.SemaphoreType.DMA((2,)),
                        pltpu.SemaphoreType.DMA((2,))],
    )
    return pl.pallas_call(
        functools.partial(_expert_kernel, n_tokens=t),
        out_shape=jax.ShapeDtypeStruct((t + 2 * TM, D_MODEL), F32),
        grid_spec=grid_spec,
        compiler_params=pltpu.CompilerParams(dimension_semantics=("arbitrary",)),
        name="moe_experts",
    )(tile_ea, tile_eb, n_used, src, src, dst, h, w_router, b_router, wgu, wd, wgu, wd, ln_g, ln_b)


def _class_expert_tables():
    first, second = [], []
    for g in range(N_GROUPS):
        for lo in range(EXPERTS_PER_GROUP):
            for hi in range(lo + 1, EXPERTS_PER_GROUP):
                first.append(g * EXPERTS_PER_GROUP + lo)
                second.append(g * EXPERTS_PER_GROUP + hi)
    return jnp.array(first, jnp.int32), jnp.array(second, jnp.int32)


def _moe_layer(h, t, wg, bg, we, be, w_gate, w_up, w_down, ln_g, ln_b):
    n_tiles = t // TM + N_CLASSES
    n_rows = (n_tiles + 1) * TM

    w_router = jnp.zeros((D_MODEL, LANES), F32)
    w_router = w_router.at[:, :N_GROUPS].set(wg).at[:, N_GROUPS:N_GROUPS + N_EXPERTS].set(we)
    b_router = jnp.zeros((1, LANES), F32)
    b_router = b_router.at[0, :N_GROUPS].set(bg).at[0, N_GROUPS:N_GROUPS + N_EXPERTS].set(be)
    w_hi = w_router.astype(BF16)
    w_lo = (w_router - w_hi.astype(F32)).astype(BF16)

    meta, counts = _router(h, t, w_hi, w_lo, b_router)

    cnt = counts[0]
    tiles_per = (cnt + TM - 1) // TM
    tile_end = jnp.cumsum(tiles_per)
    offs = (tile_end - tiles_per) * TM
    n_used = tile_end[-1]
    tile_id = jnp.arange(n_tiles, dtype=jnp.int32)
    tile_cls = jnp.sum((jnp.minimum(tile_id, n_used - 1)[:, None] >= tile_end[None, :N_CLASSES])
                       .astype(jnp.int32), axis=1)
    first, second = _class_expert_tables()
    src, dst = _invperm(offs, meta[0], meta[1], n_rows)
    wgu = jnp.concatenate([w_gate, w_up], axis=-1).astype(BF16)
    return _experts(h, t, src, dst, first[tile_cls], second[tile_cls], n_used.reshape(1),
                    w_hi, b_router, wgu, w_down.astype(BF16), ln_g, ln_b)


def kernel(x, positions, w_in, sc_conv_w, sc_conv_b, lru_conv_w, lru_conv_b, lru_w_r, lru_b_r,
           lru_w_i, lru_b_i, lru_lambda, sg_norm_g, sg_w_s, sg_b_s, ret_norm_g, branch_proj, w_out,
           ln_mix_g, ln_mix_b, router_group_w, router_group_b, router_expert_w, router_expert_b,
           exp_w_gate, exp_w_up, exp_w_down, ln_ffn_g, ln_ffn_b):
    batch, seq, d = x.shape
    assert d == D_MODEL and seq % TS == 0 and w_in.shape[-1] == N_IN
    depth = w_in.shape[0]
    t = batch * seq
    cos_t, sin_t = _rope_tables(positions)
    dec_all, qd_tab, kd_tab, cd_tab = _retention_tables()
    avg = _block_diag(jnp.full((N_HEADS, HEAD_DIM, HEAD_DIM), 1.0 / HEAD_DIM, F32)).astype(BF16)

    h = x.reshape(t, d)
    for l in range(depth):
        w_ri = jnp.concatenate([_block_diag(lru_w_r[l]), _block_diag(lru_w_i[l])], axis=1).astype(BF16)
        b_ri = jnp.concatenate([lru_b_r[l], lru_b_i[l]])[None, :]
        bs_tab = jnp.repeat(sg_b_s[l].T, HEAD_DIM, axis=1)
        consts = (
            w_in[l].astype(BF16), sc_conv_w[l], sc_conv_b[l][None, :], lru_conv_w[l], lru_conv_b[l][None, :],
            w_ri, b_ri, lru_lambda[l][None, :], sg_norm_g[l][None, :], sg_w_s[l], bs_tab,
            ret_norm_g[l][None, :], dec_all, qd_tab, kd_tab, cd_tab, avg,
            branch_proj[l].astype(BF16), w_out[l].astype(BF16), ln_mix_g[l][None, :], ln_mix_b[l][None, :],
        )
        h = _mixer_layer(h, cos_t, sin_t, consts, batch, seq)
        h = _moe_layer(h, t, router_group_w[l], router_group_b[l], router_expert_w[l], router_expert_b[l],
                       exp_w_gate[l], exp_w_up[l], exp_w_down[l], ln_ffn_g[l][None, :], ln_ffn_b[l][None, :])
    return h[:t].reshape(batch, seq, d)
```

```python
import functools

import jax
import jax.numpy as jnp
from jax import lax
from jax.experimental import pallas as pl
from jax.experimental.pallas import tpu as pltpu

F32 = jnp.float32
BF16 = jnp.bfloat16

LANES = 128
SUBLANES = 8
VMEM_LIMIT_BYTES = 56 * 1024 * 1024

D_MODEL = 1024
ROW_TILES = D_MODEL // LANES
assert ROW_TILES == SUBLANES
WIDTH = D_MODEL // 4
N_HEADS = 4
HEAD_DIM = WIDTH // N_HEADS
CHUNK = 128
SC_K = 3
LRU_K = 4
LRU_C = 8.0
ROPE_BASE = 10000.0
N_GROUPS = 4
EXPERTS_PER_GROUP = 8
N_EXPERTS = N_GROUPS * EXPERTS_PER_GROUP
D_EXPERT = D_MODEL // 4
LN_EPS = 1e-5
DEPTH = 2
ALPHA = (2.0 * DEPTH) ** 0.25

COL_SC_B, COL_SC_C, COL_SC_X, COL_LRU, COL_SG_U, COL_SG_V, COL_Q, COL_K, COL_V, COL_G = (
    i * WIDTH for i in range(10))
COL_GATES = 10 * WIDTH
N_IN = COL_GATES + 4 * D_MODEL

TS = 256
SCAN_PAD = TS // 2
CONV_PAD = SUBLANES
TB_ROUTE = 512
TB_PERM = 2048
TM = 128
PAIRS_PER_GROUP = EXPERTS_PER_GROUP * (EXPERTS_PER_GROUP - 1) // 2
N_CLASSES = N_GROUPS * PAIRS_PER_GROUP
assert N_CLASSES <= LANES


def _sigmoid(x):
    return 0.5 * jnp.tanh(0.5 * x) + 0.5


def _dot(a, b):
    return jnp.dot(a, b, preferred_element_type=F32)


def _load_token_major(ref, rows, lead=()):
    return jnp.concatenate(
        [ref[lead + (pl.ds(c, rows, stride=ROW_TILES), slice(None))] for c in range(ROW_TILES)], axis=1)


def _store_token_major(ref, y, rows, lead=()):
    for c in range(ROW_TILES):
        ref[lead + (pl.ds(c, rows, stride=ROW_TILES), slice(None))] = y[:, c * LANES:(c + 1) * LANES]


def _layer_norm_rows(y, g, b):
    mu = jnp.mean(y, axis=-1, keepdims=True)
    yc = y - mu
    var = jnp.mean(yc * yc, axis=-1, keepdims=True)
    return yc * lax.rsqrt(var + LN_EPS) * g + b


def _rope_table_kernel(pos_ref, freq_ref, cos_ref, sin_ref):
    ang = pos_ref[...].astype(F32) * freq_ref[...]
    lane = lax.broadcasted_iota(jnp.int32, ang.shape, 1)
    first_half = (lane % HEAD_DIM) < (HEAD_DIM // 2)
    cos_ref[...] = jnp.cos(ang)
    s = jnp.sin(ang)
    sin_ref[...] = jnp.where(first_half, -s, s)


def _rope_tables(positions):
    t = positions.size
    half = HEAD_DIM // 2
    inv_freq = ROPE_BASE ** (-jnp.arange(half, dtype=F32) / half)
    freq_row = jnp.tile(inv_freq, LANES // half)[None, :]
    rows = 1024
    return pl.pallas_call(
        _rope_table_kernel,
        out_shape=(jax.ShapeDtypeStruct((t, LANES), F32), jax.ShapeDtypeStruct((t, LANES), F32)),
        grid=(t // rows,),
        in_specs=[pl.BlockSpec((rows, 1), lambda i: (i, 0)),
                  pl.BlockSpec((1, LANES), lambda i: (0, 0))],
        out_specs=(pl.BlockSpec((rows, LANES), lambda i: (i, 0)),
                   pl.BlockSpec((rows, LANES), lambda i: (i, 0))),
        name="rope_tables",
    )(positions.reshape(t, 1), freq_row)


def _mixer_kernel(h_ref, cos_ref, sin_ref, w_in_ref, scw_ref, scb_ref, lcw_ref, lcb_ref,
                  wri_ref, bri_ref, lam_ref, sgg_ref, ws_ref, bs_ref, rg_ref,
                  dec_ref, qd_ref, kd_ref, cd_ref, avg_ref, bp_ref, wo_ref, lng_ref, lnb_ref,
                  out_ref, ubuf, xbuf, hcar, sbd, abuf, hbuf, *, token_major_in):
    step = pl.program_id(1)

    @pl.when(step == 0)
    def _():
        ubuf[0:CONV_PAD, :] = jnp.zeros((CONV_PAD, WIDTH), F32)
        xbuf[0:CONV_PAD, :] = jnp.zeros((CONV_PAD, WIDTH), F32)
        hcar[...] = jnp.zeros_like(hcar)
        sbd[...] = jnp.zeros_like(sbd)
        abuf[0:SCAN_PAD, :] = jnp.ones((SCAN_PAD, WIDTH), F32)
        hbuf[0:SCAN_PAD, :] = jnp.zeros((SCAN_PAD, WIDTH), F32)

    h = _load_token_major(h_ref, TS) if token_major_in else h_ref[...]
    hb = h.astype(BF16)

    def proj(col, width=WIDTH):
        return _dot(hb, w_in_ref[:, col:col + width])

    lane = lax.broadcasted_iota(jnp.int32, (CHUNK, WIDTH), 1)
    head_of_lane = lane // HEAD_DIM
    avg = avg_ref[...]

    def group_standardize(x):
        mean = _dot(x.astype(BF16), avg)
        xc = x - mean
        var = _dot((xc * xc).astype(BF16), avg)
        return xc * lax.rsqrt(var + LN_EPS)

    sc_b, sc_c, sc_x = proj(COL_SC_B), proj(COL_SC_C), proj(COL_SC_X)
    u = sc_c * sc_x
    ubuf[CONV_PAD:CONV_PAD + TS, :] = u
    conv = scw_ref[2:3, :] * u + scb_ref[...]
    for j in range(SC_K - 1):
        back = SC_K - 1 - j
        conv = conv + scw_ref[j:j + 1, :] * ubuf[CONV_PAD - back:CONV_PAD - back + TS, :]
    ubuf[0:CONV_PAD, :] = ubuf[TS:TS + CONV_PAD, :]
    branch_a = sc_b * conv

    lx = proj(COL_LRU)
    xbuf[CONV_PAD:CONV_PAD + TS, :] = lx
    xc = lcw_ref[LRU_K - 1:LRU_K, :] * lx + lcb_ref[...]
    for j in range(LRU_K - 1):
        back = LRU_K - 1 - j
        xc = xc + lcw_ref[j:j + 1, :] * xbuf[CONV_PAD - back:CONV_PAD - back + TS, :]
    xbuf[0:CONV_PAD, :] = xbuf[TS:TS + CONV_PAD, :]
    ri = _dot(xc.astype(BF16), wri_ref[...]) + bri_ref[...]
    r = _sigmoid(ri[:, :WIDTH])
    ig = _sigmoid(ri[:, WIDTH:])
    neg_lam = -lam_ref[...]
    softplus = jnp.maximum(neg_lam, 0.0) + jnp.log1p(jnp.exp(-jnp.abs(neg_lam)))
    log_a = (-LRU_C) * r * softplus
    a = jnp.exp(log_a)
    th = jnp.tanh(log_a)
    uu = jnp.sqrt((-2.0) * th / (1.0 - th)) * (ig * xc)
    row = lax.broadcasted_iota(jnp.int32, (TS, WIDTH), 0)
    uu = uu + jnp.where(row == 0, a * hcar[...], 0.0)
    abuf[SCAN_PAD:SCAN_PAD + TS, :] = a
    hbuf[SCAN_PAD:SCAN_PAD + TS, :] = uu
    shift = 1
    while shift < TS:
        a_cur = abuf[SCAN_PAD:SCAN_PAD + TS, :]
        h_prev = hbuf[SCAN_PAD - shift:SCAN_PAD - shift + TS, :]
        h_new = a_cur * h_prev + hbuf[SCAN_PAD:SCAN_PAD + TS, :]
        if shift * 2 < TS:
            a_new = a_cur * abuf[SCAN_PAD - shift:SCAN_PAD - shift + TS, :]
            abuf[SCAN_PAD:SCAN_PAD + TS, :] = a_new
        hbuf[SCAN_PAD:SCAN_PAD + TS, :] = h_new
        shift *= 2
    branch_b = hbuf[SCAN_PAD:SCAN_PAD + TS, :]
    hcar[...] = hbuf[SCAN_PAD + TS - 1:SCAN_PAD + TS, :]

    gu = jax.nn.gelu(proj(COL_SG_U))
    gv = jax.nn.gelu(proj(COL_SG_V))
    vn = (group_standardize(gv) * sgg_ref[...]).astype(BF16)
    trow = lax.broadcasted_iota(jnp.int32, (CHUNK, CHUNK), 0)
    tcol = lax.broadcasted_iota(jnp.int32, (CHUNK, CHUNK), 1)
    w_causal = [jnp.where(trow >= tcol, ws_ref[g], 0.0).astype(BF16) for g in range(N_HEADS)]
    sv_chunks = []
    for c in range(TS // CHUNK):
        vch = vn[c * CHUNK:(c + 1) * CHUNK, :]
        sv = bs_ref[...]
        for g in range(N_HEADS):
            sv = sv + jnp.where(head_of_lane == g, _dot(w_causal[g], vch), 0.0)
        sv_chunks.append(sv)
    branch_c = gu * jnp.concatenate(sv_chunks, axis=0)

    cos = jnp.concatenate([cos_ref[...]] * (WIDTH // LANES), axis=1)
    sin = jnp.concatenate([sin_ref[...]] * (WIDTH // LANES), axis=1)
    lane_ts = lax.broadcasted_iota(jnp.int32, (TS, WIDTH), 1)
    first_half = (lane_ts % HEAD_DIM) < (HEAD_DIM // 2)

    def rope(t):
        swapped = jnp.where(first_half,
                            pltpu.roll(t, WIDTH - HEAD_DIM // 2, 1),
                            pltpu.roll(t, HEAD_DIM // 2, 1))
        return t * cos + swapped * sin

    q = rope(proj(COL_Q))
    k = rope(proj(COL_K)) * (HEAD_DIM ** -0.5)
    v = proj(COL_V)
    zg = proj(COL_G)
    bd_mask = (lax.broadcasted_iota(jnp.int32, (WIDTH, WIDTH), 0) // HEAD_DIM
               == lax.broadcasted_iota(jnp.int32, (WIDTH, WIDTH), 1) // HEAD_DIM)
    o_chunks = []
    for c in range(TS // CHUNK):
        sl = slice(c * CHUNK, (c + 1) * CHUNK)
        qc, kc, vc = q[sl, :], k[sl, :], v[sl, :]
        qcb = qc.astype(BF16)
        kstack = jnp.concatenate(
            [jnp.where(head_of_lane == hh, kc, 0.0) for hh in range(N_HEADS)], axis=0).astype(BF16)
        vstack = jnp.concatenate(
            [jnp.where(head_of_lane == hh, vc, 0.0) for hh in range(N_HEADS)], axis=0).astype(BF16)
        scores = lax.dot_general(qcb, kstack, (((1,), (1,)), ((), ())),
                                 preferred_element_type=F32)
        inner = _dot((scores * dec_ref[...]).astype(BF16), vstack)
        state = sbd[...]
        cross = _dot(qcb, state.astype(BF16)) * qd_ref[...]
        o_chunks.append(inner + cross)
        kdec = (kc * kd_ref[...]).astype(BF16)
        kv = lax.dot_general(kdec, vc.astype(BF16), (((0,), (0,)), ((), ())),
                             preferred_element_type=F32)
        sbd[...] = cd_ref[...] * state + jnp.where(bd_mask, kv, 0.0)
    o = jnp.concatenate(o_chunks, axis=0)
    branch_d = (zg * _sigmoid(zg)) * (group_standardize(o) * rg_ref[...])

    merged = jnp.zeros((TS, D_MODEL), F32)
    for b_idx, branch in enumerate((branch_a, branch_b, branch_c, branch_d)):
        gate = _sigmoid(proj(COL_GATES + b_idx * D_MODEL, D_MODEL))
        merged = merged + gate * _dot(branch.astype(BF16), bp_ref[b_idx])
    mix = _dot(merged.astype(BF16), wo_ref[...])
    _store_token_major(out_ref, _layer_norm_rows(ALPHA * h + mix, lng_ref[...], lnb_ref[...]), TS)


def _const_spec(shape):
    nd = len(shape)
    return pl.BlockSpec(shape, lambda b, i, _nd=nd: (0,) * _nd, pipeline_mode=pl.Buffered(1))


def _mixer_layer(h, cos_t, sin_t, consts, batch, seq, token_major_in):
    t = batch * seq
    steps = seq // TS
    row_map = lambda b, i: (b * steps + i, 0)
    tm_spec = pl.BlockSpec((TS * ROW_TILES, LANES), row_map)
    in_specs = [tm_spec if token_major_in else pl.BlockSpec((TS, D_MODEL), row_map),
                pl.BlockSpec((TS, LANES), row_map),
                pl.BlockSpec((TS, LANES), row_map)]
    in_specs += [_const_spec(c.shape) for c in consts]
    return pl.pallas_call(
        functools.partial(_mixer_kernel, token_major_in=token_major_in),
        out_shape=jax.ShapeDtypeStruct((t * ROW_TILES, LANES), F32),
        grid=(batch, steps),
        in_specs=in_specs,
        out_specs=tm_spec,
        scratch_shapes=[
            pltpu.VMEM((CONV_PAD + TS, WIDTH), F32),
            pltpu.VMEM((CONV_PAD + TS, WIDTH), F32),
            pltpu.VMEM((1, WIDTH), F32),
            pltpu.VMEM((WIDTH, WIDTH), F32),
            pltpu.VMEM((SCAN_PAD + TS, WIDTH), F32),
            pltpu.VMEM((SCAN_PAD + TS, WIDTH), F32),
        ],
        compiler_params=pltpu.CompilerParams(
            dimension_semantics=("arbitrary", "arbitrary"),
            vmem_limit_bytes=VMEM_LIMIT_BYTES),
        name="mixer_layer",
    )(h, cos_t, sin_t, *consts)


def _block_diag(w):
    heads, d, e = w.shape
    eye = jnp.eye(heads, dtype=w.dtype)
    return (eye[:, None, :, None] * w[:, :, None, :]).reshape(heads * d, heads * e)


def _retention_tables():
    log_gamma = jnp.log1p(-jnp.exp2(-5.0 - jnp.arange(N_HEADS, dtype=F32)))
    pos = jnp.arange(CHUNK, dtype=F32)
    diff = pos[:, None] - pos[None, :]
    decay = jnp.where(diff >= 0, jnp.exp(jnp.maximum(diff, 0.0) * log_gamma[:, None, None]), 0.0)
    dec_all = jnp.transpose(decay, (1, 0, 2)).reshape(CHUNK, N_HEADS * CHUNK)
    q_decay = jnp.exp((pos + 1.0)[:, None] * log_gamma)
    k_decay = jnp.exp((CHUNK - 1.0 - pos)[:, None] * log_gamma)
    qd_tab = jnp.repeat(q_decay, HEAD_DIM, axis=1)
    kd_tab = jnp.repeat(k_decay, HEAD_DIM, axis=1)
    chunk_decay = jnp.repeat(jnp.exp(CHUNK * log_gamma), HEAD_DIM)
    cd_tab = jnp.broadcast_to(chunk_decay[:, None], (WIDTH, WIDTH))
    return dec_all, qd_tab, kd_tab, cd_tab


def _router_kernel(h_ref, whi_ref, wlo_ref, b_ref, meta_ref, cnt_ref, carry):
    step = pl.program_id(0)

    @pl.when(step == 0)
    def _():
        carry[...] = jnp.zeros_like(carry)

    h = _load_token_major(h_ref, TB_ROUTE)
    h_hi = h.astype(BF16)
    h_lo = (h - h_hi.astype(F32)).astype(BF16)
    whi = whi_ref[...]
    logits = _dot(h_hi, whi) + _dot(h_lo, whi) + _dot(h_hi, wlo_ref[...]) + b_ref[...]
    lane = lax.broadcasted_iota(jnp.int32, logits.shape, 1)
    big = jnp.int32(2 ** 30)
    neg_inf = F32(-jnp.inf)

    gl = jnp.where(lane < N_GROUPS, logits, neg_inf)
    gmax = jnp.max(gl, axis=-1, keepdims=True)
    g_idx = jnp.min(jnp.where(gl == gmax, lane, big), axis=-1, keepdims=True)

    e_lane = lane - N_GROUPS
    in_group = (e_lane >= g_idx * EXPERTS_PER_GROUP) & (e_lane < (g_idx + 1) * EXPERTS_PER_GROUP)
    el = jnp.where(in_group, logits, neg_inf)
    m1 = jnp.max(el, axis=-1, keepdims=True)
    i1 = jnp.min(jnp.where(el == m1, lane, big), axis=-1, keepdims=True)
    el2 = jnp.where(lane == i1, neg_inf, el)
    m2 = jnp.max(el2, axis=-1, keepdims=True)
    i2 = jnp.min(jnp.where(el2 == m2, lane, big), axis=-1, keepdims=True)
    base = (g_idx * EXPERTS_PER_GROUP + N_GROUPS).astype(F32)
    lo = jnp.minimum(i1, i2).astype(F32) - base
    hi = jnp.maximum(i1, i2).astype(F32) - base
    pair = lo * (2.0 * EXPERTS_PER_GROUP - 1.0 - lo) * 0.5 + (hi - lo - 1.0)
    cls = (g_idx.astype(F32) * PAIRS_PER_GROUP + pair).astype(jnp.int32)

    onehot = jnp.where(lane == cls, 1.0, 0.0)
    tb = h.shape[0]
    r_i = lax.broadcasted_iota(jnp.int32, (tb, tb), 0)
    c_i = lax.broadcasted_iota(jnp.int32, (tb, tb), 1)
    strict_lower = jnp.where(r_i > c_i, 1.0, 0.0).astype(BF16)
    before = _dot(strict_lower, onehot.astype(BF16)) + carry[...]
    rank = jnp.sum(jnp.where(lane == cls, before, 0.0), axis=-1, keepdims=True)
    carry[...] = carry[...] + jnp.sum(onehot, axis=0, keepdims=True)

    meta = jnp.where(lane == 0, cls.astype(F32), 0.0) + jnp.where(lane == 1, rank, 0.0)
    meta_ref[...] = jnp.transpose(meta)[0:SUBLANES, :].astype(jnp.int32)
    cnt_ref[...] = jnp.broadcast_to(carry[...], cnt_ref.shape).astype(jnp.int32)


def _router(h, t, whi, wlo, bias):
    return pl.pallas_call(
        _router_kernel,
        out_shape=(jax.ShapeDtypeStruct((SUBLANES, t), jnp.int32),
                   jax.ShapeDtypeStruct((SUBLANES, LANES), jnp.int32)),
        grid=(t // TB_ROUTE,),
        in_specs=[pl.BlockSpec((TB_ROUTE * ROW_TILES, LANES), lambda i: (i, 0)),
                  pl.BlockSpec((D_MODEL, LANES), lambda i: (0, 0)),
                  pl.BlockSpec((D_MODEL, LANES), lambda i: (0, 0)),
                  pl.BlockSpec((1, LANES), lambda i: (0, 0))],
        out_specs=(pl.BlockSpec((SUBLANES, TB_ROUTE), lambda i: (0, i)),
                   pl.BlockSpec((SUBLANES, LANES), lambda i: (0, 0))),
        scratch_shapes=[pltpu.VMEM((1, LANES), F32)],
        compiler_params=pltpu.CompilerParams(dimension_semantics=("arbitrary",)),
        name="router",
    )(h, whi, wlo, bias)


def _invperm_kernel(offs_ref, cls_ref, rank_ref, src0_hbm, dst0_hbm, src_hbm, dst_hbm,
                    src_ref, dst_ref, sems):
    step = pl.program_id(0)

    def move(pairs):
        copies = [pltpu.make_async_copy(a, b, sems.at[i]) for i, (a, b) in enumerate(pairs)]
        for c in copies:
            c.start()
        for c in copies:
            c.wait()

    @pl.when(step == 0)
    def _():
        move([(src0_hbm, src_ref), (dst0_hbm, dst_ref)])

    def place(r, _):
        p = offs_ref[cls_ref[r]] + rank_ref[r]
        tok = step * TB_PERM + r
        src_ref[p] = tok
        dst_ref[p] = tok
        return 0

    lax.fori_loop(0, TB_PERM, place, 0, unroll=8)

    @pl.when(step == pl.num_programs(0) - 1)
    def _():
        move([(src_ref, src_hbm), (dst_ref, dst_hbm)])


def _invperm(offs, cls, rank, n_rows):
    t = cls.shape[0]
    row = jnp.arange(n_rows, dtype=jnp.int32)
    src0 = jnp.zeros((n_rows,), jnp.int32)
    dst0 = t + row % (2 * TM)
    smem_blk = pl.BlockSpec((TB_PERM,), lambda i, offs: (i,), memory_space=pltpu.SMEM)
    hbm = pl.BlockSpec(memory_space=pl.ANY)
    return pl.pallas_call(
        _invperm_kernel,
        out_shape=(jax.ShapeDtypeStruct((n_rows,), jnp.int32),
                   jax.ShapeDtypeStruct((n_rows,), jnp.int32)),
        grid_spec=pltpu.PrefetchScalarGridSpec(
            num_scalar_prefetch=1,
            grid=(t // TB_PERM,),
            in_specs=[smem_blk, smem_blk, hbm, hbm],
            out_specs=(hbm, hbm),
            scratch_shapes=[pltpu.SMEM((n_rows,), jnp.int32),
                            pltpu.SMEM((n_rows,), jnp.int32),
                            pltpu.SemaphoreType.DMA((2,))]),
        compiler_params=pltpu.CompilerParams(dimension_semantics=("arbitrary",)),
        name="moe_invperm",
    )(offs, cls, rank, src0, dst0)


def _expert_kernel(ea_ref, eb_ref, nused_ref, src_cur_ref, src_nxt_ref, dst_ref,
                   h_hbm, wr_ref, br_ref, wgua_ref, wda_ref, wgub_ref, wdb_ref, lng_ref, lnb_ref,
                   out_hbm, xbuf, ybuf, gsem, ssem, *, n_tokens):
    j = pl.program_id(0)
    n_used = nused_ref[0]
    slot = j % 2
    other = 1 - slot

    def token_rows(tok):
        if isinstance(tok, int):
            return pl.ds(tok * ROW_TILES, ROW_TILES)
        return pl.ds(pl.multiple_of(tok * ROW_TILES, ROW_TILES), ROW_TILES)

    def gather(idx_ref, r, s):
        return pltpu.make_async_copy(h_hbm.at[token_rows(idx_ref[r]), :],
                                     xbuf.at[s, token_rows(r), :], gsem.at[s])

    def gather_done(r, s):
        return pltpu.make_async_copy(h_hbm.at[token_rows(0), :],
                                     xbuf.at[s, token_rows(r), :], gsem.at[s])

    def scatter(r, s):
        return pltpu.make_async_copy(ybuf.at[s, token_rows(r), :],
                                     out_hbm.at[token_rows(dst_ref[r]), :], ssem.at[s])

    def scatter_done(r, s):
        return pltpu.make_async_copy(ybuf.at[s, token_rows(r), :],
                                     out_hbm.at[token_rows(0), :], ssem.at[s])

    def each_row(fn):
        def body(r, _):
            fn(r)
            return 0
        lax.fori_loop(0, TM, body, 0)

    @pl.when(j < n_used)
    def _():
        @pl.when(j == 0)
        def _():
            ybuf[...] = jnp.zeros_like(ybuf)
            each_row(lambda r: gather(src_cur_ref, r, 0).start())

        for r in range(TM):
            gather_done(r, slot).wait()
        for r in range(TM):
            gather(src_nxt_ref, r, other).start(priority=r % 2)

        x = _load_token_major(xbuf, TM, lead=(slot,))
        xb = x.astype(BF16)

        logits = _dot(xb, wr_ref[...]) + br_ref[...]
        lane = lax.broadcasted_iota(jnp.int32, logits.shape, 1)
        ea, eb = ea_ref[j], eb_ref[j]
        group = ea // EXPERTS_PER_GROUP
        gl = jnp.where(lane < N_GROUPS, logits, F32(-jnp.inf))
        gmax = jnp.max(gl, axis=-1, keepdims=True)
        gsum = jnp.sum(jnp.where(lane < N_GROUPS, jnp.exp(gl - gmax), 0.0), axis=-1, keepdims=True)
        lg = jnp.sum(jnp.where(lane == group, logits, 0.0), axis=-1, keepdims=True)
        g_top_p = jnp.exp(lg - gmax) / gsum
        la = jnp.sum(jnp.where(lane == ea + N_GROUPS, logits, 0.0), axis=-1, keepdims=True)
        lb = jnp.sum(jnp.where(lane == eb + N_GROUPS, logits, 0.0), axis=-1, keepdims=True)
        w_a = g_top_p / (1.0 + jnp.exp(lb - la))
        w_b = g_top_p / (1.0 + jnp.exp(la - lb))

        def expert(wgu_ref, wd_ref):
            gu = _dot(xb, wgu_ref[0])
            gate, up = gu[:, :D_EXPERT], gu[:, D_EXPERT:]
            hid = (gate * _sigmoid(gate)) * up
            return _dot(hid.astype(BF16), wd_ref[0])

        ffn = w_a * expert(wgua_ref, wda_ref) + w_b * expert(wgub_ref, wdb_ref)
        res = _layer_norm_rows(ALPHA * x + ffn, lng_ref[...], lnb_ref[...])

        @pl.when(j >= 2)
        def _():
            for r in range(TM):
                scatter_done(r, slot).wait()

        _store_token_major(ybuf, res, TM, lead=(slot,))
        for r in range(TM):
            scatter(r, slot).start(priority=r % 2)

        @pl.when(j == n_used - 1)
        def _():
            for r in range(TM):
                gather_done(r, other).wait()

            @pl.when(j >= 1)
            def _():
                for r in range(TM):
                    scatter_done(r, other).wait()

            for r in range(TM):
                scatter_done(r, slot).wait()

            for s in range(2):
                fill = pltpu.make_async_copy(
                    ybuf.at[s], out_hbm.at[pl.ds((n_tokens + s * TM) * ROW_TILES, TM * ROW_TILES), :],
                    ssem.at[s])
                fill.start()
                fill.wait()


def _experts(h, t, src, dst, tile_ea, tile_eb, n_used, w_router, b_router, wgu, wd, ln_g, ln_b):
    n_tiles = src.shape[0] // TM - 1
    smem_tile = lambda off: pl.BlockSpec((TM,), lambda j, ea, eb, nu, _o=off: (j + _o,),
                                         memory_space=pltpu.SMEM)
    const = lambda shape: pl.BlockSpec(shape, lambda j, ea, eb, nu: (0,) * len(shape))
    wgu_spec = lambda which: pl.BlockSpec(
        (1, D_MODEL, 2 * D_EXPERT), lambda j, ea, eb, nu, _w=which: ((ea, eb)[_w][j], 0, 0))
    wd_spec = lambda which: pl.BlockSpec(
        (1, D_EXPERT, D_MODEL), lambda j, ea, eb, nu, _w=which: ((ea, eb)[_w][j], 0, 0))
    grid_spec = pltpu.PrefetchScalarGridSpec(
        num_scalar_prefetch=3,
        grid=(n_tiles,),
        in_specs=[smem_tile(0), smem_tile(1), smem_tile(0),
                  pl.BlockSpec(memory_space=pl.ANY),
                  const((D_MODEL, LANES)), const((1, LANES)),
                  wgu_spec(0), wd_spec(0), wgu_spec(1), wd_spec(1),
                  const((1, D_MODEL)), const((1, D_MODEL))],
        out_specs=pl.BlockSpec(memory_space=pl.ANY),
        scratch_shapes=[pltpu.VMEM((2, TM * ROW_TILES, LANES), F32),
                        pltpu.VMEM((2, TM * ROW_TILES, LANES), F32),
                        pltpu.SemaphoreType.DMA((2,)),
                        pltpu.SemaphoreType.DMA((2,))],
    )
    return pl.pallas_call(
        functools.partial(_expert_kernel, n_tokens=t),
        out_shape=jax.ShapeDtypeStruct(((t + 2 * TM) * ROW_TILES, LANES), F32),
        grid_spec=grid_spec,
        compiler_params=pltpu.CompilerParams(dimension_semantics=("arbitrary",)),
        name="moe_experts",
    )(tile_ea, tile_eb, n_used, src, src, dst, h, w_router, b_router, wgu, wd, wgu, wd, ln_g, ln_b)


def _class_expert_tables():
    first, second = [], []
    for g in range(N_GROUPS):
        for lo in range(EXPERTS_PER_GROUP):
            for hi in range(lo + 1, EXPERTS_PER_GROUP):
                first.append(g * EXPERTS_PER_GROUP + lo)
                second.append(g * EXPERTS_PER_GROUP + hi)
    return jnp.array(first, jnp.int32), jnp.array(second, jnp.int32)


def _moe_layer(h, t, wg, bg, we, be, w_gate, w_up, w_down, ln_g, ln_b):
    n_tiles = t // TM + N_CLASSES
    n_rows = (n_tiles + 1) * TM

    w_router = jnp.zeros((D_MODEL, LANES), F32)
    w_router = w_router.at[:, :N_GROUPS].set(wg).at[:, N_GROUPS:N_GROUPS + N_EXPERTS].set(we)
    b_router = jnp.zeros((1, LANES), F32)
    b_router = b_router.at[0, :N_GROUPS].set(bg).at[0, N_GROUPS:N_GROUPS + N_EXPERTS].set(be)
    w_hi = w_router.astype(BF16)
    w_lo = (w_router - w_hi.astype(F32)).astype(BF16)

    meta, counts = _router(h, t, w_hi, w_lo, b_router)

    cnt = counts[0]
    tiles_per = (cnt + TM - 1) // TM
    tile_end = jnp.cumsum(tiles_per)
    offs = (tile_end - tiles_per) * TM
    n_used = tile_end[-1]
    tile_id = jnp.arange(n_tiles, dtype=jnp.int32)
    tile_cls = jnp.sum((jnp.minimum(tile_id, n_used - 1)[:, None] >= tile_end[None, :N_CLASSES])
                       .astype(jnp.int32), axis=1)
    first, second = _class_expert_tables()
    src, dst = _invperm(offs, meta[0], meta[1], n_rows)
    wgu = jnp.concatenate([w_gate, w_up], axis=-1).astype(BF16)
    return _experts(h, t, src, dst, first[tile_cls], second[tile_cls], n_used.reshape(1),
                    w_hi, b_router, wgu, w_down.astype(BF16), ln_g, ln_b)


def kernel(x, positions, w_in, sc_conv_w, sc_conv_b, lru_conv_w, lru_conv_b, lru_w_r, lru_b_r,
           lru_w_i, lru_b_i, lru_lambda, sg_norm_g, sg_w_s, sg_b_s, ret_norm_g, branch_proj, w_out,
           ln_mix_g, ln_mix_b, router_group_w, router_group_b, router_expert_w, router_expert_b,
           exp_w_gate, exp_w_up, exp_w_down, ln_ffn_g, ln_ffn_b):
    batch, seq, d = x.shape
    assert d == D_MODEL and seq % TS == 0 and w_in.shape[-1] == N_IN
    depth = w_in.shape[0]
    t = batch * seq
    cos_t, sin_t = _rope_tables(positions)
    dec_all, qd_tab, kd_tab, cd_tab = _retention_tables()
    avg = _block_diag(jnp.full((N_HEADS, HEAD_DIM, HEAD_DIM), 1.0 / HEAD_DIM, F32)).astype(BF16)

    h = x.reshape(t, d)
    for l in range(depth):
        w_ri = jnp.concatenate([_block_diag(lru_w_r[l]), _block_diag(lru_w_i[l])], axis=1).astype(BF16)
        b_ri = jnp.concatenate([lru_b_r[l], lru_b_i[l]])[None, :]
        bs_tab = jnp.repeat(sg_b_s[l].T, HEAD_DIM, axis=1)
        consts = (
            w_in[l].astype(BF16), sc_conv_w[l], sc_conv_b[l][None, :], lru_conv_w[l], lru_conv_b[l][None, :],
            w_ri, b_ri, lru_lambda[l][None, :], sg_norm_g[l][None, :], sg_w_s[l], bs_tab,
            ret_norm_g[l][None, :], dec_all, qd_tab, kd_tab, cd_tab, avg,
            branch_proj[l].astype(BF16), w_out[l].astype(BF16), ln_mix_g[l][None, :], ln_mix_b[l][None, :],
        )
        h = _mixer_layer(h, cos_t, sin_t, consts, batch, seq, token_major_in=l > 0)
        h = _moe_layer(h, t, router_group_w[l], router_group_b[l], router_expert_w[l], router_expert_b[l],
                       exp_w_gate[l], exp_w_up[l], exp_w_down[l], ln_ffn_g[l][None, :], ln_ffn_b[l][None, :])
    return h[:t * ROW_TILES].reshape(batch, seq, d)
```

```python
import functools

import jax
import jax.numpy as jnp
from jax import lax
from jax.experimental import pallas as pl
from jax.experimental.pallas import tpu as pltpu

F32 = jnp.float32
BF16 = jnp.bfloat16

LANES = 128
SUBLANES = 8
VMEM_LIMIT_BYTES = 56 * 1024 * 1024

D_MODEL = 1024
ROW_TILES = D_MODEL // LANES
assert ROW_TILES == SUBLANES
WIDTH = D_MODEL // 4
N_HEADS = 4
HEAD_DIM = WIDTH // N_HEADS
CHUNK = 128
SC_K = 3
LRU_K = 4
LRU_C = 8.0
ROPE_BASE = 10000.0
N_GROUPS = 4
EXPERTS_PER_GROUP = 8
N_EXPERTS = N_GROUPS * EXPERTS_PER_GROUP
D_EXPERT = D_MODEL // 4
LN_EPS = 1e-5
DEPTH = 2
ALPHA = (2.0 * DEPTH) ** 0.25

COL_SC_B, COL_SC_C, COL_SC_X, COL_LRU, COL_SG_U, COL_SG_V, COL_Q, COL_K, COL_V, COL_G = (
    i * WIDTH for i in range(10))
COL_GATES = 10 * WIDTH
N_IN = COL_GATES + 4 * D_MODEL

TS = 256
SCAN_PAD = TS // 2
CONV_PAD = SUBLANES
TB_ROUTE = 512
TB_DISPATCH = 512
DISPATCH_UNROLL = 8
TM = 128
PAIRS_PER_GROUP = EXPERTS_PER_GROUP * (EXPERTS_PER_GROUP - 1) // 2
N_CLASSES = N_GROUPS * PAIRS_PER_GROUP
assert N_CLASSES <= LANES


def _sigmoid(x):
    return 0.5 * jnp.tanh(0.5 * x) + 0.5


def _dot(a, b):
    return jnp.dot(a, b, preferred_element_type=F32)


def _load_token_major(ref, rows, lead=()):
    return jnp.concatenate(
        [ref[lead + (pl.ds(c, rows, stride=ROW_TILES), slice(None))] for c in range(ROW_TILES)], axis=1)


def _store_token_major(ref, y, rows, lead=()):
    for c in range(ROW_TILES):
        ref[lead + (pl.ds(c, rows, stride=ROW_TILES), slice(None))] = y[:, c * LANES:(c + 1) * LANES]


def _layer_norm_rows(y, g, b):
    mu = jnp.mean(y, axis=-1, keepdims=True)
    yc = y - mu
    var = jnp.mean(yc * yc, axis=-1, keepdims=True)
    return yc * lax.rsqrt(var + LN_EPS) * g + b


def _rope_table_kernel(pos_ref, freq_ref, cos_ref, sin_ref):
    ang = pos_ref[...].astype(F32) * freq_ref[...]
    lane = lax.broadcasted_iota(jnp.int32, ang.shape, 1)
    first_half = (lane % HEAD_DIM) < (HEAD_DIM // 2)
    cos_ref[...] = jnp.cos(ang)
    s = jnp.sin(ang)
    sin_ref[...] = jnp.where(first_half, -s, s)


def _rope_tables(positions):
    t = positions.size
    half = HEAD_DIM // 2
    inv_freq = ROPE_BASE ** (-jnp.arange(half, dtype=F32) / half)
    freq_row = jnp.tile(inv_freq, LANES // half)[None, :]
    rows = 1024
    return pl.pallas_call(
        _rope_table_kernel,
        out_shape=(jax.ShapeDtypeStruct((t, LANES), F32), jax.ShapeDtypeStruct((t, LANES), F32)),
        grid=(t // rows,),
        in_specs=[pl.BlockSpec((rows, 1), lambda i: (i, 0)),
                  pl.BlockSpec((1, LANES), lambda i: (0, 0))],
        out_specs=(pl.BlockSpec((rows, LANES), lambda i: (i, 0)),
                   pl.BlockSpec((rows, LANES), lambda i: (i, 0))),
        name="rope_tables",
    )(positions.reshape(t, 1), freq_row)


def _mixer_kernel(h_ref, cos_ref, sin_ref, w_in_ref, scw_ref, scb_ref, lcw_ref, lcb_ref,
                  wri_ref, bri_ref, lam_ref, sgg_ref, ws_ref, bs_ref, rg_ref,
                  dec_ref, qd_ref, kd_ref, cd_ref, avg_ref, bp_ref, wo_ref, lng_ref, lnb_ref,
                  out_ref, ubuf, xbuf, hcar, sbd, abuf, hbuf, *, token_major_in):
    step = pl.program_id(1)

    @pl.when(step == 0)
    def _():
        ubuf[0:CONV_PAD, :] = jnp.zeros((CONV_PAD, WIDTH), F32)
        xbuf[0:CONV_PAD, :] = jnp.zeros((CONV_PAD, WIDTH), F32)
        hcar[...] = jnp.zeros_like(hcar)
        sbd[...] = jnp.zeros_like(sbd)
        abuf[0:SCAN_PAD, :] = jnp.ones((SCAN_PAD, WIDTH), F32)
        hbuf[0:SCAN_PAD, :] = jnp.zeros((SCAN_PAD, WIDTH), F32)

    h = _load_token_major(h_ref, TS) if token_major_in else h_ref[...]
    hb = h.astype(BF16)

    def proj(col, width=WIDTH):
        return _dot(hb, w_in_ref[:, col:col + width])

    lane = lax.broadcasted_iota(jnp.int32, (CHUNK, WIDTH), 1)
    head_of_lane = lane // HEAD_DIM
    avg = avg_ref[...]

    def group_standardize(x):
        mean = _dot(x.astype(BF16), avg)
        xc = x - mean
        var = _dot((xc * xc).astype(BF16), avg)
        return xc * lax.rsqrt(var + LN_EPS)

    sc_b, sc_c, sc_x = proj(COL_SC_B), proj(COL_SC_C), proj(COL_SC_X)
    u = sc_c * sc_x
    ubuf[CONV_PAD:CONV_PAD + TS, :] = u
    conv = scw_ref[2:3, :] * u + scb_ref[...]
    for j in range(SC_K - 1):
        back = SC_K - 1 - j
        conv = conv + scw_ref[j:j + 1, :] * ubuf[CONV_PAD - back:CONV_PAD - back + TS, :]
    ubuf[0:CONV_PAD, :] = ubuf[TS:TS + CONV_PAD, :]
    branch_a = sc_b * conv

    lx = proj(COL_LRU)
    xbuf[CONV_PAD:CONV_PAD + TS, :] = lx
    xc = lcw_ref[LRU_K - 1:LRU_K, :] * lx + lcb_ref[...]
    for j in range(LRU_K - 1):
        back = LRU_K - 1 - j
        xc = xc + lcw_ref[j:j + 1, :] * xbuf[CONV_PAD - back:CONV_PAD - back + TS, :]
    xbuf[0:CONV_PAD, :] = xbuf[TS:TS + CONV_PAD, :]
    ri = _dot(xc.astype(BF16), wri_ref[...]) + bri_ref[...]
    r = _sigmoid(ri[:, :WIDTH])
    ig = _sigmoid(ri[:, WIDTH:])
    neg_lam = -lam_ref[...]
    softplus = jnp.maximum(neg_lam, 0.0) + jnp.log1p(jnp.exp(-jnp.abs(neg_lam)))
    log_a = (-LRU_C) * r * softplus
    a = jnp.exp(log_a)
    th = jnp.tanh(log_a)
    uu = jnp.sqrt((-2.0) * th / (1.0 - th)) * (ig * xc)
    row = lax.broadcasted_iota(jnp.int32, (TS, WIDTH), 0)
    uu = uu + jnp.where(row == 0, a * hcar[...], 0.0)
    abuf[SCAN_PAD:SCAN_PAD + TS, :] = a
    hbuf[SCAN_PAD:SCAN_PAD + TS, :] = uu
    shift = 1
    while shift < TS:
        a_cur = abuf[SCAN_PAD:SCAN_PAD + TS, :]
        h_prev = hbuf[SCAN_PAD - shift:SCAN_PAD - shift + TS, :]
        h_new = a_cur * h_prev + hbuf[SCAN_PAD:SCAN_PAD + TS, :]
        if shift * 2 < TS:
            a_new = a_cur * abuf[SCAN_PAD - shift:SCAN_PAD - shift + TS, :]
            abuf[SCAN_PAD:SCAN_PAD + TS, :] = a_new
        hbuf[SCAN_PAD:SCAN_PAD + TS, :] = h_new
        shift *= 2
    branch_b = hbuf[SCAN_PAD:SCAN_PAD + TS, :]
    hcar[...] = hbuf[SCAN_PAD + TS - 1:SCAN_PAD + TS, :]

    gu = jax.nn.gelu(proj(COL_SG_U))
    gv = jax.nn.gelu(proj(COL_SG_V))
    vn = (group_standardize(gv) * sgg_ref[...]).astype(BF16)
    trow = lax.broadcasted_iota(jnp.int32, (CHUNK, CHUNK), 0)
    tcol = lax.broadcasted_iota(jnp.int32, (CHUNK, CHUNK), 1)
    w_causal = [jnp.where(trow >= tcol, ws_ref[g], 0.0).astype(BF16) for g in range(N_HEADS)]
    sv_chunks = []
    for c in range(TS // CHUNK):
        vch = vn[c * CHUNK:(c + 1) * CHUNK, :]
        sv = bs_ref[...]
        for g in range(N_HEADS):
            sv = sv + jnp.where(head_of_lane == g, _dot(w_causal[g], vch), 0.0)
        sv_chunks.append(sv)
    branch_c = gu * jnp.concatenate(sv_chunks, axis=0)

    cos = jnp.concatenate([cos_ref[...]] * (WIDTH // LANES), axis=1)
    sin = jnp.concatenate([sin_ref[...]] * (WIDTH // LANES), axis=1)
    lane_ts = lax.broadcasted_iota(jnp.int32, (TS, WIDTH), 1)
    first_half = (lane_ts % HEAD_DIM) < (HEAD_DIM // 2)

    def rope(t):
        swapped = jnp.where(first_half,
                            pltpu.roll(t, WIDTH - HEAD_DIM // 2, 1),
                            pltpu.roll(t, HEAD_DIM // 2, 1))
        return t * cos + swapped * sin

    q = rope(proj(COL_Q))
    k = rope(proj(COL_K)) * (HEAD_DIM ** -0.5)
    v = proj(COL_V)
    zg = proj(COL_G)
    bd_mask = (lax.broadcasted_iota(jnp.int32, (WIDTH, WIDTH), 0) // HEAD_DIM
               == lax.broadcasted_iota(jnp.int32, (WIDTH, WIDTH), 1) // HEAD_DIM)
    o_chunks = []
    for c in range(TS // CHUNK):
        sl = slice(c * CHUNK, (c + 1) * CHUNK)
        qc, kc, vc = q[sl, :], k[sl, :], v[sl, :]
        qcb = qc.astype(BF16)
        kstack = jnp.concatenate(
            [jnp.where(head_of_lane == hh, kc, 0.0) for hh in range(N_HEADS)], axis=0).astype(BF16)
        vstack = jnp.concatenate(
            [jnp.where(head_of_lane == hh, vc, 0.0) for hh in range(N_HEADS)], axis=0).astype(BF16)
        scores = lax.dot_general(qcb, kstack, (((1,), (1,)), ((), ())),
                                 preferred_element_type=F32)
        inner = _dot((scores * dec_ref[...]).astype(BF16), vstack)
        state = sbd[...]
        cross = _dot(qcb, state.astype(BF16)) * qd_ref[...]
        o_chunks.append(inner + cross)
        kdec = (kc * kd_ref[...]).astype(BF16)
        kv = lax.dot_general(kdec, vc.astype(BF16), (((0,), (0,)), ((), ())),
                             preferred_element_type=F32)
        sbd[...] = cd_ref[...] * state + jnp.where(bd_mask, kv, 0.0)
    o = jnp.concatenate(o_chunks, axis=0)
    branch_d = (zg * _sigmoid(zg)) * (group_standardize(o) * rg_ref[...])

    merged = jnp.zeros((TS, D_MODEL), F32)
    for b_idx, branch in enumerate((branch_a, branch_b, branch_c, branch_d)):
        gate = _sigmoid(proj(COL_GATES + b_idx * D_MODEL, D_MODEL))
        merged = merged + gate * _dot(branch.astype(BF16), bp_ref[b_idx])
    mix = _dot(merged.astype(BF16), wo_ref[...])
    _store_token_major(out_ref, _layer_norm_rows(ALPHA * h + mix, lng_ref[...], lnb_ref[...]), TS)


def _const_spec(shape):
    nd = len(shape)
    return pl.BlockSpec(shape, lambda b, i, _nd=nd: (0,) * _nd, pipeline_mode=pl.Buffered(1))


def _mixer_layer(h, cos_t, sin_t, consts, batch, seq, token_major_in):
    t = batch * seq
    steps = seq // TS
    row_map = lambda b, i: (b * steps + i, 0)
    tm_spec = pl.BlockSpec((TS * ROW_TILES, LANES), row_map)
    in_specs = [tm_spec if token_major_in else pl.BlockSpec((TS, D_MODEL), row_map),
                pl.BlockSpec((TS, LANES), row_map),
                pl.BlockSpec((TS, LANES), row_map)]
    in_specs += [_const_spec(c.shape) for c in consts]
    return pl.pallas_call(
        functools.partial(_mixer_kernel, token_major_in=token_major_in),
        out_shape=jax.ShapeDtypeStruct((t * ROW_TILES, LANES), F32),
        grid=(batch, steps),
        in_specs=in_specs,
        out_specs=tm_spec,
        scratch_shapes=[
            pltpu.VMEM((CONV_PAD + TS, WIDTH), F32),
            pltpu.VMEM((CONV_PAD + TS, WIDTH), F32),
            pltpu.VMEM((1, WIDTH), F32),
            pltpu.VMEM((WIDTH, WIDTH), F32),
            pltpu.VMEM((SCAN_PAD + TS, WIDTH), F32),
            pltpu.VMEM((SCAN_PAD + TS, WIDTH), F32),
        ],
        compiler_params=pltpu.CompilerParams(
            dimension_semantics=("arbitrary", "arbitrary"),
            vmem_limit_bytes=VMEM_LIMIT_BYTES),
        name="mixer_layer",
    )(h, cos_t, sin_t, *consts)


def _block_diag(w):
    heads, d, e = w.shape
    eye = jnp.eye(heads, dtype=w.dtype)
    return (eye[:, None, :, None] * w[:, :, None, :]).reshape(heads * d, heads * e)


def _retention_tables():
    log_gamma = jnp.log1p(-jnp.exp2(-5.0 - jnp.arange(N_HEADS, dtype=F32)))
    pos = jnp.arange(CHUNK, dtype=F32)
    diff = pos[:, None] - pos[None, :]
    decay = jnp.where(diff >= 0, jnp.exp(jnp.maximum(diff, 0.0) * log_gamma[:, None, None]), 0.0)
    dec_all = jnp.transpose(decay, (1, 0, 2)).reshape(CHUNK, N_HEADS * CHUNK)
    q_decay = jnp.exp((pos + 1.0)[:, None] * log_gamma)
    k_decay = jnp.exp((CHUNK - 1.0 - pos)[:, None] * log_gamma)
    qd_tab = jnp.repeat(q_decay, HEAD_DIM, axis=1)
    kd_tab = jnp.repeat(k_decay, HEAD_DIM, axis=1)
    chunk_decay = jnp.repeat(jnp.exp(CHUNK * log_gamma), HEAD_DIM)
    cd_tab = jnp.broadcast_to(chunk_decay[:, None], (WIDTH, WIDTH))
    return dec_all, qd_tab, kd_tab, cd_tab


def _router_kernel(h_ref, whi_ref, wlo_ref, b_ref, meta_ref, cnt_ref, carry):
    step = pl.program_id(0)

    @pl.when(step == 0)
    def _():
        carry[...] = jnp.zeros_like(carry)

    h = _load_token_major(h_ref, TB_ROUTE)
    h_hi = h.astype(BF16)
    h_lo = (h - h_hi.astype(F32)).astype(BF16)
    whi = whi_ref[...]
    logits = _dot(h_hi, whi) + _dot(h_lo, whi) + _dot(h_hi, wlo_ref[...]) + b_ref[...]
    lane = lax.broadcasted_iota(jnp.int32, logits.shape, 1)
    big = jnp.int32(2 ** 30)
    neg_inf = F32(-jnp.inf)

    gl = jnp.where(lane < N_GROUPS, logits, neg_inf)
    gmax = jnp.max(gl, axis=-1, keepdims=True)
    g_idx = jnp.min(jnp.where(gl == gmax, lane, big), axis=-1, keepdims=True)

    e_lane = lane - N_GROUPS
    in_group = (e_lane >= g_idx * EXPERTS_PER_GROUP) & (e_lane < (g_idx + 1) * EXPERTS_PER_GROUP)
    el = jnp.where(in_group, logits, neg_inf)
    m1 = jnp.max(el, axis=-1, keepdims=True)
    i1 = jnp.min(jnp.where(el == m1, lane, big), axis=-1, keepdims=True)
    el2 = jnp.where(lane == i1, neg_inf, el)
    m2 = jnp.max(el2, axis=-1, keepdims=True)
    i2 = jnp.min(jnp.where(el2 == m2, lane, big), axis=-1, keepdims=True)
    base = (g_idx * EXPERTS_PER_GROUP + N_GROUPS).astype(F32)
    lo = jnp.minimum(i1, i2).astype(F32) - base
    hi = jnp.maximum(i1, i2).astype(F32) - base
    pair = lo * (2.0 * EXPERTS_PER_GROUP - 1.0 - lo) * 0.5 + (hi - lo - 1.0)
    cls = (g_idx.astype(F32) * PAIRS_PER_GROUP + pair).astype(jnp.int32)

    onehot = jnp.where(lane == cls, 1.0, 0.0)
    tb = h.shape[0]
    r_i = lax.broadcasted_iota(jnp.int32, (tb, tb), 0)
    c_i = lax.broadcasted_iota(jnp.int32, (tb, tb), 1)
    strict_lower = jnp.where(r_i > c_i, 1.0, 0.0).astype(BF16)
    before = _dot(strict_lower, onehot.astype(BF16)) + carry[...]
    rank = jnp.sum(jnp.where(lane == cls, before, 0.0), axis=-1, keepdims=True)
    carry[...] = carry[...] + jnp.sum(onehot, axis=0, keepdims=True)

    meta = jnp.where(lane == 0, cls.astype(F32), 0.0) + jnp.where(lane == 1, rank, 0.0)
    meta_ref[...] = jnp.transpose(meta)[0:SUBLANES, :].astype(jnp.int32)
    cnt_ref[...] = jnp.broadcast_to(carry[...], cnt_ref.shape).astype(jnp.int32)


def _router(h, t, whi, wlo, bias):
    return pl.pallas_call(
        _router_kernel,
        out_shape=(jax.ShapeDtypeStruct((SUBLANES, t), jnp.int32),
                   jax.ShapeDtypeStruct((SUBLANES, LANES), jnp.int32)),
        grid=(t // TB_ROUTE,),
        in_specs=[pl.BlockSpec((TB_ROUTE * ROW_TILES, LANES), lambda i: (i, 0)),
                  pl.BlockSpec((D_MODEL, LANES), lambda i: (0, 0)),
                  pl.BlockSpec((D_MODEL, LANES), lambda i: (0, 0)),
                  pl.BlockSpec((1, LANES), lambda i: (0, 0))],
        out_specs=(pl.BlockSpec((SUBLANES, TB_ROUTE), lambda i: (0, i)),
                   pl.BlockSpec((SUBLANES, LANES), lambda i: (0, 0))),
        scratch_shapes=[pltpu.VMEM((1, LANES), F32)],
        compiler_params=pltpu.CompilerParams(dimension_semantics=("arbitrary",)),
        name="router",
    )(h, whi, wlo, bias)


def _token_rows(tok):
    if isinstance(tok, int):
        return pl.ds(tok * ROW_TILES, ROW_TILES)
    return pl.ds(pl.multiple_of(tok * ROW_TILES, ROW_TILES), ROW_TILES)


def _dispatch_kernel(offs_ref, cls_ref, rank_ref, h_ref, dst0_hbm, xs_in_hbm, xs_hbm, dst_hbm,
                     dst_ref, row_sem, map_sem):
    del xs_in_hbm
    step = pl.program_id(0)

    @pl.when(step == 0)
    def _():
        load = pltpu.make_async_copy(dst0_hbm, dst_ref, map_sem)
        load.start()
        load.wait()

    def send(g, _):
        for k in range(DISPATCH_UNROLL):
            r = g * DISPATCH_UNROLL + k
            p = offs_ref[cls_ref[r]] + rank_ref[r]
            dst_ref[p] = step * TB_DISPATCH + r
            pltpu.make_async_copy(h_ref.at[_token_rows(r), :], xs_hbm.at[_token_rows(p), :],
                                  row_sem).start(priority=k % 2)
        return 0

    lax.fori_loop(0, TB_DISPATCH // DISPATCH_UNROLL, send, 0)

    def drain(g, _):
        for _k in range(DISPATCH_UNROLL):
            pltpu.make_async_copy(h_ref.at[_token_rows(0), :], xs_hbm.at[_token_rows(0), :], row_sem).wait()
        return 0

    lax.fori_loop(0, TB_DISPATCH // DISPATCH_UNROLL, drain, 0)

    @pl.when(step == pl.num_programs(0) - 1)
    def _():
        store = pltpu.make_async_copy(dst_ref, dst_hbm, map_sem)
        store.start()
        store.wait()


def _dispatch(h, offs, cls, rank, n_rows):
    t = cls.shape[0]
    dst0 = t + jnp.arange(n_rows, dtype=jnp.int32) % (2 * TM)
    xs0 = jnp.zeros((n_rows * ROW_TILES, LANES), F32)
    smem_blk = pl.BlockSpec((TB_DISPATCH,), lambda i, offs: (i,), memory_space=pltpu.SMEM)
    hbm = pl.BlockSpec(memory_space=pl.ANY)
    return pl.pallas_call(
        _dispatch_kernel,
        out_shape=(jax.ShapeDtypeStruct((n_rows * ROW_TILES, LANES), F32),
                   jax.ShapeDtypeStruct((n_rows,), jnp.int32)),
        grid_spec=pltpu.PrefetchScalarGridSpec(
            num_scalar_prefetch=1,
            grid=(t // TB_DISPATCH,),
            in_specs=[smem_blk, smem_blk,
                      pl.BlockSpec((TB_DISPATCH * ROW_TILES, LANES), lambda i, offs: (i, 0)),
                      hbm, hbm],
            out_specs=(hbm, hbm),
            scratch_shapes=[pltpu.SMEM((n_rows,), jnp.int32),
                            pltpu.SemaphoreType.DMA,
                            pltpu.SemaphoreType.DMA]),
        input_output_aliases={5: 0},
        compiler_params=pltpu.CompilerParams(dimension_semantics=("arbitrary",)),
        name="moe_dispatch",
    )(offs, cls, rank, h, dst0, xs0)


def _expert_kernel(ea_ref, eb_ref, nused_ref, dst_ref,
                   x_ref, wr_ref, br_ref, wgua_ref, wda_ref, wgub_ref, wdb_ref, lng_ref, lnb_ref,
                   out_hbm, ybuf, ssem, *, n_tokens):
    j = pl.program_id(0)
    n_used = nused_ref[0]
    slot = j % 2
    other = 1 - slot

    def scatter(r, s):
        return pltpu.make_async_copy(ybuf.at[s, _token_rows(r), :],
                                     out_hbm.at[_token_rows(dst_ref[r]), :], ssem.at[s])

    def scatter_done(r, s):
        return pltpu.make_async_copy(ybuf.at[s, _token_rows(r), :],
                                     out_hbm.at[_token_rows(0), :], ssem.at[s])

    @pl.when(j < n_used)
    def _():
        @pl.when(j == 0)
        def _():
            ybuf[...] = jnp.zeros_like(ybuf)

        x = _load_token_major(x_ref, TM)
        xb = x.astype(BF16)

        logits = _dot(xb, wr_ref[...]) + br_ref[...]
        lane = lax.broadcasted_iota(jnp.int32, logits.shape, 1)
        ea, eb = ea_ref[j], eb_ref[j]
        group = ea // EXPERTS_PER_GROUP
        gl = jnp.where(lane < N_GROUPS, logits, F32(-jnp.inf))
        gmax = jnp.max(gl, axis=-1, keepdims=True)
        gsum = jnp.sum(jnp.where(lane < N_GROUPS, jnp.exp(gl - gmax), 0.0), axis=-1, keepdims=True)
        lg = jnp.sum(jnp.where(lane == group, logits, 0.0), axis=-1, keepdims=True)
        g_top_p = jnp.exp(lg - gmax) / gsum
        la = jnp.sum(jnp.where(lane == ea + N_GROUPS, logits, 0.0), axis=-1, keepdims=True)
        lb = jnp.sum(jnp.where(lane == eb + N_GROUPS, logits, 0.0), axis=-1, keepdims=True)
        w_a = g_top_p / (1.0 + jnp.exp(lb - la))
        w_b = g_top_p / (1.0 + jnp.exp(la - lb))

        def expert(wgu_ref, wd_ref):
            gu = _dot(xb, wgu_ref[0])
            gate, up = gu[:, :D_EXPERT], gu[:, D_EXPERT:]
            hid = (gate * _sigmoid(gate)) * up
            return _dot(hid.astype(BF16), wd_ref[0])

        ffn = w_a * expert(wgua_ref, wda_ref) + w_b * expert(wgub_ref, wdb_ref)
        res = _layer_norm_rows(ALPHA * x + ffn, lng_ref[...], lnb_ref[...])

        @pl.when(j >= 2)
        def _():
            for r in range(TM):
                scatter_done(r, slot).wait()

        _store_token_major(ybuf, res, TM, lead=(slot,))
        for r in range(TM):
            scatter(r, slot).start(priority=r % 2)

        @pl.when(j == n_used - 1)
        def _():
            @pl.when(j >= 1)
            def _():
                for r in range(TM):
                    scatter_done(r, other).wait()

            for r in range(TM):
                scatter_done(r, slot).wait()

            for s in range(2):
                fill = pltpu.make_async_copy(
                    ybuf.at[s], out_hbm.at[pl.ds((n_tokens + s * TM) * ROW_TILES, TM * ROW_TILES), :],
                    ssem.at[s])
                fill.start()
                fill.wait()


def _experts(xs, t, dst, tile_ea, tile_eb, n_used, w_router, b_router, wgu, wd, ln_g, ln_b):
    n_tiles = dst.shape[0] // TM
    const = lambda shape: pl.BlockSpec(shape, lambda j, ea, eb, nu: (0,) * len(shape))
    wgu_spec = lambda which: pl.BlockSpec(
        (1, D_MODEL, 2 * D_EXPERT), lambda j, ea, eb, nu, _w=which: ((ea, eb)[_w][j], 0, 0))
    wd_spec = lambda which: pl.BlockSpec(
        (1, D_EXPERT, D_MODEL), lambda j, ea, eb, nu, _w=which: ((ea, eb)[_w][j], 0, 0))
    grid_spec = pltpu.PrefetchScalarGridSpec(
        num_scalar_prefetch=3,
        grid=(n_tiles,),
        in_specs=[pl.BlockSpec((TM,), lambda j, ea, eb, nu: (j,), memory_space=pltpu.SMEM),
                  pl.BlockSpec((TM * ROW_TILES, LANES), lambda j, ea, eb, nu: (j, 0)),
                  const((D_MODEL, LANES)), const((1, LANES)),
                  wgu_spec(0), wd_spec(0), wgu_spec(1), wd_spec(1),
                  const((1, D_MODEL)), const((1, D_MODEL))],
        out_specs=pl.BlockSpec(memory_space=pl.ANY),
        scratch_shapes=[pltpu.VMEM((2, TM * ROW_TILES, LANES), F32),
                        pltpu.SemaphoreType.DMA((2,))],
    )
    return pl.pallas_call(
        functools.partial(_expert_kernel, n_tokens=t),
        out_shape=jax.ShapeDtypeStruct(((t + 2 * TM) * ROW_TILES, LANES), F32),
        grid_spec=grid_spec,
        compiler_params=pltpu.CompilerParams(dimension_semantics=("arbitrary",)),
        name="moe_experts",
    )(tile_ea, tile_eb, n_used, dst, xs, w_router, b_router, wgu, wd, wgu, wd, ln_g, ln_b)


def _class_expert_tables():
    first, second = [], []
    for g in range(N_GROUPS):
        for lo in range(EXPERTS_PER_GROUP):
            for hi in range(lo + 1, EXPERTS_PER_GROUP):
                first.append(g * EXPERTS_PER_GROUP + lo)
                second.append(g * EXPERTS_PER_GROUP + hi)
    return jnp.array(first, jnp.int32), jnp.array(second, jnp.int32)


def _moe_layer(h, t, wg, bg, we, be, w_gate, w_up, w_down, ln_g, ln_b):
    n_tiles = t // TM + N_CLASSES
    n_rows = n_tiles * TM

    w_router = jnp.zeros((D_MODEL, LANES), F32)
    w_router = w_router.at[:, :N_GROUPS].set(wg).at[:, N_GROUPS:N_GROUPS + N_EXPERTS].set(we)
    b_router = jnp.zeros((1, LANES), F32)
    b_router = b_router.at[0, :N_GROUPS].set(bg).at[0, N_GROUPS:N_GROUPS + N_EXPERTS].set(be)
    w_hi = w_router.astype(BF16)
    w_lo = (w_router - w_hi.astype(F32)).astype(BF16)

    meta, counts = _router(h, t, w_hi, w_lo, b_router)

    cnt = counts[0]
    tiles_per = (cnt + TM - 1) // TM
    tile_end = jnp.cumsum(tiles_per)
    offs = (tile_end - tiles_per) * TM
    n_used = tile_end[-1]
    tile_id = jnp.arange(n_tiles, dtype=jnp.int32)
    tile_cls = jnp.sum((jnp.minimum(tile_id, n_used - 1)[:, None] >= tile_end[None, :N_CLASSES])
                       .astype(jnp.int32), axis=1)
    first, second = _class_expert_tables()
    xs, dst = _dispatch(h, offs, meta[0], meta[1], n_rows)
    wgu = jnp.concatenate([w_gate, w_up], axis=-1).astype(BF16)
    return _experts(xs, t, dst, first[tile_cls], second[tile_cls], n_used.reshape(1),
                    w_hi, b_router, wgu, w_down.astype(BF16), ln_g, ln_b)


def kernel(x, positions, w_in, sc_conv_w, sc_conv_b, lru_conv_w, lru_conv_b, lru_w_r, lru_b_r,
           lru_w_i, lru_b_i, lru_lambda, sg_norm_g, sg_w_s, sg_b_s, ret_norm_g, branch_proj, w_out,
           ln_mix_g, ln_mix_b, router_group_w, router_group_b, router_expert_w, router_expert_b,
           exp_w_gate, exp_w_up, exp_w_down, ln_ffn_g, ln_ffn_b):
    batch, seq, d = x.shape
    assert d == D_MODEL and seq % TS == 0 and w_in.shape[-1] == N_IN
    depth = w_in.shape[0]
    t = batch * seq
    cos_t, sin_t = _rope_tables(positions)
    dec_all, qd_tab, kd_tab, cd_tab = _retention_tables()
    avg = _block_diag(jnp.full((N_HEADS, HEAD_DIM, HEAD_DIM), 1.0 / HEAD_DIM, F32)).astype(BF16)

    h = x.reshape(t, d)
    for l in range(depth):
        w_ri = jnp.concatenate([_block_diag(lru_w_r[l]), _block_diag(lru_w_i[l])], axis=1).astype(BF16)
        b_ri = jnp.concatenate([lru_b_r[l], lru_b_i[l]])[None, :]
        bs_tab = jnp.repeat(sg_b_s[l].T, HEAD_DIM, axis=1)
        consts = (
            w_in[l].astype(BF16), sc_conv_w[l], sc_conv_b[l][None, :], lru_conv_w[l], lru_conv_b[l][None, :],
            w_ri, b_ri, lru_lambda[l][None, :], sg_norm_g[l][None, :], sg_w_s[l], bs_tab,
            ret_norm_g[l][None, :], dec_all, qd_tab, kd_tab, cd_tab, avg,
            branch_proj[l].astype(BF16), w_out[l].astype(BF16), ln_mix_g[l][None, :], ln_mix_b[l][None, :],
        )
        h = _mixer_layer(h, cos_t, sin_t, consts, batch, seq, token_major_in=l > 0)
        h = _moe_layer(h, t, router_group_w[l], router_group_b[l], router_expert_w[l], router_expert_b[l],
                       exp_w_gate[l], exp_w_up[l], exp_w_down[l], ln_ffn_g[l][None, :], ln_ffn_b[l][None, :])
    return h[:t * ROW_TILES].reshape(batch, seq, d)
```

```python
import functools

import jax
import jax.numpy as jnp
from jax import lax
from jax.experimental import pallas as pl
from jax.experimental.pallas import tpu as pltpu

F32 = jnp.float32
BF16 = jnp.bfloat16

LANES = 128
SUBLANES = 8
VMEM_LIMIT_BYTES = 56 * 1024 * 1024

D_MODEL = 1024
ROW_TILES = D_MODEL // LANES
assert ROW_TILES == SUBLANES
WIDTH = D_MODEL // 4
N_HEADS = 4
HEAD_DIM = WIDTH // N_HEADS
CHUNK = 128
SC_K = 3
LRU_K = 4
LRU_C = 8.0
ROPE_BASE = 10000.0
N_GROUPS = 4
EXPERTS_PER_GROUP = 8
N_EXPERTS = N_GROUPS * EXPERTS_PER_GROUP
D_EXPERT = D_MODEL // 4
LN_EPS = 1e-5
DEPTH = 2
ALPHA = (2.0 * DEPTH) ** 0.25

COL_SC_B, COL_SC_C, COL_SC_X, COL_LRU, COL_SG_U, COL_SG_V, COL_Q, COL_K, COL_V, COL_G = (
    i * WIDTH for i in range(10))
COL_GATES = 10 * WIDTH
N_IN = COL_GATES + 4 * D_MODEL

TS = 512
SCAN_PAD = TS // 2
CONV_PAD = SUBLANES
TB_ROUTE = 512
TB_DISPATCH = 512
DISPATCH_UNROLL = 8
TM = 128
PAIRS_PER_GROUP = EXPERTS_PER_GROUP * (EXPERTS_PER_GROUP - 1) // 2
N_CLASSES = N_GROUPS * PAIRS_PER_GROUP
assert N_CLASSES <= LANES


def _sigmoid(x):
    return 0.5 * jnp.tanh(0.5 * x) + 0.5


def _dot(a, b):
    return jnp.dot(a, b, preferred_element_type=F32)


def _load_token_major(ref, rows, lead=()):
    return jnp.concatenate(
        [ref[lead + (pl.ds(c, rows, stride=ROW_TILES), slice(None))] for c in range(ROW_TILES)], axis=1)


def _store_token_major(ref, y, rows, lead=()):
    for c in range(ROW_TILES):
        ref[lead + (pl.ds(c, rows, stride=ROW_TILES), slice(None))] = y[:, c * LANES:(c + 1) * LANES]


def _layer_norm_rows(y, g, b):
    mu = jnp.mean(y, axis=-1, keepdims=True)
    yc = y - mu
    var = jnp.mean(yc * yc, axis=-1, keepdims=True)
    return yc * lax.rsqrt(var + LN_EPS) * g + b


def _rope_table_kernel(pos_ref, freq_ref, cos_ref, sin_ref):
    ang = pos_ref[...].astype(F32) * freq_ref[...]
    lane = lax.broadcasted_iota(jnp.int32, ang.shape, 1)
    first_half = (lane % HEAD_DIM) < (HEAD_DIM // 2)
    cos_ref[...] = jnp.cos(ang)
    s = jnp.sin(ang)
    sin_ref[...] = jnp.where(first_half, -s, s)


def _rope_tables(positions):
    t = positions.size
    half = HEAD_DIM // 2
    inv_freq = ROPE_BASE ** (-jnp.arange(half, dtype=F32) / half)
    freq_row = jnp.tile(inv_freq, LANES // half)[None, :]
    rows = 1024
    return pl.pallas_call(
        _rope_table_kernel,
        out_shape=(jax.ShapeDtypeStruct((t, LANES), F32), jax.ShapeDtypeStruct((t, LANES), F32)),
        grid=(t // rows,),
        in_specs=[pl.BlockSpec((rows, 1), lambda i: (i, 0)),
                  pl.BlockSpec((1, LANES), lambda i: (0, 0))],
        out_specs=(pl.BlockSpec((rows, LANES), lambda i: (i, 0)),
                   pl.BlockSpec((rows, LANES), lambda i: (i, 0))),
        name="rope_tables",
    )(positions.reshape(t, 1), freq_row)


def _mixer_kernel(h_ref, cos_ref, sin_ref, w_in_ref, scw_ref, scb_ref, lcw_ref, lcb_ref,
                  wri_ref, bri_ref, lam_ref, sgg_ref, ws_ref, bs_ref, rg_ref,
                  dec_ref, qd_ref, kd_ref, cd_ref, avg_ref, bp_ref, wo_ref, lng_ref, lnb_ref,
                  out_ref, ubuf, xbuf, hcar, sbd, abuf, hbuf, *, token_major_in):
    step = pl.program_id(1)

    @pl.when(step == 0)
    def _():
        ubuf[0:CONV_PAD, :] = jnp.zeros((CONV_PAD, WIDTH), F32)
        xbuf[0:CONV_PAD, :] = jnp.zeros((CONV_PAD, WIDTH), F32)
        hcar[...] = jnp.zeros_like(hcar)
        sbd[...] = jnp.zeros_like(sbd)
        abuf[0:SCAN_PAD, :] = jnp.ones((SCAN_PAD, WIDTH), F32)
        hbuf[0:SCAN_PAD, :] = jnp.zeros((SCAN_PAD, WIDTH), F32)

    h = _load_token_major(h_ref, TS) if token_major_in else h_ref[...]
    hb = h.astype(BF16)

    def proj(col, width=WIDTH):
        return _dot(hb, w_in_ref[:, col:col + width])

    lane = lax.broadcasted_iota(jnp.int32, (CHUNK, WIDTH), 1)
    head_of_lane = lane // HEAD_DIM
    avg = avg_ref[...]

    def group_standardize(x):
        mean = _dot(x.astype(BF16), avg)
        xc = x - mean
        var = _dot((xc * xc).astype(BF16), avg)
        return xc * lax.rsqrt(var + LN_EPS)

    sc_b, sc_c, sc_x = proj(COL_SC_B), proj(COL_SC_C), proj(COL_SC_X)
    u = sc_c * sc_x
    ubuf[CONV_PAD:CONV_PAD + TS, :] = u
    conv = scw_ref[2:3, :] * u + scb_ref[...]
    for j in range(SC_K - 1):
        back = SC_K - 1 - j
        conv = conv + scw_ref[j:j + 1, :] * ubuf[CONV_PAD - back:CONV_PAD - back + TS, :]
    ubuf[0:CONV_PAD, :] = ubuf[TS:TS + CONV_PAD, :]
    branch_a = sc_b * conv

    lx = proj(COL_LRU)
    xbuf[CONV_PAD:CONV_PAD + TS, :] = lx
    xc = lcw_ref[LRU_K - 1:LRU_K, :] * lx + lcb_ref[...]
    for j in range(LRU_K - 1):
        back = LRU_K - 1 - j
        xc = xc + lcw_ref[j:j + 1, :] * xbuf[CONV_PAD - back:CONV_PAD - back + TS, :]
    xbuf[0:CONV_PAD, :] = xbuf[TS:TS + CONV_PAD, :]
    ri = _dot(xc.astype(BF16), wri_ref[...]) + bri_ref[...]
    r = _sigmoid(ri[:, :WIDTH])
    ig = _sigmoid(ri[:, WIDTH:])
    neg_lam = -lam_ref[...]
    softplus = jnp.maximum(neg_lam, 0.0) + jnp.log1p(jnp.exp(-jnp.abs(neg_lam)))
    log_a = (-LRU_C) * r * softplus
    a = jnp.exp(log_a)
    th = jnp.tanh(log_a)
    uu = jnp.sqrt((-2.0) * th / (1.0 - th)) * (ig * xc)
    row = lax.broadcasted_iota(jnp.int32, (TS, WIDTH), 0)
    uu = uu + jnp.where(row == 0, a * hcar[...], 0.0)
    abuf[SCAN_PAD:SCAN_PAD + TS, :] = a
    hbuf[SCAN_PAD:SCAN_PAD + TS, :] = uu
    shift = 1
    while shift < TS:
        a_cur = abuf[SCAN_PAD:SCAN_PAD + TS, :]
        h_prev = hbuf[SCAN_PAD - shift:SCAN_PAD - shift + TS, :]
        h_new = a_cur * h_prev + hbuf[SCAN_PAD:SCAN_PAD + TS, :]
        if shift * 2 < TS:
            a_new = a_cur * abuf[SCAN_PAD - shift:SCAN_PAD - shift + TS, :]
            abuf[SCAN_PAD:SCAN_PAD + TS, :] = a_new
        hbuf[SCAN_PAD:SCAN_PAD + TS, :] = h_new
        shift *= 2
    branch_b = hbuf[SCAN_PAD:SCAN_PAD + TS, :]
    hcar[...] = hbuf[SCAN_PAD + TS - 1:SCAN_PAD + TS, :]

    gu = jax.nn.gelu(proj(COL_SG_U))
    gv = jax.nn.gelu(proj(COL_SG_V))
    vn = (group_standardize(gv) * sgg_ref[...]).astype(BF16)
    trow = lax.broadcasted_iota(jnp.int32, (CHUNK, CHUNK), 0)
    tcol = lax.broadcasted_iota(jnp.int32, (CHUNK, CHUNK), 1)
    w_causal = [jnp.where(trow >= tcol, ws_ref[g], 0.0).astype(BF16) for g in range(N_HEADS)]
    sv_chunks = []
    for c in range(TS // CHUNK):
        vch = vn[c * CHUNK:(c + 1) * CHUNK, :]
        sv = bs_ref[...]
        for g in range(N_HEADS):
            sv = sv + jnp.where(head_of_lane == g, _dot(w_causal[g], vch), 0.0)
        sv_chunks.append(sv)
    branch_c = gu * jnp.concatenate(sv_chunks, axis=0)

    cos = jnp.concatenate([cos_ref[...]] * (WIDTH // LANES), axis=1)
    sin = jnp.concatenate([sin_ref[...]] * (WIDTH // LANES), axis=1)
    lane_ts = lax.broadcasted_iota(jnp.int32, (TS, WIDTH), 1)
    first_half = (lane_ts % HEAD_DIM) < (HEAD_DIM // 2)

    def rope(t):
        swapped = jnp.where(first_half,
                            pltpu.roll(t, WIDTH - HEAD_DIM // 2, 1),
                            pltpu.roll(t, HEAD_DIM // 2, 1))
        return t * cos + swapped * sin

    q = rope(proj(COL_Q))
    k = rope(proj(COL_K)) * (HEAD_DIM ** -0.5)
    v = proj(COL_V)
    zg = proj(COL_G)
    bd_mask = (lax.broadcasted_iota(jnp.int32, (WIDTH, WIDTH), 0) // HEAD_DIM
               == lax.broadcasted_iota(jnp.int32, (WIDTH, WIDTH), 1) // HEAD_DIM)
    o_chunks = []
    for c in range(TS // CHUNK):
        sl = slice(c * CHUNK, (c + 1) * CHUNK)
        qc, kc, vc = q[sl, :], k[sl, :], v[sl, :]
        qcb = qc.astype(BF16)
        kstack = jnp.concatenate(
            [jnp.where(head_of_lane == hh, kc, 0.0) for hh in range(N_HEADS)], axis=0).astype(BF16)
        vstack = jnp.concatenate(
            [jnp.where(head_of_lane == hh, vc, 0.0) for hh in range(N_HEADS)], axis=0).astype(BF16)
        scores = lax.dot_general(qcb, kstack, (((1,), (1,)), ((), ())),
                                 preferred_element_type=F32)
        inner = _dot((scores * dec_ref[...]).astype(BF16), vstack)
        state = sbd[...]
        cross = _dot(qcb, state.astype(BF16)) * qd_ref[...]
        o_chunks.append(inner + cross)
        kdec = (kc * kd_ref[...]).astype(BF16)
        kv = lax.dot_general(kdec, vc.astype(BF16), (((0,), (0,)), ((), ())),
                             preferred_element_type=F32)
        sbd[...] = cd_ref[...] * state + jnp.where(bd_mask, kv, 0.0)
    o = jnp.concatenate(o_chunks, axis=0)
    branch_d = (zg * _sigmoid(zg)) * (group_standardize(o) * rg_ref[...])

    merged = jnp.zeros((TS, D_MODEL), F32)
    for b_idx, branch in enumerate((branch_a, branch_b, branch_c, branch_d)):
        gate = _sigmoid(proj(COL_GATES + b_idx * D_MODEL, D_MODEL))
        merged = merged + gate * _dot(branch.astype(BF16), bp_ref[b_idx])
    mix = _dot(merged.astype(BF16), wo_ref[...])
    _store_token_major(out_ref, _layer_norm_rows(ALPHA * h + mix, lng_ref[...], lnb_ref[...]), TS)


def _const_spec(shape):
    nd = len(shape)
    return pl.BlockSpec(shape, lambda b, i, _nd=nd: (0,) * _nd, pipeline_mode=pl.Buffered(1))


def _mixer_layer(h, cos_t, sin_t, consts, batch, seq, token_major_in):
    t = batch * seq
    steps = seq // TS
    row_map = lambda b, i: (b * steps + i, 0)
    tm_spec = pl.BlockSpec((TS * ROW_TILES, LANES), row_map)
    in_specs = [tm_spec if token_major_in else pl.BlockSpec((TS, D_MODEL), row_map),
                pl.BlockSpec((TS, LANES), row_map),
                pl.BlockSpec((TS, LANES), row_map)]
    in_specs += [_const_spec(c.shape) for c in consts]
    return pl.pallas_call(
        functools.partial(_mixer_kernel, token_major_in=token_major_in),
        out_shape=jax.ShapeDtypeStruct((t * ROW_TILES, LANES), F32),
        grid=(batch, steps),
        in_specs=in_specs,
        out_specs=tm_spec,
        scratch_shapes=[
            pltpu.VMEM((CONV_PAD + TS, WIDTH), F32),
            pltpu.VMEM((CONV_PAD + TS, WIDTH), F32),
            pltpu.VMEM((1, WIDTH), F32),
            pltpu.VMEM((WIDTH, WIDTH), F32),
            pltpu.VMEM((SCAN_PAD + TS, WIDTH), F32),
            pltpu.VMEM((SCAN_PAD + TS, WIDTH), F32),
        ],
        compiler_params=pltpu.CompilerParams(
            dimension_semantics=("arbitrary", "arbitrary"),
            vmem_limit_bytes=VMEM_LIMIT_BYTES),
        name="mixer_layer",
    )(h, cos_t, sin_t, *consts)


def _block_diag(w):
    heads, d, e = w.shape
    eye = jnp.eye(heads, dtype=w.dtype)
    return (eye[:, None, :, None] * w[:, :, None, :]).reshape(heads * d, heads * e)


def _retention_tables():
    log_gamma = jnp.log1p(-jnp.exp2(-5.0 - jnp.arange(N_HEADS, dtype=F32)))
    pos = jnp.arange(CHUNK, dtype=F32)
    diff = pos[:, None] - pos[None, :]
    decay = jnp.where(diff >= 0, jnp.exp(jnp.maximum(diff, 0.0) * log_gamma[:, None, None]), 0.0)
    dec_all = jnp.transpose(decay, (1, 0, 2)).reshape(CHUNK, N_HEADS * CHUNK)
    q_decay = jnp.exp((pos + 1.0)[:, None] * log_gamma)
    k_decay = jnp.exp((CHUNK - 1.0 - pos)[:, None] * log_gamma)
    qd_tab = jnp.repeat(q_decay, HEAD_DIM, axis=1)
    kd_tab = jnp.repeat(k_decay, HEAD_DIM, axis=1)
    chunk_decay = jnp.repeat(jnp.exp(CHUNK * log_gamma), HEAD_DIM)
    cd_tab = jnp.broadcast_to(chunk_decay[:, None], (WIDTH, WIDTH))
    return dec_all, qd_tab, kd_tab, cd_tab


def _router_kernel(h_ref, whi_ref, wlo_ref, b_ref, meta_ref, cnt_ref, carry):
    step = pl.program_id(0)

    @pl.when(step == 0)
    def _():
        carry[...] = jnp.zeros_like(carry)

    h = _load_token_major(h_ref, TB_ROUTE)
    h_hi = h.astype(BF16)
    h_lo = (h - h_hi.astype(F32)).astype(BF16)
    whi = whi_ref[...]
    logits = _dot(h_hi, whi) + _dot(h_lo, whi) + _dot(h_hi, wlo_ref[...]) + b_ref[...]
    lane = lax.broadcasted_iota(jnp.int32, logits.shape, 1)
    big = jnp.int32(2 ** 30)
    neg_inf = F32(-jnp.inf)

    gl = jnp.where(lane < N_GROUPS, logits, neg_inf)
    gmax = jnp.max(gl, axis=-1, keepdims=True)
    g_idx = jnp.min(jnp.where(gl == gmax, lane, big), axis=-1, keepdims=True)

    e_lane = lane - N_GROUPS
    in_group = (e_lane >= g_idx * EXPERTS_PER_GROUP) & (e_lane < (g_idx + 1) * EXPERTS_PER_GROUP)
    el = jnp.where(in_group, logits, neg_inf)
    m1 = jnp.max(el, axis=-1, keepdims=True)
    i1 = jnp.min(jnp.where(el == m1, lane, big), axis=-1, keepdims=True)
    el2 = jnp.where(lane == i1, neg_inf, el)
    m2 = jnp.max(el2, axis=-1, keepdims=True)
    i2 = jnp.min(jnp.where(el2 == m2, lane, big), axis=-1, keepdims=True)
    base = (g_idx * EXPERTS_PER_GROUP + N_GROUPS).astype(F32)
    lo = jnp.minimum(i1, i2).astype(F32) - base
    hi = jnp.maximum(i1, i2).astype(F32) - base
    pair = lo * (2.0 * EXPERTS_PER_GROUP - 1.0 - lo) * 0.5 + (hi - lo - 1.0)
    cls = (g_idx.astype(F32) * PAIRS_PER_GROUP + pair).astype(jnp.int32)

    onehot = jnp.where(lane == cls, 1.0, 0.0)
    tb = h.shape[0]
    r_i = lax.broadcasted_iota(jnp.int32, (tb, tb), 0)
    c_i = lax.broadcasted_iota(jnp.int32, (tb, tb), 1)
    strict_lower = jnp.where(r_i > c_i, 1.0, 0.0).astype(BF16)
    before = _dot(strict_lower, onehot.astype(BF16)) + carry[...]
    rank = jnp.sum(jnp.where(lane == cls, before, 0.0), axis=-1, keepdims=True)
    carry[...] = carry[...] + jnp.sum(onehot, axis=0, keepdims=True)

    meta = jnp.where(lane == 0, cls.astype(F32), 0.0) + jnp.where(lane == 1, rank, 0.0)
    meta_ref[...] = jnp.transpose(meta)[0:SUBLANES, :].astype(jnp.int32)
    cnt_ref[...] = jnp.broadcast_to(carry[...], cnt_ref.shape).astype(jnp.int32)


def _router(h, t, whi, wlo, bias):
    return pl.pallas_call(
        _router_kernel,
        out_shape=(jax.ShapeDtypeStruct((SUBLANES, t), jnp.int32),
                   jax.ShapeDtypeStruct((SUBLANES, LANES), jnp.int32)),
        grid=(t // TB_ROUTE,),
        in_specs=[pl.BlockSpec((TB_ROUTE * ROW_TILES, LANES), lambda i: (i, 0)),
                  pl.BlockSpec((D_MODEL, LANES), lambda i: (0, 0)),
                  pl.BlockSpec((D_MODEL, LANES), lambda i: (0, 0)),
                  pl.BlockSpec((1, LANES), lambda i: (0, 0))],
        out_specs=(pl.BlockSpec((SUBLANES, TB_ROUTE), lambda i: (0, i)),
                   pl.BlockSpec((SUBLANES, LANES), lambda i: (0, 0))),
        scratch_shapes=[pltpu.VMEM((1, LANES), F32)],
        compiler_params=pltpu.CompilerParams(dimension_semantics=("arbitrary",)),
        name="router",
    )(h, whi, wlo, bias)


def _token_rows(tok):
    if isinstance(tok, int):
        return pl.ds(tok * ROW_TILES, ROW_TILES)
    return pl.ds(pl.multiple_of(tok * ROW_TILES, ROW_TILES), ROW_TILES)


def _dispatch_kernel(offs_ref, cls_ref, rank_ref, h_ref, dst0_hbm, xs_in_hbm, xs_hbm, dst_hbm,
                     dst_ref, row_sem, map_sem):
    del xs_in_hbm
    step = pl.program_id(0)

    @pl.when(step == 0)
    def _():
        load = pltpu.make_async_copy(dst0_hbm, dst_ref, map_sem)
        load.start()
        load.wait()

    def send(g, _):
        for k in range(DISPATCH_UNROLL):
            r = g * DISPATCH_UNROLL + k
            p = offs_ref[cls_ref[r]] + rank_ref[r]
            dst_ref[p] = step * TB_DISPATCH + r
            pltpu.make_async_copy(h_ref.at[_token_rows(r), :], xs_hbm.at[_token_rows(p), :],
                                  row_sem).start(priority=k % 2)
        return 0

    lax.fori_loop(0, TB_DISPATCH // DISPATCH_UNROLL, send, 0)

    def drain(g, _):
        for _k in range(DISPATCH_UNROLL):
            pltpu.make_async_copy(h_ref.at[_token_rows(0), :], xs_hbm.at[_token_rows(0), :], row_sem).wait()
        return 0

    lax.fori_loop(0, TB_DISPATCH // DISPATCH_UNROLL, drain, 0)

    @pl.when(step == pl.num_programs(0) - 1)
    def _():
        store = pltpu.make_async_copy(dst_ref, dst_hbm, map_sem)
        store.start()
        store.wait()


def _dispatch(h, offs, cls, rank, n_rows):
    t = cls.shape[0]
    dst0 = jnp.zeros((n_rows,), jnp.int32)
    xs0 = jnp.zeros((n_rows * ROW_TILES, LANES), F32)
    smem_blk = pl.BlockSpec((TB_DISPATCH,), lambda i, offs: (i,), memory_space=pltpu.SMEM)
    hbm = pl.BlockSpec(memory_space=pl.ANY)
    return pl.pallas_call(
        _dispatch_kernel,
        out_shape=(jax.ShapeDtypeStruct((n_rows * ROW_TILES, LANES), F32),
                   jax.ShapeDtypeStruct((n_rows,), jnp.int32)),
        grid_spec=pltpu.PrefetchScalarGridSpec(
            num_scalar_prefetch=1,
            grid=(t // TB_DISPATCH,),
            in_specs=[smem_blk, smem_blk,
                      pl.BlockSpec((TB_DISPATCH * ROW_TILES, LANES), lambda i, offs: (i, 0)),
                      hbm, hbm],
            out_specs=(hbm, hbm),
            scratch_shapes=[pltpu.SMEM((n_rows,), jnp.int32),
                            pltpu.SemaphoreType.DMA,
                            pltpu.SemaphoreType.DMA]),
        input_output_aliases={5: 0},
        compiler_params=pltpu.CompilerParams(dimension_semantics=("arbitrary",)),
        name="moe_dispatch",
    )(offs, cls, rank, h, dst0, xs0)


def _expert_kernel(ea_ref, eb_ref, nvalid_ref, nused_ref, dst_ref,
                   x_ref, wr_ref, br_ref, wgua_ref, wda_ref, wgub_ref, wdb_ref, lng_ref, lnb_ref,
                   out_hbm, ybuf, ssem):
    j = pl.program_id(0)
    n_used = nused_ref[0]
    slot = j % 2
    other = 1 - slot

    def start_rows(n, s):
        for r in range(TM):
            @pl.when(r < n)
            def _(r=r):
                pltpu.make_async_copy(ybuf.at[s, _token_rows(r), :],
                                      out_hbm.at[_token_rows(dst_ref[r]), :], ssem.at[s]).start(priority=r % 2)

    def wait_rows(n, s):
        def body(r, _):
            pltpu.make_async_copy(ybuf.at[s, _token_rows(0), :],
                                  out_hbm.at[_token_rows(0), :], ssem.at[s]).wait()
            return 0
        lax.fori_loop(0, n, body, 0)

    @pl.when(j < n_used)
    def _():
        xb = _load_token_major(x_ref, TM).astype(BF16)

        logits = _dot(xb, wr_ref[...]) + br_ref[...]
        lane = lax.broadcasted_iota(jnp.int32, logits.shape, 1)
        ea, eb = ea_ref[j], eb_ref[j]
        group = ea // EXPERTS_PER_GROUP
        gl = jnp.where(lane < N_GROUPS, logits, F32(-jnp.inf))
        gmax = jnp.max(gl, axis=-1, keepdims=True)
        gsum = jnp.sum(jnp.where(lane < N_GROUPS, jnp.exp(gl - gmax), 0.0), axis=-1, keepdims=True)
        lg = jnp.sum(jnp.where(lane == group, logits, 0.0), axis=-1, keepdims=True)
        g_top_p = jnp.exp(lg - gmax) / gsum
        la = jnp.sum(jnp.where(lane == ea + N_GROUPS, logits, 0.0), axis=-1, keepdims=True)
        lb = jnp.sum(jnp.where(lane == eb + N_GROUPS, logits, 0.0), axis=-1, keepdims=True)
        w_a = g_top_p / (1.0 + jnp.exp(lb - la))
        w_b = g_top_p / (1.0 + jnp.exp(la - lb))

        def hidden(wgu_ref, weight):
            gu = _dot(xb, wgu_ref[0])
            gate, up = gu[:, :D_EXPERT], gu[:, D_EXPERT:]
            return ((gate * _sigmoid(gate)) * up * weight).astype(BF16)

        ffn = _dot(hidden(wgua_ref, w_a), wda_ref[0]) + _dot(hidden(wgub_ref, w_b), wdb_ref[0])
        res = _layer_norm_rows(ALPHA * _load_token_major(x_ref, TM) + ffn, lng_ref[...], lnb_ref[...])

        @pl.when(j >= 2)
        def _():
            wait_rows(nvalid_ref[jnp.maximum(j - 2, 0)], slot)

        _store_token_major(ybuf, res, TM, lead=(slot,))
        start_rows(nvalid_ref[j], slot)

        @pl.when(j == n_used - 1)
        def _():
            @pl.when(j >= 1)
            def _():
                wait_rows(nvalid_ref[jnp.maximum(j - 1, 0)], other)

            wait_rows(nvalid_ref[j], slot)


def _experts(xs, t, dst, tile_ea, tile_eb, tile_nvalid, n_used, w_router, b_router, wgu, wd, ln_g, ln_b):
    n_tiles = dst.shape[0] // TM
    const = lambda shape: pl.BlockSpec(shape, lambda j, ea, eb, nv, nu: (0,) * len(shape))
    wgu_spec = lambda which: pl.BlockSpec(
        (1, D_MODEL, 2 * D_EXPERT), lambda j, ea, eb, nv, nu, _w=which: ((ea, eb)[_w][j], 0, 0))
    wd_spec = lambda which: pl.BlockSpec(
        (1, D_EXPERT, D_MODEL), lambda j, ea, eb, nv, nu, _w=which: ((ea, eb)[_w][j], 0, 0))
    grid_spec = pltpu.PrefetchScalarGridSpec(
        num_scalar_prefetch=4,
        grid=(n_tiles,),
        in_specs=[pl.BlockSpec((TM,), lambda j, ea, eb, nv, nu: (j,), memory_space=pltpu.SMEM),
                  pl.BlockSpec((TM * ROW_TILES, LANES), lambda j, ea, eb, nv, nu: (j, 0)),
                  const((D_MODEL, LANES)), const((1, LANES)),
                  wgu_spec(0), wd_spec(0), wgu_spec(1), wd_spec(1),
                  const((1, D_MODEL)), const((1, D_MODEL))],
        out_specs=pl.BlockSpec(memory_space=pl.ANY),
        scratch_shapes=[pltpu.VMEM((2, TM * ROW_TILES, LANES), F32),
                        pltpu.SemaphoreType.DMA((2,))],
    )
    return pl.pallas_call(
        _expert_kernel,
        out_shape=jax.ShapeDtypeStruct((t * ROW_TILES, LANES), F32),
        grid_spec=grid_spec,
        compiler_params=pltpu.CompilerParams(dimension_semantics=("arbitrary",)),
        name="moe_experts",
    )(tile_ea, tile_eb, tile_nvalid, n_used, dst, xs, w_router, b_router, wgu, wd, wgu, wd, ln_g, ln_b)


def _class_expert_tables():
    first, second = [], []
    for g in range(N_GROUPS):
        for lo in range(EXPERTS_PER_GROUP):
            for hi in range(lo + 1, EXPERTS_PER_GROUP):
                first.append(g * EXPERTS_PER_GROUP + lo)
                second.append(g * EXPERTS_PER_GROUP + hi)
    return jnp.array(first, jnp.int32), jnp.array(second, jnp.int32)


def _moe_layer(h, t, wg, bg, we, be, w_gate, w_up, w_down, ln_g, ln_b):
    n_tiles = t // TM + N_CLASSES
    n_rows = n_tiles * TM

    w_router = jnp.zeros((D_MODEL, LANES), F32)
    w_router = w_router.at[:, :N_GROUPS].set(wg).at[:, N_GROUPS:N_GROUPS + N_EXPERTS].set(we)
    b_router = jnp.zeros((1, LANES), F32)
    b_router = b_router.at[0, :N_GROUPS].set(bg).at[0, N_GROUPS:N_GROUPS + N_EXPERTS].set(be)
    w_hi = w_router.astype(BF16)
    w_lo = (w_router - w_hi.astype(F32)).astype(BF16)

    meta, counts = _router(h, t, w_hi, w_lo, b_router)

    cnt = counts[0]
    tiles_per = (cnt + TM - 1) // TM
    tile_end = jnp.cumsum(tiles_per)
    offs = (tile_end - tiles_per) * TM
    n_used = tile_end[-1]
    tile_id = jnp.arange(n_tiles, dtype=jnp.int32)
    tile_cls = jnp.sum((jnp.minimum(tile_id, n_used - 1)[:, None] >= tile_end[None, :N_CLASSES])
                       .astype(jnp.int32), axis=1)
    seg_tile = tile_id - (tile_end - tiles_per)[tile_cls]
    tile_nvalid = jnp.where(tile_id < n_used, jnp.clip(cnt[tile_cls] - seg_tile * TM, 0, TM), 0)
    first, second = _class_expert_tables()
    xs, dst = _dispatch(h, offs, meta[0], meta[1], n_rows)
    wgu = jnp.concatenate([w_gate, w_up], axis=-1).astype(BF16)
    return _experts(xs, t, dst, first[tile_cls], second[tile_cls], tile_nvalid, n_used.reshape(1),
                    w_hi, b_router, wgu, w_down.astype(BF16), ln_g, ln_b)


def kernel(x, positions, w_in, sc_conv_w, sc_conv_b, lru_conv_w, lru_conv_b, lru_w_r, lru_b_r,
           lru_w_i, lru_b_i, lru_lambda, sg_norm_g, sg_w_s, sg_b_s, ret_norm_g, branch_proj, w_out,
           ln_mix_g, ln_mix_b, router_group_w, router_group_b, router_expert_w, router_expert_b,
           exp_w_gate, exp_w_up, exp_w_down, ln_ffn_g, ln_ffn_b):
    batch, seq, d = x.shape
    assert d == D_MODEL and seq % TS == 0 and w_in.shape[-1] == N_IN
    depth = w_in.shape[0]
    t = batch * seq
    cos_t, sin_t = _rope_tables(positions)
    dec_all, qd_tab, kd_tab, cd_tab = _retention_tables()
    avg = _block_diag(jnp.full((N_HEADS, HEAD_DIM, HEAD_DIM), 1.0 / HEAD_DIM, F32)).astype(BF16)

    h = x.reshape(t, d)
    for l in range(depth):
        w_ri = jnp.concatenate([_block_diag(lru_w_r[l]), _block_diag(lru_w_i[l])], axis=1).astype(BF16)
        b_ri = jnp.concatenate([lru_b_r[l], lru_b_i[l]])[None, :]
        bs_tab = jnp.repeat(sg_b_s[l].T, HEAD_DIM, axis=1)
        consts = (
            w_in[l].astype(BF16), sc_conv_w[l], sc_conv_b[l][None, :], lru_conv_w[l], lru_conv_b[l][None, :],
            w_ri, b_ri, lru_lambda[l][None, :], sg_norm_g[l][None, :], sg_w_s[l], bs_tab,
            ret_norm_g[l][None, :], dec_all, qd_tab, kd_tab, cd_tab, avg,
            branch_proj[l].astype(BF16), w_out[l].astype(BF16), ln_mix_g[l][None, :], ln_mix_b[l][None, :],
        )
        h = _mixer_layer(h, cos_t, sin_t, consts, batch, seq, token_major_in=l > 0)
        h = _moe_layer(h, t, router_group_w[l], router_group_b[l], router_expert_w[l], router_expert_b[l],
                       exp_w_gate[l], exp_w_up[l], exp_w_down[l], ln_ffn_g[l][None, :], ln_ffn_b[l][None, :])
    return h.reshape(batch, seq, d)
```

```python
import functools

import jax
import jax.numpy as jnp
from jax import lax
from jax.experimental import pallas as pl
from jax.experimental.pallas import tpu as pltpu

F32 = jnp.float32
BF16 = jnp.bfloat16

LANES = 128
SUBLANES = 8
VMEM_LIMIT_BYTES = 56 * 1024 * 1024

D_MODEL = 1024
ROW_TILES = D_MODEL // LANES
assert ROW_TILES == SUBLANES
WIDTH = D_MODEL // 4
N_HEADS = 4
HEAD_DIM = WIDTH // N_HEADS
CHUNK = 128
SC_K = 3
LRU_K = 4
LRU_C = 8.0
ROPE_BASE = 10000.0
N_GROUPS = 4
EXPERTS_PER_GROUP = 8
N_EXPERTS = N_GROUPS * EXPERTS_PER_GROUP
D_EXPERT = D_MODEL // 4
LN_EPS = 1e-5
DEPTH = 2
ALPHA = (2.0 * DEPTH) ** 0.25

COL_SC_B, COL_SC_C, COL_SC_X, COL_LRU, COL_SG_U, COL_SG_V, COL_Q, COL_K, COL_V, COL_G = (
    i * WIDTH for i in range(10))
COL_GATES = 10 * WIDTH
N_IN = COL_GATES + 4 * D_MODEL

TS = 512
SCAN_PAD = TS // 2
CONV_PAD = SUBLANES
TB_ROUTE = 512
TB_DISPATCH = 512
DISPATCH_UNROLL = 8
TM = 128
PAIRS_PER_GROUP = EXPERTS_PER_GROUP * (EXPERTS_PER_GROUP - 1) // 2
N_CLASSES = N_GROUPS * PAIRS_PER_GROUP
assert N_CLASSES <= LANES


def _sigmoid(x):
    return 0.5 * jnp.tanh(0.5 * x) + 0.5


def _dot(a, b):
    return jnp.dot(a, b, preferred_element_type=F32)


def _load_token_major(ref, rows, lead=()):
    return jnp.concatenate(
        [ref[lead + (pl.ds(c, rows, stride=ROW_TILES), slice(None))] for c in range(ROW_TILES)], axis=1)


def _store_token_major(ref, y, rows, lead=()):
    for c in range(ROW_TILES):
        ref[lead + (pl.ds(c, rows, stride=ROW_TILES), slice(None))] = y[:, c * LANES:(c + 1) * LANES]


def _layer_norm_rows(y, g, b):
    mu = jnp.mean(y, axis=-1, keepdims=True)
    yc = y - mu
    var = jnp.mean(yc * yc, axis=-1, keepdims=True)
    return yc * lax.rsqrt(var + LN_EPS) * g + b


def _rope_table_kernel(pos_ref, freq_ref, cos_ref, sin_ref):
    ang = pos_ref[...].astype(F32) * freq_ref[...]
    lane = lax.broadcasted_iota(jnp.int32, ang.shape, 1)
    first_half = (lane % HEAD_DIM) < (HEAD_DIM // 2)
    cos_ref[...] = jnp.cos(ang)
    s = jnp.sin(ang)
    sin_ref[...] = jnp.where(first_half, -s, s)


def _rope_tables(positions):
    t = positions.size
    half = HEAD_DIM // 2
    inv_freq = ROPE_BASE ** (-jnp.arange(half, dtype=F32) / half)
    freq_row = jnp.tile(inv_freq, LANES // half)[None, :]
    rows = 1024
    return pl.pallas_call(
        _rope_table_kernel,
        out_shape=(jax.ShapeDtypeStruct((t, LANES), F32), jax.ShapeDtypeStruct((t, LANES), F32)),
        grid=(t // rows,),
        in_specs=[pl.BlockSpec((rows, 1), lambda i: (i, 0)),
                  pl.BlockSpec((1, LANES), lambda i: (0, 0))],
        out_specs=(pl.BlockSpec((rows, LANES), lambda i: (i, 0)),
                   pl.BlockSpec((rows, LANES), lambda i: (i, 0))),
        name="rope_tables",
    )(positions.reshape(t, 1), freq_row)


def _mixer_kernel(h_ref, cos_ref, sin_ref, w_in_ref, scw_ref, scb_ref, lcw_ref, lcb_ref,
                  wri_ref, bri_ref, lam_ref, sgg_ref, ws_ref, bs_ref, rg_ref,
                  dec_ref, qd_ref, kd_ref, cd_ref, avg_ref, bp_ref, wo_ref, lng_ref, lnb_ref,
                  out_ref, ubuf, xbuf, hcar, sbd, abuf, hbuf, *, token_major_in):
    step = pl.program_id(1)

    @pl.when(step == 0)
    def _():
        ubuf[0:CONV_PAD, :] = jnp.zeros((CONV_PAD, WIDTH), F32)
        xbuf[0:CONV_PAD, :] = jnp.zeros((CONV_PAD, WIDTH), F32)
        hcar[...] = jnp.zeros_like(hcar)
        sbd[...] = jnp.zeros_like(sbd)
        abuf[0:SCAN_PAD, :] = jnp.ones((SCAN_PAD, WIDTH), F32)
        hbuf[0:SCAN_PAD, :] = jnp.zeros((SCAN_PAD, WIDTH), F32)

    h = _load_token_major(h_ref, TS) if token_major_in else h_ref[...]
    hb = h.astype(BF16)

    def proj(col, width=WIDTH):
        return _dot(hb, w_in_ref[:, col:col + width])

    lane = lax.broadcasted_iota(jnp.int32, (CHUNK, WIDTH), 1)
    head_of_lane = lane // HEAD_DIM
    avg = avg_ref[...]

    def group_standardize(x):
        mean = _dot(x.astype(BF16), avg)
        xc = x - mean
        var = _dot((xc * xc).astype(BF16), avg)
        return xc * lax.rsqrt(var + LN_EPS)

    sc_b, sc_c, sc_x = proj(COL_SC_B), proj(COL_SC_C), proj(COL_SC_X)
    u = sc_c * sc_x
    ubuf[CONV_PAD:CONV_PAD + TS, :] = u
    conv = scw_ref[2:3, :] * u + scb_ref[...]
    for j in range(SC_K - 1):
        back = SC_K - 1 - j
        conv = conv + scw_ref[j:j + 1, :] * ubuf[CONV_PAD - back:CONV_PAD - back + TS, :]
    ubuf[0:CONV_PAD, :] = ubuf[TS:TS + CONV_PAD, :]
    branch_a = sc_b * conv

    lx = proj(COL_LRU)
    xbuf[CONV_PAD:CONV_PAD + TS, :] = lx
    xc = lcw_ref[LRU_K - 1:LRU_K, :] * lx + lcb_ref[...]
    for j in range(LRU_K - 1):
        back = LRU_K - 1 - j
        xc = xc + lcw_ref[j:j + 1, :] * xbuf[CONV_PAD - back:CONV_PAD - back + TS, :]
    xbuf[0:CONV_PAD, :] = xbuf[TS:TS + CONV_PAD, :]
    ri = _dot(xc.astype(BF16), wri_ref[...]) + bri_ref[...]
    r = _sigmoid(ri[:, :WIDTH])
    ig = _sigmoid(ri[:, WIDTH:])
    neg_lam = -lam_ref[...]
    softplus = jnp.maximum(neg_lam, 0.0) + jnp.log1p(jnp.exp(-jnp.abs(neg_lam)))
    log_a = (-LRU_C) * r * softplus
    a = jnp.exp(log_a)
    th = jnp.tanh(log_a)
    uu = jnp.sqrt((-2.0) * th / (1.0 - th)) * (ig * xc)
    row = lax.broadcasted_iota(jnp.int32, (TS, WIDTH), 0)
    uu = uu + jnp.where(row == 0, a * hcar[...], 0.0)
    abuf[SCAN_PAD:SCAN_PAD + TS, :] = a
    hbuf[SCAN_PAD:SCAN_PAD + TS, :] = uu
    shift = 1
    while shift < TS:
        a_cur = abuf[SCAN_PAD:SCAN_PAD + TS, :]
        h_prev = hbuf[SCAN_PAD - shift:SCAN_PAD - shift + TS, :]
        h_new = a_cur * h_prev + hbuf[SCAN_PAD:SCAN_PAD + TS, :]
        if shift * 2 < TS:
            a_new = a_cur * abuf[SCAN_PAD - shift:SCAN_PAD - shift + TS, :]
            abuf[SCAN_PAD:SCAN_PAD + TS, :] = a_new
        hbuf[SCAN_PAD:SCAN_PAD + TS, :] = h_new
        shift *= 2
    branch_b = hbuf[SCAN_PAD:SCAN_PAD + TS, :]
    hcar[...] = hbuf[SCAN_PAD + TS - 1:SCAN_PAD + TS, :]

    gu = jax.nn.gelu(proj(COL_SG_U))
    gv = jax.nn.gelu(proj(COL_SG_V))
    vn = (group_standardize(gv) * sgg_ref[...]).astype(BF16)
    trow = lax.broadcasted_iota(jnp.int32, (CHUNK, CHUNK), 0)
    tcol = lax.broadcasted_iota(jnp.int32, (CHUNK, CHUNK), 1)
    w_causal = [jnp.where(trow >= tcol, ws_ref[g], 0.0).astype(BF16) for g in range(N_HEADS)]
    sv_chunks = []
    for c in range(TS // CHUNK):
        vch = vn[c * CHUNK:(c + 1) * CHUNK, :]
        sv = bs_ref[...]
        for g in range(N_HEADS):
            sv = sv + jnp.where(head_of_lane == g, _dot(w_causal[g], vch), 0.0)
        sv_chunks.append(sv)
    branch_c = gu * jnp.concatenate(sv_chunks, axis=0)

    cos = jnp.concatenate([cos_ref[...]] * (WIDTH // LANES), axis=1)
    sin = jnp.concatenate([sin_ref[...]] * (WIDTH // LANES), axis=1)
    lane_ts = lax.broadcasted_iota(jnp.int32, (TS, WIDTH), 1)
    first_half = (lane_ts % HEAD_DIM) < (HEAD_DIM // 2)

    def rope(t):
        swapped = jnp.where(first_half,
                            pltpu.roll(t, WIDTH - HEAD_DIM // 2, 1),
                            pltpu.roll(t, HEAD_DIM // 2, 1))
        return t * cos + swapped * sin

    q = rope(proj(COL_Q))
    k = rope(proj(COL_K)) * (HEAD_DIM ** -0.5)
    v = proj(COL_V)
    zg = proj(COL_G)
    bd_mask = (lax.broadcasted_iota(jnp.int32, (WIDTH, WIDTH), 0) // HEAD_DIM
               == lax.broadcasted_iota(jnp.int32, (WIDTH, WIDTH), 1) // HEAD_DIM)
    o_chunks = []
    for c in range(TS // CHUNK):
        sl = slice(c * CHUNK, (c + 1) * CHUNK)
        qc, kc, vc = q[sl, :], k[sl, :], v[sl, :]
        qcb = qc.astype(BF16)
        kstack = jnp.concatenate(
            [jnp.where(head_of_lane == hh, kc, 0.0) for hh in range(N_HEADS)], axis=0).astype(BF16)
        vstack = jnp.concatenate(
            [jnp.where(head_of_lane == hh, vc, 0.0) for hh in range(N_HEADS)], axis=0).astype(BF16)
        scores = lax.dot_general(qcb, kstack, (((1,), (1,)), ((), ())),
                                 preferred_element_type=F32)
        inner = _dot((scores * dec_ref[...]).astype(BF16), vstack)
        state = sbd[...]
        cross = _dot(qcb, state.astype(BF16)) * qd_ref[...]
        o_chunks.append(inner + cross)
        kdec = (kc * kd_ref[...]).astype(BF16)
        kv = lax.dot_general(kdec, vc.astype(BF16), (((0,), (0,)), ((), ())),
                             preferred_element_type=F32)
        sbd[...] = cd_ref[...] * state + jnp.where(bd_mask, kv, 0.0)
    o = jnp.concatenate(o_chunks, axis=0)
    branch_d = (zg * _sigmoid(zg)) * (group_standardize(o) * rg_ref[...])

    merged = jnp.zeros((TS, D_MODEL), F32)
    for b_idx, branch in enumerate((branch_a, branch_b, branch_c, branch_d)):
        gate = _sigmoid(proj(COL_GATES + b_idx * D_MODEL, D_MODEL))
        merged = merged + gate * _dot(branch.astype(BF16), bp_ref[b_idx])
    mix = _dot(merged.astype(BF16), wo_ref[...])
    _store_token_major(out_ref, _layer_norm_rows(ALPHA * h + mix, lng_ref[...], lnb_ref[...]), TS)


def _const_spec(shape):
    nd = len(shape)
    return pl.BlockSpec(shape, lambda b, i, _nd=nd: (0,) * _nd, pipeline_mode=pl.Buffered(1))


def _mixer_layer(h, cos_t, sin_t, consts, batch, seq, token_major_in):
    t = batch * seq
    steps = seq // TS
    row_map = lambda b, i: (b * steps + i, 0)
    tm_spec = pl.BlockSpec((TS * ROW_TILES, LANES), row_map)
    in_specs = [tm_spec if token_major_in else pl.BlockSpec((TS, D_MODEL), row_map),
                pl.BlockSpec((TS, LANES), row_map),
                pl.BlockSpec((TS, LANES), row_map)]
    in_specs += [_const_spec(c.shape) for c in consts]
    return pl.pallas_call(
        functools.partial(_mixer_kernel, token_major_in=token_major_in),
        out_shape=jax.ShapeDtypeStruct((t * ROW_TILES, LANES), F32),
        grid=(batch, steps),
        in_specs=in_specs,
        out_specs=tm_spec,
        scratch_shapes=[
            pltpu.VMEM((CONV_PAD + TS, WIDTH), F32),
            pltpu.VMEM((CONV_PAD + TS, WIDTH), F32),
            pltpu.VMEM((1, WIDTH), F32),
            pltpu.VMEM((WIDTH, WIDTH), F32),
            pltpu.VMEM((SCAN_PAD + TS, WIDTH), F32),
            pltpu.VMEM((SCAN_PAD + TS, WIDTH), F32),
        ],
        compiler_params=pltpu.CompilerParams(
            dimension_semantics=("arbitrary", "arbitrary"),
            vmem_limit_bytes=VMEM_LIMIT_BYTES),
        name="mixer_layer",
    )(h, cos_t, sin_t, *consts)


def _block_diag(w):
    heads, d, e = w.shape
    eye = jnp.eye(heads, dtype=w.dtype)
    return (eye[:, None, :, None] * w[:, :, None, :]).reshape(heads * d, heads * e)


def _retention_tables():
    log_gamma = jnp.log1p(-jnp.exp2(-5.0 - jnp.arange(N_HEADS, dtype=F32)))
    pos = jnp.arange(CHUNK, dtype=F32)
    diff = pos[:, None] - pos[None, :]
    decay = jnp.where(diff >= 0, jnp.exp(jnp.maximum(diff, 0.0) * log_gamma[:, None, None]), 0.0)
    dec_all = jnp.transpose(decay, (1, 0, 2)).reshape(CHUNK, N_HEADS * CHUNK)
    q_decay = jnp.exp((pos + 1.0)[:, None] * log_gamma)
    k_decay = jnp.exp((CHUNK - 1.0 - pos)[:, None] * log_gamma)
    qd_tab = jnp.repeat(q_decay, HEAD_DIM, axis=1)
    kd_tab = jnp.repeat(k_decay, HEAD_DIM, axis=1)
    chunk_decay = jnp.repeat(jnp.exp(CHUNK * log_gamma), HEAD_DIM)
    cd_tab = jnp.broadcast_to(chunk_decay[:, None], (WIDTH, WIDTH))
    return dec_all, qd_tab, kd_tab, cd_tab


def _router_kernel(h_ref, whi_ref, wlo_ref, b_ref, meta_ref, cnt_ref, carry):
    step = pl.program_id(0)

    @pl.when(step == 0)
    def _():
        carry[...] = jnp.zeros_like(carry)

    h = _load_token_major(h_ref, TB_ROUTE)
    h_hi = h.astype(BF16)
    h_lo = (h - h_hi.astype(F32)).astype(BF16)
    whi = whi_ref[...]
    logits = _dot(h_hi, whi) + _dot(h_lo, whi) + _dot(h_hi, wlo_ref[...]) + b_ref[...]
    lane = lax.broadcasted_iota(jnp.int32, logits.shape, 1)
    big = jnp.int32(2 ** 30)
    neg_inf = F32(-jnp.inf)

    gl = jnp.where(lane < N_GROUPS, logits, neg_inf)
    gmax = jnp.max(gl, axis=-1, keepdims=True)
    g_idx = jnp.min(jnp.where(gl == gmax, lane, big), axis=-1, keepdims=True)

    e_lane = lane - N_GROUPS
    in_group = (e_lane >= g_idx * EXPERTS_PER_GROUP) & (e_lane < (g_idx + 1) * EXPERTS_PER_GROUP)
    el = jnp.where(in_group, logits, neg_inf)
    m1 = jnp.max(el, axis=-1, keepdims=True)
    i1 = jnp.min(jnp.where(el == m1, lane, big), axis=-1, keepdims=True)
    el2 = jnp.where(lane == i1, neg_inf, el)
    m2 = jnp.max(el2, axis=-1, keepdims=True)
    i2 = jnp.min(jnp.where(el2 == m2, lane, big), axis=-1, keepdims=True)
    base = (g_idx * EXPERTS_PER_GROUP + N_GROUPS).astype(F32)
    lo = jnp.minimum(i1, i2).astype(F32) - base
    hi = jnp.maximum(i1, i2).astype(F32) - base
    pair = lo * (2.0 * EXPERTS_PER_GROUP - 1.0 - lo) * 0.5 + (hi - lo - 1.0)
    cls = (g_idx.astype(F32) * PAIRS_PER_GROUP + pair).astype(jnp.int32)

    onehot = jnp.where(lane == cls, 1.0, 0.0)
    tb = h.shape[0]
    r_i = lax.broadcasted_iota(jnp.int32, (tb, tb), 0)
    c_i = lax.broadcasted_iota(jnp.int32, (tb, tb), 1)
    strict_lower = jnp.where(r_i > c_i, 1.0, 0.0).astype(BF16)
    before = _dot(strict_lower, onehot.astype(BF16)) + carry[...]
    rank = jnp.sum(jnp.where(lane == cls, before, 0.0), axis=-1, keepdims=True)
    carry[...] = carry[...] + jnp.sum(onehot, axis=0, keepdims=True)

    meta = jnp.where(lane == 0, cls.astype(F32), 0.0) + jnp.where(lane == 1, rank, 0.0)
    meta_ref[...] = jnp.transpose(meta)[0:SUBLANES, :].astype(jnp.int32)
    cnt_ref[...] = jnp.broadcast_to(carry[...], cnt_ref.shape).astype(jnp.int32)


def _router(h, t, whi, wlo, bias):
    return pl.pallas_call(
        _router_kernel,
        out_shape=(jax.ShapeDtypeStruct((SUBLANES, t), jnp.int32),
                   jax.ShapeDtypeStruct((SUBLANES, LANES), jnp.int32)),
        grid=(t // TB_ROUTE,),
        in_specs=[pl.BlockSpec((TB_ROUTE * ROW_TILES, LANES), lambda i: (i, 0)),
                  pl.BlockSpec((D_MODEL, LANES), lambda i: (0, 0)),
                  pl.BlockSpec((D_MODEL, LANES), lambda i: (0, 0)),
                  pl.BlockSpec((1, LANES), lambda i: (0, 0))],
        out_specs=(pl.BlockSpec((SUBLANES, TB_ROUTE), lambda i: (0, i)),
                   pl.BlockSpec((SUBLANES, LANES), lambda i: (0, 0))),
        scratch_shapes=[pltpu.VMEM((1, LANES), F32)],
        compiler_params=pltpu.CompilerParams(dimension_semantics=("arbitrary",)),
        name="router",
    )(h, whi, wlo, bias)


def _token_rows(tok):
    if isinstance(tok, int):
        return pl.ds(tok * ROW_TILES, ROW_TILES)
    return pl.ds(pl.multiple_of(tok * ROW_TILES, ROW_TILES), ROW_TILES)


def _dispatch_kernel(offs_ref, cls_ref, rank_ref, h_ref, dst0_hbm, xs_in_hbm, xs_hbm, dst_hbm,
                     dst_ref, row_sem, map_sem):
    del xs_in_hbm
    step = pl.program_id(0)

    @pl.when(step == 0)
    def _():
        load = pltpu.make_async_copy(dst0_hbm, dst_ref, map_sem)
        load.start()
        load.wait()

    def send(g, _):
        for k in range(DISPATCH_UNROLL):
            r = g * DISPATCH_UNROLL + k
            p = offs_ref[cls_ref[r]] + rank_ref[r]
            dst_ref[p] = step * TB_DISPATCH + r
            pltpu.make_async_copy(h_ref.at[_token_rows(r), :], xs_hbm.at[_token_rows(p), :],
                                  row_sem).start(priority=k % 2)
        return 0

    lax.fori_loop(0, TB_DISPATCH // DISPATCH_UNROLL, send, 0)

    pltpu.make_async_copy(h_ref, xs_hbm.at[pl.ds(0, TB_DISPATCH * ROW_TILES), :], row_sem).wait()

    @pl.when(step == pl.num_programs(0) - 1)
    def _():
        store = pltpu.make_async_copy(dst_ref, dst_hbm, map_sem)
        store.start()
        store.wait()


def _dispatch(h, offs, cls, rank, n_rows):
    t = cls.shape[0]
    dst0 = jnp.zeros((n_rows,), jnp.int32)
    xs0 = jnp.zeros((n_rows * ROW_TILES, LANES), F32)
    smem_blk = pl.BlockSpec((TB_DISPATCH,), lambda i, offs: (i,), memory_space=pltpu.SMEM)
    hbm = pl.BlockSpec(memory_space=pl.ANY)
    return pl.pallas_call(
        _dispatch_kernel,
        out_shape=(jax.ShapeDtypeStruct((n_rows * ROW_TILES, LANES), F32),
                   jax.ShapeDtypeStruct((n_rows,), jnp.int32)),
        grid_spec=pltpu.PrefetchScalarGridSpec(
            num_scalar_prefetch=1,
            grid=(t // TB_DISPATCH,),
            in_specs=[smem_blk, smem_blk,
                      pl.BlockSpec((TB_DISPATCH * ROW_TILES, LANES), lambda i, offs: (i, 0)),
                      hbm, hbm],
            out_specs=(hbm, hbm),
            scratch_shapes=[pltpu.SMEM((n_rows,), jnp.int32),
                            pltpu.SemaphoreType.DMA,
                            pltpu.SemaphoreType.DMA]),
        input_output_aliases={5: 0},
        compiler_params=pltpu.CompilerParams(dimension_semantics=("arbitrary",)),
        name="moe_dispatch",
    )(offs, cls, rank, h, dst0, xs0)


def _expert_kernel(ea_ref, eb_ref, nvalid_ref, nused_ref, dst_ref,
                   x_ref, wr_ref, br_ref, wgua_ref, wda_ref, wgub_ref, wdb_ref, lng_ref, lnb_ref,
                   out_hbm, ybuf, ssem):
    j = pl.program_id(0)
    n_used = nused_ref[0]
    slot = j % 2
    other = 1 - slot

    def start_rows(n, s):
        for r in range(TM):
            @pl.when(r < n)
            def _(r=r):
                pltpu.make_async_copy(ybuf.at[s, _token_rows(r), :],
                                      out_hbm.at[_token_rows(dst_ref[r]), :], ssem.at[s]).start(priority=r % 2)

    def wait_rows(n, s):
        @pl.when(n > 0)
        def _():
            rows = pl.ds(0, n * ROW_TILES)
            pltpu.make_async_copy(ybuf.at[s, rows, :], out_hbm.at[rows, :], ssem.at[s]).wait()

    @pl.when(j < n_used)
    def _():
        xb = _load_token_major(x_ref, TM).astype(BF16)

        logits = _dot(xb, wr_ref[...]) + br_ref[...]
        lane = lax.broadcasted_iota(jnp.int32, logits.shape, 1)
        ea, eb = ea_ref[j], eb_ref[j]
        group = ea // EXPERTS_PER_GROUP
        gl = jnp.where(lane < N_GROUPS, logits, F32(-jnp.inf))
        gmax = jnp.max(gl, axis=-1, keepdims=True)
        gsum = jnp.sum(jnp.where(lane < N_GROUPS, jnp.exp(gl - gmax), 0.0), axis=-1, keepdims=True)
        lg = jnp.sum(jnp.where(lane == group, logits, 0.0), axis=-1, keepdims=True)
        g_top_p = jnp.exp(lg - gmax) / gsum
        la = jnp.sum(jnp.where(lane == ea + N_GROUPS, logits, 0.0), axis=-1, keepdims=True)
        lb = jnp.sum(jnp.where(lane == eb + N_GROUPS, logits, 0.0), axis=-1, keepdims=True)
        w_a = g_top_p / (1.0 + jnp.exp(lb - la))
        w_b = g_top_p / (1.0 + jnp.exp(la - lb))

        def hidden(wgu_ref, weight):
            gu = _dot(xb, wgu_ref[0])
            gate, up = gu[:, :D_EXPERT], gu[:, D_EXPERT:]
            return ((gate * _sigmoid(gate)) * up * weight).astype(BF16)

        ffn = _dot(hidden(wgua_ref, w_a), wda_ref[0]) + _dot(hidden(wgub_ref, w_b), wdb_ref[0])
        res = _layer_norm_rows(ALPHA * _load_token_major(x_ref, TM) + ffn, lng_ref[...], lnb_ref[...])

        @pl.when(j >= 2)
        def _():
            wait_rows(nvalid_ref[jnp.maximum(j - 2, 0)], slot)

        _store_token_major(ybuf, res, TM, lead=(slot,))
        start_rows(nvalid_ref[j], slot)

        @pl.when(j == n_used - 1)
        def _():
            @pl.when(j >= 1)
            def _():
                wait_rows(nvalid_ref[jnp.maximum(j - 1, 0)], other)

            wait_rows(nvalid_ref[j], slot)


def _experts(xs, t, dst, tile_ea, tile_eb, tile_nvalid, n_used, w_router, b_router, wgu, wd, ln_g, ln_b):
    n_tiles = dst.shape[0] // TM
    const = lambda shape: pl.BlockSpec(shape, lambda j, ea, eb, nv, nu: (0,) * len(shape))
    wgu_spec = lambda which: pl.BlockSpec(
        (1, D_MODEL, 2 * D_EXPERT), lambda j, ea, eb, nv, nu, _w=which: ((ea, eb)[_w][j], 0, 0))
    wd_spec = lambda which: pl.BlockSpec(
        (1, D_EXPERT, D_MODEL), lambda j, ea, eb, nv, nu, _w=which: ((ea, eb)[_w][j], 0, 0))
    grid_spec = pltpu.PrefetchScalarGridSpec(
        num_scalar_prefetch=4,
        grid=(n_tiles,),
        in_specs=[pl.BlockSpec((TM,), lambda j, ea, eb, nv, nu: (j,), memory_space=pltpu.SMEM),
                  pl.BlockSpec((TM * ROW_TILES, LANES), lambda j, ea, eb, nv, nu: (j, 0)),
                  const((D_MODEL, LANES)), const((1, LANES)),
                  wgu_spec(0), wd_spec(0), wgu_spec(1), wd_spec(1),
                  const((1, D_MODEL)), const((1, D_MODEL))],
        out_specs=pl.BlockSpec(memory_space=pl.ANY),
        scratch_shapes=[pltpu.VMEM((2, TM * ROW_TILES, LANES), F32),
                        pltpu.SemaphoreType.DMA((2,))],
    )
    return pl.pallas_call(
        _expert_kernel,
        out_shape=jax.ShapeDtypeStruct((t * ROW_TILES, LANES), F32),
        grid_spec=grid_spec,
        compiler_params=pltpu.CompilerParams(dimension_semantics=("arbitrary",)),
        name="moe_experts",
    )(tile_ea, tile_eb, tile_nvalid, n_used, dst, xs, w_router, b_router, wgu, wd, wgu, wd, ln_g, ln_b)


def _class_expert_tables():
    first, second = [], []
    for g in range(N_GROUPS):
        for lo in range(EXPERTS_PER_GROUP):
            for hi in range(lo + 1, EXPERTS_PER_GROUP):
                first.append(g * EXPERTS_PER_GROUP + lo)
                second.append(g * EXPERTS_PER_GROUP + hi)
    return jnp.array(first, jnp.int32), jnp.array(second, jnp.int32)


def _moe_layer(h, t, wg, bg, we, be, w_gate, w_up, w_down, ln_g, ln_b):
    n_tiles = t // TM + N_CLASSES
    n_rows = n_tiles * TM

    w_router = jnp.zeros((D_MODEL, LANES), F32)
    w_router = w_router.at[:, :N_GROUPS].set(wg).at[:, N_GROUPS:N_GROUPS + N_EXPERTS].set(we)
    b_router = jnp.zeros((1, LANES), F32)
    b_router = b_router.at[0, :N_GROUPS].set(bg).at[0, N_GROUPS:N_GROUPS + N_EXPERTS].set(be)
    w_hi = w_router.astype(BF16)
    w_lo = (w_router - w_hi.astype(F32)).astype(BF16)

    meta, counts = _router(h, t, w_hi, w_lo, b_router)

    cnt = counts[0]
    tiles_per = (cnt + TM - 1) // TM
    tile_end = jnp.cumsum(tiles_per)
    offs = (tile_end - tiles_per) * TM
    n_used = tile_end[-1]
    tile_id = jnp.arange(n_tiles, dtype=jnp.int32)
    tile_cls = jnp.sum((jnp.minimum(tile_id, n_used - 1)[:, None] >= tile_end[None, :N_CLASSES])
                       .astype(jnp.int32), axis=1)
    seg_tile = tile_id - (tile_end - tiles_per)[tile_cls]
    tile_nvalid = jnp.where(tile_id < n_used, jnp.clip(cnt[tile_cls] - seg_tile * TM, 0, TM), 0)
    first, second = _class_expert_tables()
    xs, dst = _dispatch(h, offs, meta[0], meta[1], n_rows)
    wgu = jnp.concatenate([w_gate, w_up], axis=-1).astype(BF16)
    return _experts(xs, t, dst, first[tile_cls], second[tile_cls], tile_nvalid, n_used.reshape(1),
                    w_hi, b_router, wgu, w_down.astype(BF16), ln_g, ln_b)


def kernel(x, positions, w_in, sc_conv_w, sc_conv_b, lru_conv_w, lru_conv_b, lru_w_r, lru_b_r,
           lru_w_i, lru_b_i, lru_lambda, sg_norm_g, sg_w_s, sg_b_s, ret_norm_g, branch_proj, w_out,
           ln_mix_g, ln_mix_b, router_group_w, router_group_b, router_expert_w, router_expert_b,
           exp_w_gate, exp_w_up, exp_w_down, ln_ffn_g, ln_ffn_b):
    batch, seq, d = x.shape
    assert d == D_MODEL and seq % TS == 0 and w_in.shape[-1] == N_IN
    depth = w_in.shape[0]
    t = batch * seq
    cos_t, sin_t = _rope_tables(positions)
    dec_all, qd_tab, kd_tab, cd_tab = _retention_tables()
    avg = _block_diag(jnp.full((N_HEADS, HEAD_DIM, HEAD_DIM), 1.0 / HEAD_DIM, F32)).astype(BF16)

    h = x.reshape(t, d)
    for l in range(depth):
        w_ri = jnp.concatenate([_block_diag(lru_w_r[l]), _block_diag(lru_w_i[l])], axis=1).astype(BF16)
        b_ri = jnp.concatenate([lru_b_r[l], lru_b_i[l]])[None, :]
        bs_tab = jnp.repeat(sg_b_s[l].T, HEAD_DIM, axis=1)
        consts = (
            w_in[l].astype(BF16), sc_conv_w[l], sc_conv_b[l][None, :], lru_conv_w[l], lru_conv_b[l][None, :],
            w_ri, b_ri, lru_lambda[l][None, :], sg_norm_g[l][None, :], sg_w_s[l], bs_tab,
            ret_norm_g[l][None, :], dec_all, qd_tab, kd_tab, cd_tab, avg,
            branch_proj[l].astype(BF16), w_out[l].astype(BF16), ln_mix_g[l][None, :], ln_mix_b[l][None, :],
        )
        h = _mixer_layer(h, cos_t, sin_t, consts, batch, seq, token_major_in=l > 0)
        h = _moe_layer(h, t, router_group_w[l], router_group_b[l], router_expert_w[l], router_expert_b[l],
                       exp_w_gate[l], exp_w_up[l], exp_w_down[l], ln_ffn_g[l][None, :], ln_ffn_b[l][None, :])
    return h.reshape(batch, seq, d)
```

```python
import functools

import jax
import jax.numpy as jnp
from jax import lax
from jax.experimental import pallas as pl
from jax.experimental.pallas import tpu as pltpu

F32 = jnp.float32
BF16 = jnp.bfloat16

LANES = 128
SUBLANES = 8
VMEM_LIMIT_BYTES = 56 * 1024 * 1024

D_MODEL = 1024
ROW_TILES = D_MODEL // LANES
assert ROW_TILES == SUBLANES
WIDTH = D_MODEL // 4
N_HEADS = 4
HEAD_DIM = WIDTH // N_HEADS
CHUNK = 128
SC_K = 3
LRU_K = 4
LRU_C = 8.0
ROPE_BASE = 10000.0
N_GROUPS = 4
EXPERTS_PER_GROUP = 8
N_EXPERTS = N_GROUPS * EXPERTS_PER_GROUP
D_EXPERT = D_MODEL // 4
LN_EPS = 1e-5
DEPTH = 2
ALPHA = (2.0 * DEPTH) ** 0.25

COL_SC_B, COL_SC_C, COL_SC_X, COL_LRU, COL_SG_U, COL_SG_V, COL_Q, COL_K, COL_V, COL_G = (
    i * WIDTH for i in range(10))
COL_GATES = 10 * WIDTH
N_IN = COL_GATES + 4 * D_MODEL

TS = 512
SCAN_PAD = TS // 2
CONV_PAD = SUBLANES
TB_ROUTE = 512
TB_DISPATCH = 512
DISPATCH_UNROLL = 8
TM = 128
TILES_PER_STEP = 2
PAIRS_PER_GROUP = EXPERTS_PER_GROUP * (EXPERTS_PER_GROUP - 1) // 2
N_CLASSES = N_GROUPS * PAIRS_PER_GROUP
assert N_CLASSES <= LANES


def _sigmoid(x):
    return 0.5 * jnp.tanh(0.5 * x) + 0.5


def _dot(a, b):
    return jnp.dot(a, b, preferred_element_type=F32)


def _load_token_major(ref, rows, lead=()):
    return jnp.concatenate(
        [ref[lead + (pl.ds(c, rows, stride=ROW_TILES), slice(None))] for c in range(ROW_TILES)], axis=1)


def _store_token_major(ref, y, rows, lead=()):
    for c in range(ROW_TILES):
        ref[lead + (pl.ds(c, rows, stride=ROW_TILES), slice(None))] = y[:, c * LANES:(c + 1) * LANES]


def _layer_norm_rows(y, g, b):
    mu = jnp.mean(y, axis=-1, keepdims=True)
    yc = y - mu
    var = jnp.mean(yc * yc, axis=-1, keepdims=True)
    return yc * lax.rsqrt(var + LN_EPS) * g + b


def _rope_table_kernel(pos_ref, freq_ref, cos_ref, sin_ref):
    ang = pos_ref[...].astype(F32) * freq_ref[...]
    lane = lax.broadcasted_iota(jnp.int32, ang.shape, 1)
    first_half = (lane % HEAD_DIM) < (HEAD_DIM // 2)
    cos_ref[...] = jnp.cos(ang)
    s = jnp.sin(ang)
    sin_ref[...] = jnp.where(first_half, -s, s)


def _rope_tables(positions):
    t = positions.size
    half = HEAD_DIM // 2
    inv_freq = ROPE_BASE ** (-jnp.arange(half, dtype=F32) / half)
    freq_row = jnp.tile(inv_freq, LANES // half)[None, :]
    rows = 1024
    return pl.pallas_call(
        _rope_table_kernel,
        out_shape=(jax.ShapeDtypeStruct((t, LANES), F32), jax.ShapeDtypeStruct((t, LANES), F32)),
        grid=(t // rows,),
        in_specs=[pl.BlockSpec((rows, 1), lambda i: (i, 0)),
                  pl.BlockSpec((1, LANES), lambda i: (0, 0))],
        out_specs=(pl.BlockSpec((rows, LANES), lambda i: (i, 0)),
                   pl.BlockSpec((rows, LANES), lambda i: (i, 0))),
        name="rope_tables",
    )(positions.reshape(t, 1), freq_row)


def _mixer_kernel(h_ref, cos_ref, sin_ref, w_in_ref, scw_ref, scb_ref, lcw_ref, lcb_ref,
                  wri_ref, bri_ref, lam_ref, sgg_ref, ws_ref, bs_ref, rg_ref,
                  dec_ref, qd_ref, kd_ref, cd_ref, avg_ref, bp_ref, wo_ref, lng_ref, lnb_ref,
                  out_ref, ubuf, xbuf, hcar, sbd, abuf, hbuf, *, token_major_in):
    step = pl.program_id(1)

    @pl.when(step == 0)
    def _():
        ubuf[0:CONV_PAD, :] = jnp.zeros((CONV_PAD, WIDTH), F32)
        xbuf[0:CONV_PAD, :] = jnp.zeros((CONV_PAD, WIDTH), F32)
        hcar[...] = jnp.zeros_like(hcar)
        sbd[...] = jnp.zeros_like(sbd)
        abuf[0:SCAN_PAD, :] = jnp.ones((SCAN_PAD, WIDTH), F32)
        hbuf[0:SCAN_PAD, :] = jnp.zeros((SCAN_PAD, WIDTH), F32)

    h = _load_token_major(h_ref, TS) if token_major_in else h_ref[...]
    hb = h.astype(BF16)

    def proj(col, width=WIDTH):
        return _dot(hb, w_in_ref[:, col:col + width])

    lane = lax.broadcasted_iota(jnp.int32, (CHUNK, WIDTH), 1)
    head_of_lane = lane // HEAD_DIM
    avg = avg_ref[...]

    def group_standardize(x):
        mean = _dot(x.astype(BF16), avg)
        xc = x - mean
        var = _dot((xc * xc).astype(BF16), avg)
        return xc * lax.rsqrt(var + LN_EPS)

    sc_b, sc_c, sc_x = proj(COL_SC_B), proj(COL_SC_C), proj(COL_SC_X)
    u = sc_c * sc_x
    ubuf[CONV_PAD:CONV_PAD + TS, :] = u
    conv = scw_ref[2:3, :] * u + scb_ref[...]
    for j in range(SC_K - 1):
        back = SC_K - 1 - j
        conv = conv + scw_ref[j:j + 1, :] * ubuf[CONV_PAD - back:CONV_PAD - back + TS, :]
    ubuf[0:CONV_PAD, :] = ubuf[TS:TS + CONV_PAD, :]
    branch_a = sc_b * conv

    lx = proj(COL_LRU)
    xbuf[CONV_PAD:CONV_PAD + TS, :] = lx
    xc = lcw_ref[LRU_K - 1:LRU_K, :] * lx + lcb_ref[...]
    for j in range(LRU_K - 1):
        back = LRU_K - 1 - j
        xc = xc + lcw_ref[j:j + 1, :] * xbuf[CONV_PAD - back:CONV_PAD - back + TS, :]
    xbuf[0:CONV_PAD, :] = xbuf[TS:TS + CONV_PAD, :]
    ri = _dot(xc.astype(BF16), wri_ref[...]) + bri_ref[...]
    r = _sigmoid(ri[:, :WIDTH])
    ig = _sigmoid(ri[:, WIDTH:])
    neg_lam = -lam_ref[...]
    softplus = jnp.maximum(neg_lam, 0.0) + jnp.log1p(jnp.exp(-jnp.abs(neg_lam)))
    log_a = (-LRU_C) * r * softplus
    a = jnp.exp(log_a)
    th = jnp.tanh(log_a)
    uu = jnp.sqrt((-2.0) * th / (1.0 - th)) * (ig * xc)
    row = lax.broadcasted_iota(jnp.int32, (TS, WIDTH), 0)
    uu = uu + jnp.where(row == 0, a * hcar[...], 0.0)
    abuf[SCAN_PAD:SCAN_PAD + TS, :] = a
    hbuf[SCAN_PAD:SCAN_PAD + TS, :] = uu
    shift = 1
    while shift < TS:
        a_cur = abuf[SCAN_PAD:SCAN_PAD + TS, :]
        h_prev = hbuf[SCAN_PAD - shift:SCAN_PAD - shift + TS, :]
        h_new = a_cur * h_prev + hbuf[SCAN_PAD:SCAN_PAD + TS, :]
        if shift * 2 < TS:
            a_new = a_cur * abuf[SCAN_PAD - shift:SCAN_PAD - shift + TS, :]
            abuf[SCAN_PAD:SCAN_PAD + TS, :] = a_new
        hbuf[SCAN_PAD:SCAN_PAD + TS, :] = h_new
        shift *= 2
    branch_b = hbuf[SCAN_PAD:SCAN_PAD + TS, :]
    hcar[...] = hbuf[SCAN_PAD + TS - 1:SCAN_PAD + TS, :]

    gu = jax.nn.gelu(proj(COL_SG_U))
    gv = jax.nn.gelu(proj(COL_SG_V))
    vn = (group_standardize(gv) * sgg_ref[...]).astype(BF16)
    trow = lax.broadcasted_iota(jnp.int32, (CHUNK, CHUNK), 0)
    tcol = lax.broadcasted_iota(jnp.int32, (CHUNK, CHUNK), 1)
    w_causal = [jnp.where(trow >= tcol, ws_ref[g], 0.0).astype(BF16) for g in range(N_HEADS)]
    sv_chunks = []
    for c in range(TS // CHUNK):
        vch = vn[c * CHUNK:(c + 1) * CHUNK, :]
        sv = bs_ref[...]
        for g in range(N_HEADS):
            sv = sv + jnp.where(head_of_lane == g, _dot(w_causal[g], vch), 0.0)
        sv_chunks.append(sv)
    branch_c = gu * jnp.concatenate(sv_chunks, axis=0)

    cos = jnp.concatenate([cos_ref[...]] * (WIDTH // LANES), axis=1)
    sin = jnp.concatenate([sin_ref[...]] * (WIDTH // LANES), axis=1)
    lane_ts = lax.broadcasted_iota(jnp.int32, (TS, WIDTH), 1)
    first_half = (lane_ts % HEAD_DIM) < (HEAD_DIM // 2)

    def rope(t):
        swapped = jnp.where(first_half,
                            pltpu.roll(t, WIDTH - HEAD_DIM // 2, 1),
                            pltpu.roll(t, HEAD_DIM // 2, 1))
        return t * cos + swapped * sin

    q = rope(proj(COL_Q))
    k = rope(proj(COL_K)) * (HEAD_DIM ** -0.5)
    v = proj(COL_V)
    zg = proj(COL_G)
    bd_mask = (lax.broadcasted_iota(jnp.int32, (WIDTH, WIDTH), 0) // HEAD_DIM
               == lax.broadcasted_iota(jnp.int32, (WIDTH, WIDTH), 1) // HEAD_DIM)
    o_chunks = []
    for c in range(TS // CHUNK):
        sl = slice(c * CHUNK, (c + 1) * CHUNK)
        qc, kc, vc = q[sl, :], k[sl, :], v[sl, :]
        qcb = qc.astype(BF16)
        kstack = jnp.concatenate(
            [jnp.where(head_of_lane == hh, kc, 0.0) for hh in range(N_HEADS)], axis=0).astype(BF16)
        vstack = jnp.concatenate(
            [jnp.where(head_of_lane == hh, vc, 0.0) for hh in range(N_HEADS)], axis=0).astype(BF16)
        scores = lax.dot_general(qcb, kstack, (((1,), (1,)), ((), ())),
                                 preferred_element_type=F32)
        inner = _dot((scores * dec_ref[...]).astype(BF16), vstack)
        state = sbd[...]
        cross = _dot(qcb, state.astype(BF16)) * qd_ref[...]
        o_chunks.append(inner + cross)
        kdec = (kc * kd_ref[...]).astype(BF16)
        kv = lax.dot_general(kdec, vc.astype(BF16), (((0,), (0,)), ((), ())),
                             preferred_element_type=F32)
        sbd[...] = cd_ref[...] * state + jnp.where(bd_mask, kv, 0.0)
    o = jnp.concatenate(o_chunks, axis=0)
    branch_d = (zg * _sigmoid(zg)) * (group_standardize(o) * rg_ref[...])

    merged = jnp.zeros((TS, D_MODEL), F32)
    for b_idx, branch in enumerate((branch_a, branch_b, branch_c, branch_d)):
        gate = _sigmoid(proj(COL_GATES + b_idx * D_MODEL, D_MODEL))
        merged = merged + gate * _dot(branch.astype(BF16), bp_ref[b_idx])
    mix = _dot(merged.astype(BF16), wo_ref[...])
    _store_token_major(out_ref, _layer_norm_rows(ALPHA * h + mix, lng_ref[...], lnb_ref[...]), TS)


def _const_spec(shape):
    nd = len(shape)
    return pl.BlockSpec(shape, lambda b, i, _nd=nd: (0,) * _nd, pipeline_mode=pl.Buffered(1))


def _mixer_layer(h, cos_t, sin_t, consts, batch, seq, token_major_in):
    t = batch * seq
    steps = seq // TS
    row_map = lambda b, i: (b * steps + i, 0)
    tm_spec = pl.BlockSpec((TS * ROW_TILES, LANES), row_map)
    in_specs = [tm_spec if token_major_in else pl.BlockSpec((TS, D_MODEL), row_map),
                pl.BlockSpec((TS, LANES), row_map),
                pl.BlockSpec((TS, LANES), row_map)]
    in_specs += [_const_spec(c.shape) for c in consts]
    return pl.pallas_call(
        functools.partial(_mixer_kernel, token_major_in=token_major_in),
        out_shape=jax.ShapeDtypeStruct((t * ROW_TILES, LANES), F32),
        grid=(batch, steps),
        in_specs=in_specs,
        out_specs=tm_spec,
        scratch_shapes=[
            pltpu.VMEM((CONV_PAD + TS, WIDTH), F32),
            pltpu.VMEM((CONV_PAD + TS, WIDTH), F32),
            pltpu.VMEM((1, WIDTH), F32),
            pltpu.VMEM((WIDTH, WIDTH), F32),
            pltpu.VMEM((SCAN_PAD + TS, WIDTH), F32),
            pltpu.VMEM((SCAN_PAD + TS, WIDTH), F32),
        ],
        compiler_params=pltpu.CompilerParams(
            dimension_semantics=("arbitrary", "arbitrary"),
            vmem_limit_bytes=VMEM_LIMIT_BYTES),
        name="mixer_layer",
    )(h, cos_t, sin_t, *consts)


def _block_diag(w):
    heads, d, e = w.shape
    eye = jnp.eye(heads, dtype=w.dtype)
    return (eye[:, None, :, None] * w[:, :, None, :]).reshape(heads * d, heads * e)


def _retention_tables():
    log_gamma = jnp.log1p(-jnp.exp2(-5.0 - jnp.arange(N_HEADS, dtype=F32)))
    pos = jnp.arange(CHUNK, dtype=F32)
    diff = pos[:, None] - pos[None, :]
    decay = jnp.where(diff >= 0, jnp.exp(jnp.maximum(diff, 0.0) * log_gamma[:, None, None]), 0.0)
    dec_all = jnp.transpose(decay, (1, 0, 2)).reshape(CHUNK, N_HEADS * CHUNK)
    q_decay = jnp.exp((pos + 1.0)[:, None] * log_gamma)
    k_decay = jnp.exp((CHUNK - 1.0 - pos)[:, None] * log_gamma)
    qd_tab = jnp.repeat(q_decay, HEAD_DIM, axis=1)
    kd_tab = jnp.repeat(k_decay, HEAD_DIM, axis=1)
    chunk_decay = jnp.repeat(jnp.exp(CHUNK * log_gamma), HEAD_DIM)
    cd_tab = jnp.broadcast_to(chunk_decay[:, None], (WIDTH, WIDTH))
    return dec_all, qd_tab, kd_tab, cd_tab


def _router_kernel(h_ref, whi_ref, wlo_ref, b_ref, meta_ref, cnt_ref, carry):
    step = pl.program_id(0)

    @pl.when(step == 0)
    def _():
        carry[...] = jnp.zeros_like(carry)

    h = _load_token_major(h_ref, TB_ROUTE)
    h_hi = h.astype(BF16)
    h_lo = (h - h_hi.astype(F32)).astype(BF16)
    whi = whi_ref[...]
    logits = _dot(h_hi, whi) + _dot(h_lo, whi) + _dot(h_hi, wlo_ref[...]) + b_ref[...]
    lane = lax.broadcasted_iota(jnp.int32, logits.shape, 1)
    big = jnp.int32(2 ** 30)
    neg_inf = F32(-jnp.inf)

    gl = jnp.where(lane < N_GROUPS, logits, neg_inf)
    gmax = jnp.max(gl, axis=-1, keepdims=True)
    g_idx = jnp.min(jnp.where(gl == gmax, lane, big), axis=-1, keepdims=True)

    e_lane = lane - N_GROUPS
    in_group = (e_lane >= g_idx * EXPERTS_PER_GROUP) & (e_lane < (g_idx + 1) * EXPERTS_PER_GROUP)
    el = jnp.where(in_group, logits, neg_inf)
    m1 = jnp.max(el, axis=-1, keepdims=True)
    i1 = jnp.min(jnp.where(el == m1, lane, big), axis=-1, keepdims=True)
    el2 = jnp.where(lane == i1, neg_inf, el)
    m2 = jnp.max(el2, axis=-1, keepdims=True)
    i2 = jnp.min(jnp.where(el2 == m2, lane, big), axis=-1, keepdims=True)
    base = (g_idx * EXPERTS_PER_GROUP + N_GROUPS).astype(F32)
    lo = jnp.minimum(i1, i2).astype(F32) - base
    hi = jnp.maximum(i1, i2).astype(F32) - base
    pair = lo * (2.0 * EXPERTS_PER_GROUP - 1.0 - lo) * 0.5 + (hi - lo - 1.0)
    cls = (g_idx.astype(F32) * PAIRS_PER_GROUP + pair).astype(jnp.int32)

    onehot = jnp.where(lane == cls, 1.0, 0.0)
    tb = h.shape[0]
    r_i = lax.broadcasted_iota(jnp.int32, (tb, tb), 0)
    c_i = lax.broadcasted_iota(jnp.int32, (tb, tb), 1)
    strict_lower = jnp.where(r_i > c_i, 1.0, 0.0).astype(BF16)
    before = _dot(strict_lower, onehot.astype(BF16)) + carry[...]
    rank = jnp.sum(jnp.where(lane == cls, before, 0.0), axis=-1, keepdims=True)
    carry[...] = carry[...] + jnp.sum(onehot, axis=0, keepdims=True)

    meta = jnp.where(lane == 0, cls.astype(F32), 0.0) + jnp.where(lane == 1, rank, 0.0)
    meta_ref[...] = jnp.transpose(meta)[0:SUBLANES, :].astype(jnp.int32)
    cnt_ref[...] = jnp.broadcast_to(carry[...], cnt_ref.shape).astype(jnp.int32)


def _router(h, t, whi, wlo, bias):
    return pl.pallas_call(
        _router_kernel,
        out_shape=(jax.ShapeDtypeStruct((SUBLANES, t), jnp.int32),
                   jax.ShapeDtypeStruct((SUBLANES, LANES), jnp.int32)),
        grid=(t // TB_ROUTE,),
        in_specs=[pl.BlockSpec((TB_ROUTE * ROW_TILES, LANES), lambda i: (i, 0)),
                  pl.BlockSpec((D_MODEL, LANES), lambda i: (0, 0)),
                  pl.BlockSpec((D_MODEL, LANES), lambda i: (0, 0)),
                  pl.BlockSpec((1, LANES), lambda i: (0, 0))],
        out_specs=(pl.BlockSpec((SUBLANES, TB_ROUTE), lambda i: (0, i)),
                   pl.BlockSpec((SUBLANES, LANES), lambda i: (0, 0))),
        scratch_shapes=[pltpu.VMEM((1, LANES), F32)],
        compiler_params=pltpu.CompilerParams(dimension_semantics=("arbitrary",)),
        name="router",
    )(h, whi, wlo, bias)


def _token_rows(tok):
    if isinstance(tok, int):
        return pl.ds(tok * ROW_TILES, ROW_TILES)
    return pl.ds(pl.multiple_of(tok * ROW_TILES, ROW_TILES), ROW_TILES)


def _dispatch_kernel(offs_ref, cls_ref, rank_ref, h_ref, dst0_hbm, xs_in_hbm, xs_hbm, dst_hbm,
                     dst_ref, row_sem, map_sem):
    del xs_in_hbm
    step = pl.program_id(0)

    @pl.when(step == 0)
    def _():
        load = pltpu.make_async_copy(dst0_hbm, dst_ref, map_sem)
        load.start()
        load.wait()

    def send(g, _):
        for k in range(DISPATCH_UNROLL):
            r = g * DISPATCH_UNROLL + k
            p = offs_ref[cls_ref[r]] + rank_ref[r]
            dst_ref[p] = step * TB_DISPATCH + r
            pltpu.make_async_copy(h_ref.at[_token_rows(r), :], xs_hbm.at[_token_rows(p), :],
                                  row_sem).start(priority=k % 2)
        return 0

    lax.fori_loop(0, TB_DISPATCH // DISPATCH_UNROLL, send, 0)

    pltpu.make_async_copy(h_ref, xs_hbm.at[pl.ds(0, TB_DISPATCH * ROW_TILES), :], row_sem).wait()

    @pl.when(step == pl.num_programs(0) - 1)
    def _():
        store = pltpu.make_async_copy(dst_ref, dst_hbm, map_sem)
        store.start()
        store.wait()


def _dispatch(h, offs, cls, rank, n_rows):
    t = cls.shape[0]
    dst0 = jnp.zeros((n_rows,), jnp.int32)
    xs0 = jnp.zeros((n_rows * ROW_TILES, LANES), F32)
    smem_blk = pl.BlockSpec((TB_DISPATCH,), lambda i, offs: (i,), memory_space=pltpu.SMEM)
    hbm = pl.BlockSpec(memory_space=pl.ANY)
    return pl.pallas_call(
        _dispatch_kernel,
        out_shape=(jax.ShapeDtypeStruct((n_rows * ROW_TILES, LANES), F32),
                   jax.ShapeDtypeStruct((n_rows,), jnp.int32)),
        grid_spec=pltpu.PrefetchScalarGridSpec(
            num_scalar_prefetch=1,
            grid=(t // TB_DISPATCH,),
            in_specs=[smem_blk, smem_blk,
                      pl.BlockSpec((TB_DISPATCH * ROW_TILES, LANES), lambda i, offs: (i, 0)),
                      hbm, hbm],
            out_specs=(hbm, hbm),
            scratch_shapes=[pltpu.SMEM((n_rows,), jnp.int32),
                            pltpu.SemaphoreType.DMA,
                            pltpu.SemaphoreType.DMA]),
        input_output_aliases={5: 0},
        compiler_params=pltpu.CompilerParams(dimension_semantics=("arbitrary",)),
        name="moe_dispatch",
    )(offs, cls, rank, h, dst0, xs0)


def _expert_kernel(ea_ref, eb_ref, nvalid_ref, nused_ref, *refs):
    k_tiles = TILES_PER_STEP
    dst_refs, x_refs = refs[:k_tiles], refs[k_tiles:2 * k_tiles]
    wr_ref, br_ref = refs[2 * k_tiles:2 * k_tiles + 2]
    w_refs = refs[2 * k_tiles + 2:6 * k_tiles + 2]
    lng_ref, lnb_ref, out_hbm, ybuf, ssem = refs[6 * k_tiles + 2:]
    step = pl.program_id(0)
    n_used = nused_ref[0]
    first_tile = step * k_tiles
    slot = step % 2
    other = 1 - slot

    def step_rows(s):
        base = jnp.maximum(s, 0) * k_tiles
        return sum(nvalid_ref[base + k] for k in range(k_tiles))

    def start_rows(k, n, s):
        for r in range(TM):
            @pl.when(r < n)
            def _(r=r):
                pltpu.make_async_copy(ybuf.at[s, _token_rows(k * TM + r), :],
                                      out_hbm.at[_token_rows(dst_refs[k][r]), :],
                                      ssem.at[s]).start(priority=r % 2)

    def wait_rows(n, s):
        @pl.when(n > 0)
        def _():
            rows = pl.ds(0, n * ROW_TILES)
            pltpu.make_async_copy(ybuf.at[s, rows, :], out_hbm.at[rows, :], ssem.at[s]).wait()

    def tile_result(k):
        x_ref = x_refs[k]
        wgua_ref, wda_ref, wgub_ref, wdb_ref = w_refs[4 * k:4 * k + 4]
        xb = _load_token_major(x_ref, TM).astype(BF16)

        logits = _dot(xb, wr_ref[...]) + br_ref[...]
        lane = lax.broadcasted_iota(jnp.int32, logits.shape, 1)
        ea, eb = ea_ref[first_tile + k], eb_ref[first_tile + k]
        group = ea // EXPERTS_PER_GROUP
        gl = jnp.where(lane < N_GROUPS, logits, F32(-jnp.inf))
        gmax = jnp.max(gl, axis=-1, keepdims=True)
        gsum = jnp.sum(jnp.where(lane < N_GROUPS, jnp.exp(gl - gmax), 0.0), axis=-1, keepdims=True)
        lg = jnp.sum(jnp.where(lane == group, logits, 0.0), axis=-1, keepdims=True)
        g_top_p = jnp.exp(lg - gmax) / gsum
        la = jnp.sum(jnp.where(lane == ea + N_GROUPS, logits, 0.0), axis=-1, keepdims=True)
        lb = jnp.sum(jnp.where(lane == eb + N_GROUPS, logits, 0.0), axis=-1, keepdims=True)
        w_a = g_top_p / (1.0 + jnp.exp(lb - la))
        w_b = g_top_p / (1.0 + jnp.exp(la - lb))

        def hidden(wgu_ref, weight):
            gu = _dot(xb, wgu_ref[0])
            gate, up = gu[:, :D_EXPERT], gu[:, D_EXPERT:]
            return ((gate * _sigmoid(gate)) * up * weight).astype(BF16)

        ffn = _dot(hidden(wgua_ref, w_a), wda_ref[0]) + _dot(hidden(wgub_ref, w_b), wdb_ref[0])
        return _layer_norm_rows(ALPHA * _load_token_major(x_ref, TM) + ffn, lng_ref[...], lnb_ref[...])

    @pl.when(first_tile < n_used)
    def _():
        results = [tile_result(k) for k in range(k_tiles)]

        @pl.when(step >= 2)
        def _():
            wait_rows(step_rows(step - 2), slot)

        for k in range(k_tiles):
            for c in range(ROW_TILES):
                ybuf[slot, pl.ds(k * TM * ROW_TILES + c, TM, stride=ROW_TILES), :] = (
                    results[k][:, c * LANES:(c + 1) * LANES])
            start_rows(k, nvalid_ref[first_tile + k], slot)

        @pl.when(first_tile + k_tiles >= n_used)
        def _():
            @pl.when(step >= 1)
            def _():
                wait_rows(step_rows(step - 1), other)

            wait_rows(step_rows(step), slot)


def _experts(xs, t, dst, tile_ea, tile_eb, tile_nvalid, n_used, w_router, b_router, wgu, wd, ln_g, ln_b):
    k_tiles = TILES_PER_STEP
    n_tiles = dst.shape[0] // TM
    assert n_tiles % k_tiles == 0
    const = lambda shape: pl.BlockSpec(shape, lambda i, ea, eb, nv, nu: (0,) * len(shape))
    dst_spec = lambda k: pl.BlockSpec((TM,), lambda i, ea, eb, nv, nu: (i * k_tiles + k,),
                                      memory_space=pltpu.SMEM)
    x_spec = lambda k: pl.BlockSpec((TM * ROW_TILES, LANES), lambda i, ea, eb, nv, nu: (i * k_tiles + k, 0))
    wgu_spec = lambda k, which: pl.BlockSpec(
        (1, D_MODEL, 2 * D_EXPERT), lambda i, ea, eb, nv, nu: ((ea, eb)[which][i * k_tiles + k], 0, 0))
    wd_spec = lambda k, which: pl.BlockSpec(
        (1, D_EXPERT, D_MODEL), lambda i, ea, eb, nv, nu: ((ea, eb)[which][i * k_tiles + k], 0, 0))
    tiles = range(k_tiles)
    weight_specs, weight_args = [], []
    for k in tiles:
        weight_specs += [wgu_spec(k, 0), wd_spec(k, 0), wgu_spec(k, 1), wd_spec(k, 1)]
        weight_args += [wgu, wd, wgu, wd]
    grid_spec = pltpu.PrefetchScalarGridSpec(
        num_scalar_prefetch=4,
        grid=(n_tiles // k_tiles,),
        in_specs=([dst_spec(k) for k in tiles] + [x_spec(k) for k in tiles]
                  + [const((D_MODEL, LANES)), const((1, LANES))] + weight_specs
                  + [const((1, D_MODEL)), const((1, D_MODEL))]),
        out_specs=pl.BlockSpec(memory_space=pl.ANY),
        scratch_shapes=[pltpu.VMEM((2, k_tiles * TM * ROW_TILES, LANES), F32),
                        pltpu.SemaphoreType.DMA((2,))],
    )
    return pl.pallas_call(
        _expert_kernel,
        out_shape=jax.ShapeDtypeStruct((t * ROW_TILES, LANES), F32),
        grid_spec=grid_spec,
        compiler_params=pltpu.CompilerParams(dimension_semantics=("arbitrary",),
                                             vmem_limit_bytes=VMEM_LIMIT_BYTES),
        name="moe_experts",
    )(tile_ea, tile_eb, tile_nvalid, n_used, *([dst] * k_tiles), *([xs] * k_tiles),
      w_router, b_router, *weight_args, ln_g, ln_b)


def _class_expert_tables():
    first, second = [], []
    for g in range(N_GROUPS):
        for lo in range(EXPERTS_PER_GROUP):
            for hi in range(lo + 1, EXPERTS_PER_GROUP):
                first.append(g * EXPERTS_PER_GROUP + lo)
                second.append(g * EXPERTS_PER_GROUP + hi)
    return jnp.array(first, jnp.int32), jnp.array(second, jnp.int32)


def _moe_layer(h, t, wg, bg, we, be, w_gate, w_up, w_down, ln_g, ln_b):
    n_tiles = t // TM + N_CLASSES
    n_rows = n_tiles * TM

    w_router = jnp.zeros((D_MODEL, LANES), F32)
    w_router = w_router.at[:, :N_GROUPS].set(wg).at[:, N_GROUPS:N_GROUPS + N_EXPERTS].set(we)
    b_router = jnp.zeros((1, LANES), F32)
    b_router = b_router.at[0, :N_GROUPS].set(bg).at[0, N_GROUPS:N_GROUPS + N_EXPERTS].set(be)
    w_hi = w_router.astype(BF16)
    w_lo = (w_router - w_hi.astype(F32)).astype(BF16)

    meta, counts = _router(h, t, w_hi, w_lo, b_router)

    cnt = counts[0]
    tiles_per = (cnt + TM - 1) // TM
    tile_end = jnp.cumsum(tiles_per)
    offs = (tile_end - tiles_per) * TM
    n_used = tile_end[-1]
    tile_id = jnp.arange(n_tiles, dtype=jnp.int32)
    tile_cls = jnp.sum((jnp.minimum(tile_id, n_used - 1)[:, None] >= tile_end[None, :N_CLASSES])
                       .astype(jnp.int32), axis=1)
    seg_tile = tile_id - (tile_end - tiles_per)[tile_cls]
    tile_nvalid = jnp.where(tile_id < n_used, jnp.clip(cnt[tile_cls] - seg_tile * TM, 0, TM), 0)
    first, second = _class_expert_tables()
    xs, dst = _dispatch(h, offs, meta[0], meta[1], n_rows)
    wgu = jnp.concatenate([w_gate, w_up], axis=-1).astype(BF16)
    return _experts(xs, t, dst, first[tile_cls], second[tile_cls], tile_nvalid, n_used.reshape(1),
                    w_hi, b_router, wgu, w_down.astype(BF16), ln_g, ln_b)


def kernel(x, positions, w_in, sc_conv_w, sc_conv_b, lru_conv_w, lru_conv_b, lru_w_r, lru_b_r,
           lru_w_i, lru_b_i, lru_lambda, sg_norm_g, sg_w_s, sg_b_s, ret_norm_g, branch_proj, w_out,
           ln_mix_g, ln_mix_b, router_group_w, router_group_b, router_expert_w, router_expert_b,
           exp_w_gate, exp_w_up, exp_w_down, ln_ffn_g, ln_ffn_b):
    batch, seq, d = x.shape
    assert d == D_MODEL and seq % TS == 0 and w_in.shape[-1] == N_IN
    depth = w_in.shape[0]
    t = batch * seq
    cos_t, sin_t = _rope_tables(positions)
    dec_all, qd_tab, kd_tab, cd_tab = _retention_tables()
    avg = _block_diag(jnp.full((N_HEADS, HEAD_DIM, HEAD_DIM), 1.0 / HEAD_DIM, F32)).astype(BF16)

    h = x.reshape(t, d)
    for l in range(depth):
        w_ri = jnp.concatenate([_block_diag(lru_w_r[l]), _block_diag(lru_w_i[l])], axis=1).astype(BF16)
        b_ri = jnp.concatenate([lru_b_r[l], lru_b_i[l]])[None, :]
        bs_tab = jnp.repeat(sg_b_s[l].T, HEAD_DIM, axis=1)
        consts = (
            w_in[l].astype(BF16), sc_conv_w[l], sc_conv_b[l][None, :], lru_conv_w[l], lru_conv_b[l][None, :],
            w_ri, b_ri, lru_lambda[l][None, :], sg_norm_g[l][None, :], sg_w_s[l], bs_tab,
            ret_norm_g[l][None, :], dec_all, qd_tab, kd_tab, cd_tab, avg,
            branch_proj[l].astype(BF16), w_out[l].astype(BF16), ln_mix_g[l][None, :], ln_mix_b[l][None, :],
        )
        h = _mixer_layer(h, cos_t, sin_t, consts, batch, seq, token_major_in=l > 0)
        h = _moe_layer(h, t, router_group_w[l], router_group_b[l], router_expert_w[l], router_expert_b[l],
                       exp_w_gate[l], exp_w_up[l], exp_w_down[l], ln_ffn_g[l][None, :], ln_ffn_b[l][None, :])
    return h.reshape(batch, seq, d)
```

```python
import functools

import jax
import jax.numpy as jnp
from jax import lax
from jax.experimental import pallas as pl
from jax.experimental.pallas import tpu as pltpu

F32 = jnp.float32
BF16 = jnp.bfloat16

LANES = 128
SUBLANES = 8
VMEM_LIMIT_BYTES = 56 * 1024 * 1024

D_MODEL = 1024
ROW_TILES = D_MODEL // LANES
assert ROW_TILES == SUBLANES
WIDTH = D_MODEL // 4
N_HEADS = 4
N_BRANCH = 4
HEAD_DIM = WIDTH // N_HEADS
CHUNK = 128
SC_K = 3
LRU_K = 4
LRU_C = 8.0
ROPE_BASE = 10000.0
N_GROUPS = 4
EXPERTS_PER_GROUP = 8
N_EXPERTS = N_GROUPS * EXPERTS_PER_GROUP
D_EXPERT = D_MODEL // 4
LN_EPS = 1e-5
DEPTH = 2
ALPHA = (2.0 * DEPTH) ** 0.25

COL_SC_B, COL_SC_C, COL_SC_X, COL_LRU, COL_SG_U, COL_SG_V, COL_Q, COL_K, COL_V, COL_G = (
    i * WIDTH for i in range(10))
COL_GATES = 10 * WIDTH
N_IN = COL_GATES + 4 * D_MODEL

TS = 512
GATE_CHUNK = 256
SCAN_PAD = TS // 2
CONV_PAD = SUBLANES
TB_ROUTE = 512
TB_DISPATCH = 512
DISPATCH_UNROLL = 8
TM = 128
TILES_PER_STEP = 4
PAIRS_PER_GROUP = EXPERTS_PER_GROUP * (EXPERTS_PER_GROUP - 1) // 2
N_CLASSES = N_GROUPS * PAIRS_PER_GROUP
assert N_CLASSES <= LANES


def _sigmoid(x):
    return 0.5 * jnp.tanh(0.5 * x) + 0.5


def _dot(a, b):
    return jnp.dot(a, b, preferred_element_type=F32)


def _load_token_major(ref, rows, lead=()):
    return jnp.concatenate(
        [ref[lead + (pl.ds(c, rows, stride=ROW_TILES), slice(None))] for c in range(ROW_TILES)], axis=1)


def _store_token_major(ref, y, rows, lead=()):
    for c in range(ROW_TILES):
        ref[lead + (pl.ds(c, rows, stride=ROW_TILES), slice(None))] = y[:, c * LANES:(c + 1) * LANES]


def _layer_norm_rows(y, g, b):
    mu = jnp.mean(y, axis=-1, keepdims=True)
    yc = y - mu
    var = jnp.mean(yc * yc, axis=-1, keepdims=True)
    return yc * lax.rsqrt(var + LN_EPS) * g + b


def _rope_table_kernel(pos_ref, freq_ref, cos_ref, sin_ref):
    ang = pos_ref[...].astype(F32) * freq_ref[...]
    lane = lax.broadcasted_iota(jnp.int32, ang.shape, 1)
    first_half = (lane % HEAD_DIM) < (HEAD_DIM // 2)
    cos_ref[...] = jnp.cos(ang)
    s = jnp.sin(ang)
    sin_ref[...] = jnp.where(first_half, -s, s)


def _rope_tables(positions):
    t = positions.size
    half = HEAD_DIM // 2
    inv_freq = ROPE_BASE ** (-jnp.arange(half, dtype=F32) / half)
    freq_row = jnp.tile(inv_freq, LANES // half)[None, :]
    rows = 1024
    return pl.pallas_call(
        _rope_table_kernel,
        out_shape=(jax.ShapeDtypeStruct((t, LANES), F32), jax.ShapeDtypeStruct((t, LANES), F32)),
        grid=(t // rows,),
        in_specs=[pl.BlockSpec((rows, 1), lambda i: (i, 0)),
                  pl.BlockSpec((1, LANES), lambda i: (0, 0))],
        out_specs=(pl.BlockSpec((rows, LANES), lambda i: (i, 0)),
                   pl.BlockSpec((rows, LANES), lambda i: (i, 0))),
        name="rope_tables",
    )(positions.reshape(t, 1), freq_row)


def _mixer_kernel(h_ref, cos_ref, sin_ref, w_in_ref, scw_ref, scb_ref, lcw_ref, lcb_ref,
                  wri_ref, bri_ref, lam_ref, sgg_ref, ws_ref, bs_ref, rg_ref,
                  dec_ref, qd_ref, kd_ref, cd_ref, avg_ref, bp_ref, wo_ref, lng_ref, lnb_ref,
                  out_ref, ubuf, xbuf, hcar, sbd, abuf, hbuf, gbuf, *, token_major_in):
    step = pl.program_id(1)

    @pl.when(step == 0)
    def _():
        ubuf[0:CONV_PAD, :] = jnp.zeros((CONV_PAD, WIDTH), F32)
        xbuf[0:CONV_PAD, :] = jnp.zeros((CONV_PAD, WIDTH), F32)
        hcar[...] = jnp.zeros_like(hcar)
        sbd[...] = jnp.zeros_like(sbd)
        abuf[0:SCAN_PAD, :] = jnp.ones((SCAN_PAD, WIDTH), F32)
        hbuf[0:SCAN_PAD, :] = jnp.zeros((SCAN_PAD, WIDTH), F32)

    h = _load_token_major(h_ref, TS) if token_major_in else h_ref[...]
    hb = h.astype(BF16)

    def proj(col, width=WIDTH):
        return _dot(hb, w_in_ref[0, :, col:col + width])

    pending_gate_cols = list(range(0, N_BRANCH * D_MODEL, GATE_CHUNK))

    def emit_gates(n_chunks=1):
        for _ in range(n_chunks):
            if pending_gate_cols:
                col = pending_gate_cols.pop(0)
                gbuf[:, col:col + GATE_CHUNK] = _sigmoid(proj(COL_GATES + col, GATE_CHUNK))

    lane = lax.broadcasted_iota(jnp.int32, (CHUNK, WIDTH), 1)
    head_of_lane = lane // HEAD_DIM
    avg = avg_ref[...]

    def group_standardize(x):
        mean = _dot(x.astype(BF16), avg)
        xc = x - mean
        var = _dot((xc * xc).astype(BF16), avg)
        return xc * lax.rsqrt(var + LN_EPS)

    sc_b, sc_c, sc_x = proj(COL_SC_B), proj(COL_SC_C), proj(COL_SC_X)
    u = sc_c * sc_x
    ubuf[CONV_PAD:CONV_PAD + TS, :] = u
    conv = scw_ref[2:3, :] * u + scb_ref[...]
    for j in range(SC_K - 1):
        back = SC_K - 1 - j
        conv = conv + scw_ref[j:j + 1, :] * ubuf[CONV_PAD - back:CONV_PAD - back + TS, :]
    ubuf[0:CONV_PAD, :] = ubuf[TS:TS + CONV_PAD, :]
    branch_a = sc_b * conv
    emit_gates()

    lx = proj(COL_LRU)
    xbuf[CONV_PAD:CONV_PAD + TS, :] = lx
    xc = lcw_ref[LRU_K - 1:LRU_K, :] * lx + lcb_ref[...]
    for j in range(LRU_K - 1):
        back = LRU_K - 1 - j
        xc = xc + lcw_ref[j:j + 1, :] * xbuf[CONV_PAD - back:CONV_PAD - back + TS, :]
    xbuf[0:CONV_PAD, :] = xbuf[TS:TS + CONV_PAD, :]
    emit_gates()
    ri = _dot(xc.astype(BF16), wri_ref[...]) + bri_ref[...]
    r = _sigmoid(ri[:, :WIDTH])
    ig = _sigmoid(ri[:, WIDTH:])
    neg_lam = -lam_ref[...]
    softplus = jnp.maximum(neg_lam, 0.0) + jnp.log1p(jnp.exp(-jnp.abs(neg_lam)))
    log_a = (-LRU_C) * r * softplus
    a = jnp.exp(log_a)
    th = jnp.tanh(log_a)
    uu = jnp.sqrt((-2.0) * th / (1.0 - th)) * (ig * xc)
    row = lax.broadcasted_iota(jnp.int32, (TS, WIDTH), 0)
    uu = uu + jnp.where(row == 0, a * hcar[...], 0.0)
    abuf[SCAN_PAD:SCAN_PAD + TS, :] = a
    hbuf[SCAN_PAD:SCAN_PAD + TS, :] = uu
    shift = 1
    while shift < TS:
        a_cur = abuf[SCAN_PAD:SCAN_PAD + TS, :]
        h_prev = hbuf[SCAN_PAD - shift:SCAN_PAD - shift + TS, :]
        h_new = a_cur * h_prev + hbuf[SCAN_PAD:SCAN_PAD + TS, :]
        if shift * 2 < TS:
            a_new = a_cur * abuf[SCAN_PAD - shift:SCAN_PAD - shift + TS, :]
            abuf[SCAN_PAD:SCAN_PAD + TS, :] = a_new
        hbuf[SCAN_PAD:SCAN_PAD + TS, :] = h_new
        emit_gates()
        shift *= 2
    branch_b = hbuf[SCAN_PAD:SCAN_PAD + TS, :]
    hcar[...] = hbuf[SCAN_PAD + TS - 1:SCAN_PAD + TS, :]

    gu = jax.nn.gelu(proj(COL_SG_U))
    gv = jax.nn.gelu(proj(COL_SG_V))
    emit_gates()
    vn = (group_standardize(gv) * sgg_ref[...]).astype(BF16)
    trow = lax.broadcasted_iota(jnp.int32, (CHUNK, CHUNK), 0)
    tcol = lax.broadcasted_iota(jnp.int32, (CHUNK, CHUNK), 1)
    w_causal = [jnp.where(trow >= tcol, ws_ref[g], 0.0).astype(BF16) for g in range(N_HEADS)]
    sv_chunks = []
    for c in range(TS // CHUNK):
        vch = vn[c * CHUNK:(c + 1) * CHUNK, :]
        sv = bs_ref[...]
        for g in range(N_HEADS):
            sv = sv + jnp.where(head_of_lane == g, _dot(w_causal[g], vch), 0.0)
        sv_chunks.append(sv)
    branch_c = gu * jnp.concatenate(sv_chunks, axis=0)
    emit_gates()

    cos = jnp.concatenate([cos_ref[...]] * (WIDTH // LANES), axis=1)
    sin = jnp.concatenate([sin_ref[...]] * (WIDTH // LANES), axis=1)
    lane_ts = lax.broadcasted_iota(jnp.int32, (TS, WIDTH), 1)
    first_half = (lane_ts % HEAD_DIM) < (HEAD_DIM // 2)

    def rope(t):
        swapped = jnp.where(first_half,
                            pltpu.roll(t, WIDTH - HEAD_DIM // 2, 1),
                            pltpu.roll(t, HEAD_DIM // 2, 1))
        return t * cos + swapped * sin

    q = rope(proj(COL_Q))
    k = rope(proj(COL_K)) * (HEAD_DIM ** -0.5)
    emit_gates()
    v = proj(COL_V)
    zg = proj(COL_G)
    bd_mask = (lax.broadcasted_iota(jnp.int32, (WIDTH, WIDTH), 0) // HEAD_DIM
               == lax.broadcasted_iota(jnp.int32, (WIDTH, WIDTH), 1) // HEAD_DIM)
    o_chunks = []
    for c in range(TS // CHUNK):
        sl = slice(c * CHUNK, (c + 1) * CHUNK)
        qc, kc, vc = q[sl, :], k[sl, :], v[sl, :]
        qcb = qc.astype(BF16)
        kstack = jnp.concatenate(
            [jnp.where(head_of_lane == hh, kc, 0.0) for hh in range(N_HEADS)], axis=0).astype(BF16)
        vstack = jnp.concatenate(
            [jnp.where(head_of_lane == hh, vc, 0.0) for hh in range(N_HEADS)], axis=0).astype(BF16)
        scores = lax.dot_general(qcb, kstack, (((1,), (1,)), ((), ())),
                                 preferred_element_type=F32)
        inner = _dot((scores * dec_ref[...]).astype(BF16), vstack)
        state = sbd[...]
        cross = _dot(qcb, state.astype(BF16)) * qd_ref[...]
        o_chunks.append(inner + cross)
        kdec = (kc * kd_ref[...]).astype(BF16)
        kv = lax.dot_general(kdec, vc.astype(BF16), (((0,), (0,)), ((), ())),
                             preferred_element_type=F32)
        sbd[...] = cd_ref[...] * state + jnp.where(bd_mask, kv, 0.0)
        emit_gates()
    o = jnp.concatenate(o_chunks, axis=0)
    branch_d = (zg * _sigmoid(zg)) * (group_standardize(o) * rg_ref[...])

    emit_gates(len(pending_gate_cols))
    merged = jnp.zeros((TS, D_MODEL), F32)
    for b_idx, branch in enumerate((branch_a, branch_b, branch_c, branch_d)):
        gate = gbuf[:, b_idx * D_MODEL:(b_idx + 1) * D_MODEL]
        merged = merged + gate * _dot(branch.astype(BF16), bp_ref[0, b_idx])
    mix = _dot(merged.astype(BF16), wo_ref[0])
    _store_token_major(out_ref, _layer_norm_rows(ALPHA * h + mix, lng_ref[...], lnb_ref[...]), TS)


def _const_spec(arr, layer):
    if layer is None:
        return pl.BlockSpec(arr.shape, lambda b, i, _nd=arr.ndim: (0,) * _nd, pipeline_mode=pl.Buffered(1))
    return pl.BlockSpec((1,) + arr.shape[1:], lambda b, i, _nd=arr.ndim: (layer,) + (0,) * (_nd - 1),
                        pipeline_mode=pl.Buffered(1))


def _mixer_layer(h, cos_t, sin_t, consts, batch, seq, token_major_in):
    t = batch * seq
    steps = seq // TS
    row_map = lambda b, i: (b * steps + i, 0)
    tm_spec = pl.BlockSpec((TS * ROW_TILES, LANES), row_map)
    in_specs = [tm_spec if token_major_in else pl.BlockSpec((TS, D_MODEL), row_map),
                pl.BlockSpec((TS, LANES), row_map),
                pl.BlockSpec((TS, LANES), row_map)]
    in_specs += [_const_spec(arr, layer) for arr, layer in consts]
    return pl.pallas_call(
        functools.partial(_mixer_kernel, token_major_in=token_major_in),
        out_shape=jax.ShapeDtypeStruct((t * ROW_TILES, LANES), F32),
        grid=(batch, steps),
        in_specs=in_specs,
        out_specs=tm_spec,
        scratch_shapes=[
            pltpu.VMEM((CONV_PAD + TS, WIDTH), F32),
            pltpu.VMEM((CONV_PAD + TS, WIDTH), F32),
            pltpu.VMEM((1, WIDTH), F32),
            pltpu.VMEM((WIDTH, WIDTH), F32),
            pltpu.VMEM((SCAN_PAD + TS, WIDTH), F32),
            pltpu.VMEM((SCAN_PAD + TS, WIDTH), F32),
            pltpu.VMEM((TS, N_BRANCH * D_MODEL), F32),
        ],
        compiler_params=pltpu.CompilerParams(
            dimension_semantics=("arbitrary", "arbitrary"),
            vmem_limit_bytes=VMEM_LIMIT_BYTES),
        name="mixer_layer",
    )(h, cos_t, sin_t, *[arr for arr, _ in consts])


def _block_diag(w):
    heads, d, e = w.shape
    eye = jnp.eye(heads, dtype=w.dtype)
    return (eye[:, None, :, None] * w[:, :, None, :]).reshape(heads * d, heads * e)


def _retention_tables():
    log_gamma = jnp.log1p(-jnp.exp2(-5.0 - jnp.arange(N_HEADS, dtype=F32)))
    pos = jnp.arange(CHUNK, dtype=F32)
    diff = pos[:, None] - pos[None, :]
    decay = jnp.where(diff >= 0, jnp.exp(jnp.maximum(diff, 0.0) * log_gamma[:, None, None]), 0.0)
    dec_all = jnp.transpose(decay, (1, 0, 2)).reshape(CHUNK, N_HEADS * CHUNK)
    q_decay = jnp.exp((pos + 1.0)[:, None] * log_gamma)
    k_decay = jnp.exp((CHUNK - 1.0 - pos)[:, None] * log_gamma)
    qd_tab = jnp.repeat(q_decay, HEAD_DIM, axis=1)
    kd_tab = jnp.repeat(k_decay, HEAD_DIM, axis=1)
    chunk_decay = jnp.repeat(jnp.exp(CHUNK * log_gamma), HEAD_DIM)
    cd_tab = jnp.broadcast_to(chunk_decay[:, None], (WIDTH, WIDTH))
    return dec_all, qd_tab, kd_tab, cd_tab


def _router_kernel(h_ref, whi_ref, wlo_ref, b_ref, meta_ref, cnt_ref, carry):
    step = pl.program_id(0)

    @pl.when(step == 0)
    def _():
        carry[...] = jnp.zeros_like(carry)

    h = _load_token_major(h_ref, TB_ROUTE)
    h_hi = h.astype(BF16)
    h_lo = (h - h_hi.astype(F32)).astype(BF16)
    whi = whi_ref[...]
    logits = _dot(h_hi, whi) + _dot(h_lo, whi) + _dot(h_hi, wlo_ref[...]) + b_ref[...]
    lane = lax.broadcasted_iota(jnp.int32, logits.shape, 1)
    big = jnp.int32(2 ** 30)
    neg_inf = F32(-jnp.inf)

    gl = jnp.where(lane < N_GROUPS, logits, neg_inf)
    gmax = jnp.max(gl, axis=-1, keepdims=True)
    g_idx = jnp.min(jnp.where(gl == gmax, lane, big), axis=-1, keepdims=True)

    e_lane = lane - N_GROUPS
    in_group = (e_lane >= g_idx * EXPERTS_PER_GROUP) & (e_lane < (g_idx + 1) * EXPERTS_PER_GROUP)
    el = jnp.where(in_group, logits, neg_inf)
    m1 = jnp.max(el, axis=-1, keepdims=True)
    i1 = jnp.min(jnp.where(el == m1, lane, big), axis=-1, keepdims=True)
    el2 = jnp.where(lane == i1, neg_inf, el)
    m2 = jnp.max(el2, axis=-1, keepdims=True)
    i2 = jnp.min(jnp.where(el2 == m2, lane, big), axis=-1, keepdims=True)
    base = (g_idx * EXPERTS_PER_GROUP + N_GROUPS).astype(F32)
    lo = jnp.minimum(i1, i2).astype(F32) - base
    hi = jnp.maximum(i1, i2).astype(F32) - base
    pair = lo * (2.0 * EXPERTS_PER_GROUP - 1.0 - lo) * 0.5 + (hi - lo - 1.0)
    cls = (g_idx.astype(F32) * PAIRS_PER_GROUP + pair).astype(jnp.int32)

    onehot = jnp.where(lane == cls, 1.0, 0.0)
    tb = h.shape[0]
    r_i = lax.broadcasted_iota(jnp.int32, (tb, tb), 0)
    c_i = lax.broadcasted_iota(jnp.int32, (tb, tb), 1)
    strict_lower = jnp.where(r_i > c_i, 1.0, 0.0).astype(BF16)
    before = _dot(strict_lower, onehot.astype(BF16)) + carry[...]
    rank = jnp.sum(jnp.where(lane == cls, before, 0.0), axis=-1, keepdims=True)
    carry[...] = carry[...] + jnp.sum(onehot, axis=0, keepdims=True)

    meta = jnp.where(lane == 0, cls.astype(F32), 0.0) + jnp.where(lane == 1, rank, 0.0)
    meta_ref[...] = jnp.transpose(meta)[0:SUBLANES, :].astype(jnp.int32)
    cnt_ref[...] = jnp.broadcast_to(carry[...], cnt_ref.shape).astype(jnp.int32)


def _router(h, t, whi, wlo, bias):
    return pl.pallas_call(
        _router_kernel,
        out_shape=(jax.ShapeDtypeStruct((SUBLANES, t), jnp.int32),
                   jax.ShapeDtypeStruct((SUBLANES, LANES), jnp.int32)),
        grid=(t // TB_ROUTE,),
        in_specs=[pl.BlockSpec((TB_ROUTE * ROW_TILES, LANES), lambda i: (i, 0)),
                  pl.BlockSpec((D_MODEL, LANES), lambda i: (0, 0)),
                  pl.BlockSpec((D_MODEL, LANES), lambda i: (0, 0)),
                  pl.BlockSpec((1, LANES), lambda i: (0, 0))],
        out_specs=(pl.BlockSpec((SUBLANES, TB_ROUTE), lambda i: (0, i)),
                   pl.BlockSpec((SUBLANES, LANES), lambda i: (0, 0))),
        scratch_shapes=[pltpu.VMEM((1, LANES), F32)],
        compiler_params=pltpu.CompilerParams(dimension_semantics=("arbitrary",)),
        name="router",
    )(h, whi, wlo, bias)


def _token_rows(tok):
    if isinstance(tok, int):
        return pl.ds(tok * ROW_TILES, ROW_TILES)
    return pl.ds(pl.multiple_of(tok * ROW_TILES, ROW_TILES), ROW_TILES)


def _dispatch_kernel(offs_ref, cnt_ref, nused_ref, cls_ref, rank_ref, h_ref, dst0_hbm, xs_hbm, dst_hbm,
                     dst_ref, zbuf, row_sem, map_sem, zero_sem, *, n_tiles):
    step = pl.program_id(0)
    pad_bits = [1 << b for b in reversed(range(TM.bit_length() - 1))]

    def class_padding(c):
        cnt = cnt_ref[c]
        return offs_ref[c] + cnt, (TM - cnt % TM) % TM

    def zero_rows(first_row, n_rows):
        start = first_row * ROW_TILES
        if not isinstance(start, int):
            start = pl.multiple_of(start, ROW_TILES)
        return pltpu.make_async_copy(zbuf.at[pl.ds(0, n_rows * ROW_TILES), :],
                                     xs_hbm.at[pl.ds(start, n_rows * ROW_TILES), :], zero_sem)

    @pl.when(step == 0)
    def _():
        load = pltpu.make_async_copy(dst0_hbm, dst_ref, map_sem)
        load.start()
        load.wait()

        zbuf[...] = jnp.zeros_like(zbuf)

        def fill_class(c, _):
            first_row, n_pad = class_padding(c)
            for bit in pad_bits:
                @pl.when((n_pad & bit) != 0)
                def _(bit=bit):
                    zero_rows(first_row + (n_pad & ~(2 * bit - 1)), bit).start()
            return 0

        lax.fori_loop(0, N_CLASSES, fill_class, 0)

        def fill_tile(j, _):
            zero_rows(j * TM, TM).start()
            return 0

        lax.fori_loop(nused_ref[0], n_tiles, fill_tile, 0)

    def send(g, _):
        for k in range(DISPATCH_UNROLL):
            r = g * DISPATCH_UNROLL + k
            p = offs_ref[cls_ref[r]] + rank_ref[r]
            dst_ref[p] = step * TB_DISPATCH + r
            pltpu.make_async_copy(h_ref.at[_token_rows(r), :], xs_hbm.at[_token_rows(p), :],
                                  row_sem).start(priority=k % 2)
        return 0

    lax.fori_loop(0, TB_DISPATCH // DISPATCH_UNROLL, send, 0)

    pltpu.make_async_copy(h_ref, xs_hbm.at[pl.ds(0, TB_DISPATCH * ROW_TILES), :], row_sem).wait()

    @pl.when(step == pl.num_programs(0) - 1)
    def _():
        store = pltpu.make_async_copy(dst_ref, dst_hbm, map_sem)
        store.start()
        store.wait()

        def drain_class(c, _):
            _, n_pad = class_padding(c)

            @pl.when(n_pad > 0)
            def _():
                zero_rows(0, n_pad).wait()
            return 0

        lax.fori_loop(0, N_CLASSES, drain_class, 0)

        def drain_tile(j, _):
            zero_rows(0, TM).wait()
            return 0

        lax.fori_loop(nused_ref[0], n_tiles, drain_tile, 0)


def _dispatch(h, offs, cnt, n_used, cls, rank, n_rows):
    t = cls.shape[0]
    dst0 = jnp.zeros((n_rows,), jnp.int32)
    smem_blk = pl.BlockSpec((TB_DISPATCH,), lambda i, *_: (i,), memory_space=pltpu.SMEM)
    hbm = pl.BlockSpec(memory_space=pl.ANY)
    return pl.pallas_call(
        functools.partial(_dispatch_kernel, n_tiles=n_rows // TM),
        out_shape=(jax.ShapeDtypeStruct((n_rows * ROW_TILES, LANES), F32),
                   jax.ShapeDtypeStruct((n_rows,), jnp.int32)),
        grid_spec=pltpu.PrefetchScalarGridSpec(
            num_scalar_prefetch=3,
            grid=(t // TB_DISPATCH,),
            in_specs=[smem_blk, smem_blk,
                      pl.BlockSpec((TB_DISPATCH * ROW_TILES, LANES), lambda i, *_: (i, 0)),
                      hbm],
            out_specs=(hbm, hbm),
            scratch_shapes=[pltpu.SMEM((n_rows,), jnp.int32),
                            pltpu.VMEM((TM * ROW_TILES, LANES), F32),
                            pltpu.SemaphoreType.DMA,
                            pltpu.SemaphoreType.DMA,
                            pltpu.SemaphoreType.DMA]),
        compiler_params=pltpu.CompilerParams(dimension_semantics=("arbitrary",)),
        name="moe_dispatch",
    )(offs, cnt, n_used, cls, rank, h, dst0)


def _expert_kernel(ea_ref, eb_ref, nvalid_ref, nused_ref, *refs):
    k_tiles = TILES_PER_STEP
    dst_refs, x_refs = refs[:k_tiles], refs[k_tiles:2 * k_tiles]
    wr_ref, br_ref = refs[2 * k_tiles:2 * k_tiles + 2]
    w_refs = refs[2 * k_tiles + 2:6 * k_tiles + 2]
    lng_ref, lnb_ref, out_hbm, ybuf, ssem = refs[6 * k_tiles + 2:]
    step = pl.program_id(0)
    n_used = nused_ref[0]
    first_tile = step * k_tiles
    slot = step % 2
    other = 1 - slot

    def step_rows(s):
        base = jnp.maximum(s, 0) * k_tiles
        return sum(nvalid_ref[base + k] for k in range(k_tiles))

    def start_rows(k, n, s):
        for r in range(TM):
            @pl.when(r < n)
            def _(r=r):
                pltpu.make_async_copy(ybuf.at[s, _token_rows(k * TM + r), :],
                                      out_hbm.at[_token_rows(dst_refs[k][r]), :],
                                      ssem.at[s]).start(priority=r % 2)

    def wait_rows(n, s):
        @pl.when(n > 0)
        def _():
            rows = pl.ds(0, n * ROW_TILES)
            pltpu.make_async_copy(ybuf.at[s, rows, :], out_hbm.at[rows, :], ssem.at[s]).wait()

    def tile_result(k):
        x_ref = x_refs[k]
        wgua_ref, wda_ref, wgub_ref, wdb_ref = w_refs[4 * k:4 * k + 4]
        xb = _load_token_major(x_ref, TM).astype(BF16)

        logits = _dot(xb, wr_ref[...]) + br_ref[...]
        lane = lax.broadcasted_iota(jnp.int32, logits.shape, 1)
        ea, eb = ea_ref[first_tile + k], eb_ref[first_tile + k]
        group = ea // EXPERTS_PER_GROUP
        gl = jnp.where(lane < N_GROUPS, logits, F32(-jnp.inf))
        gmax = jnp.max(gl, axis=-1, keepdims=True)
        gsum = jnp.sum(jnp.where(lane < N_GROUPS, jnp.exp(gl - gmax), 0.0), axis=-1, keepdims=True)
        lg = jnp.sum(jnp.where(lane == group, logits, 0.0), axis=-1, keepdims=True)
        g_top_p = jnp.exp(lg - gmax) / gsum
        la = jnp.sum(jnp.where(lane == ea + N_GROUPS, logits, 0.0), axis=-1, keepdims=True)
        lb = jnp.sum(jnp.where(lane == eb + N_GROUPS, logits, 0.0), axis=-1, keepdims=True)
        w_a = g_top_p / (1.0 + jnp.exp(lb - la))
        w_b = g_top_p / (1.0 + jnp.exp(la - lb))

        def hidden(wgu_ref, weight):
            gu = _dot(xb, wgu_ref[0, 0])
            gate, up = gu[:, :D_EXPERT], gu[:, D_EXPERT:]
            return ((gate * _sigmoid(gate)) * up * weight).astype(BF16)

        ffn = _dot(hidden(wgua_ref, w_a), wda_ref[0, 0]) + _dot(hidden(wgub_ref, w_b), wdb_ref[0, 0])
        return _layer_norm_rows(ALPHA * _load_token_major(x_ref, TM) + ffn, lng_ref[...], lnb_ref[...])

    @pl.when(first_tile < n_used)
    def _():
        results = [tile_result(k) for k in range(k_tiles)]

        @pl.when(step >= 2)
        def _():
            wait_rows(step_rows(step - 2), slot)

        for k in range(k_tiles):
            for c in range(ROW_TILES):
                ybuf[slot, pl.ds(k * TM * ROW_TILES + c, TM, stride=ROW_TILES), :] = (
                    results[k][:, c * LANES:(c + 1) * LANES])
            start_rows(k, nvalid_ref[first_tile + k], slot)

        @pl.when(first_tile + k_tiles >= n_used)
        def _():
            @pl.when(step >= 1)
            def _():
                wait_rows(step_rows(step - 1), other)

            wait_rows(step_rows(step), slot)


def _experts(xs, t, dst, tile_ea, tile_eb, tile_nvalid, n_used, w_router, b_router, wgu, wd, layer, ln_g, ln_b):
    k_tiles = TILES_PER_STEP
    n_tiles = dst.shape[0] // TM
    assert n_tiles % k_tiles == 0
    const = lambda shape: pl.BlockSpec(shape, lambda i, ea, eb, nv, nu: (0,) * len(shape))
    dst_spec = lambda k: pl.BlockSpec((TM,), lambda i, ea, eb, nv, nu: (i * k_tiles + k,),
                                      memory_space=pltpu.SMEM)
    x_spec = lambda k: pl.BlockSpec((TM * ROW_TILES, LANES), lambda i, ea, eb, nv, nu: (i * k_tiles + k, 0))
    wgu_spec = lambda k, which: pl.BlockSpec(
        (1, 1, D_MODEL, 2 * D_EXPERT),
        lambda i, ea, eb, nv, nu: (layer, (ea, eb)[which][i * k_tiles + k], 0, 0))
    wd_spec = lambda k, which: pl.BlockSpec(
        (1, 1, D_EXPERT, D_MODEL),
        lambda i, ea, eb, nv, nu: (layer, (ea, eb)[which][i * k_tiles + k], 0, 0))
    tiles = range(k_tiles)
    weight_specs, weight_args = [], []
    for k in tiles:
        weight_specs += [wgu_spec(k, 0), wd_spec(k, 0), wgu_spec(k, 1), wd_spec(k, 1)]
        weight_args += [wgu, wd, wgu, wd]
    grid_spec = pltpu.PrefetchScalarGridSpec(
        num_scalar_prefetch=4,
        grid=(n_tiles // k_tiles,),
        in_specs=([dst_spec(k) for k in tiles] + [x_spec(k) for k in tiles]
                  + [const((D_MODEL, LANES)), const((1, LANES))] + weight_specs
                  + [const((1, D_MODEL)), const((1, D_MODEL))]),
        out_specs=pl.BlockSpec(memory_space=pl.ANY),
        scratch_shapes=[pltpu.VMEM((2, k_tiles * TM * ROW_TILES, LANES), F32),
                        pltpu.SemaphoreType.DMA((2,))],
    )
    return pl.pallas_call(
        _expert_kernel,
        out_shape=jax.ShapeDtypeStruct((t * ROW_TILES, LANES), F32),
        grid_spec=grid_spec,
        compiler_params=pltpu.CompilerParams(dimension_semantics=("arbitrary",),
                                             vmem_limit_bytes=VMEM_LIMIT_BYTES),
        name="moe_experts",
    )(tile_ea, tile_eb, tile_nvalid, n_used, *([dst] * k_tiles), *([xs] * k_tiles),
      w_router, b_router, *weight_args, ln_g, ln_b)


def _class_expert_tables():
    first, second = [], []
    for g in range(N_GROUPS):
        for lo in range(EXPERTS_PER_GROUP):
            for hi in range(lo + 1, EXPERTS_PER_GROUP):
                first.append(g * EXPERTS_PER_GROUP + lo)
                second.append(g * EXPERTS_PER_GROUP + hi)
    return jnp.array(first, jnp.int32), jnp.array(second, jnp.int32)


def _moe_layer(h, t, wg, bg, we, be, wgu, wd, layer, ln_g, ln_b):
    n_tiles = t // TM + N_CLASSES
    n_rows = n_tiles * TM

    w_router = jnp.zeros((D_MODEL, LANES), F32)
    w_router = w_router.at[:, :N_GROUPS].set(wg).at[:, N_GROUPS:N_GROUPS + N_EXPERTS].set(we)
    b_router = jnp.zeros((1, LANES), F32)
    b_router = b_router.at[0, :N_GROUPS].set(bg).at[0, N_GROUPS:N_GROUPS + N_EXPERTS].set(be)
    w_hi = w_router.astype(BF16)
    w_lo = (w_router - w_hi.astype(F32)).astype(BF16)

    meta, counts = _router(h, t, w_hi, w_lo, b_router)

    cnt = counts[0]
    tiles_per = (cnt + TM - 1) // TM
    tile_end = jnp.cumsum(tiles_per)
    offs = (tile_end - tiles_per) * TM
    n_used = tile_end[-1]
    tile_id = jnp.arange(n_tiles, dtype=jnp.int32)
    tile_cls = jnp.sum((jnp.minimum(tile_id, n_used - 1)[:, None] >= tile_end[None, :N_CLASSES])
                       .astype(jnp.int32), axis=1)
    seg_tile = tile_id - (tile_end - tiles_per)[tile_cls]
    tile_nvalid = jnp.where(tile_id < n_used, jnp.clip(cnt[tile_cls] - seg_tile * TM, 0, TM), 0)
    first, second = _class_expert_tables()
    xs, dst = _dispatch(h, offs, cnt, n_used.reshape(1), meta[0], meta[1], n_rows)
    return _experts(xs, t, dst, first[tile_cls], second[tile_cls], tile_nvalid, n_used.reshape(1),
                    w_hi, b_router, wgu, wd, layer, ln_g, ln_b)


def kernel(x, positions, w_in, sc_conv_w, sc_conv_b, lru_conv_w, lru_conv_b, lru_w_r, lru_b_r,
           lru_w_i, lru_b_i, lru_lambda, sg_norm_g, sg_w_s, sg_b_s, ret_norm_g, branch_proj, w_out,
           ln_mix_g, ln_mix_b, router_group_w, router_group_b, router_expert_w, router_expert_b,
           exp_w_gate, exp_w_up, exp_w_down, ln_ffn_g, ln_ffn_b):
    batch, seq, d = x.shape
    assert d == D_MODEL and seq % TS == 0 and w_in.shape[-1] == N_IN
    depth = w_in.shape[0]
    t = batch * seq
    cos_t, sin_t = _rope_tables(positions)
    dec_all, qd_tab, kd_tab, cd_tab = _retention_tables()
    avg = _block_diag(jnp.full((N_HEADS, HEAD_DIM, HEAD_DIM), 1.0 / HEAD_DIM, F32)).astype(BF16)

    w_in_b, bp_b, wo_b = w_in.astype(BF16), branch_proj.astype(BF16), w_out.astype(BF16)
    wgu_b = jnp.concatenate([exp_w_gate, exp_w_up], axis=-1).astype(BF16)
    wd_b = exp_w_down.astype(BF16)

    h = x.reshape(t, d)
    for l in range(depth):
        w_ri = jnp.concatenate([_block_diag(lru_w_r[l]), _block_diag(lru_w_i[l])], axis=1).astype(BF16)
        b_ri = jnp.concatenate([lru_b_r[l], lru_b_i[l]])[None, :]
        bs_tab = jnp.repeat(sg_b_s[l].T, HEAD_DIM, axis=1)
        per_call = lambda *arrs: [(a, None) for a in arrs]
        consts = (
            [(w_in_b, l)]
            + per_call(sc_conv_w[l], sc_conv_b[l][None, :], lru_conv_w[l], lru_conv_b[l][None, :],
                       w_ri, b_ri, lru_lambda[l][None, :], sg_norm_g[l][None, :], sg_w_s[l], bs_tab,
                       ret_norm_g[l][None, :], dec_all, qd_tab, kd_tab, cd_tab, avg)
            + [(bp_b, l), (wo_b, l)]
            + per_call(ln_mix_g[l][None, :], ln_mix_b[l][None, :]))
        h = _mixer_layer(h, cos_t, sin_t, consts, batch, seq, token_major_in=l > 0)
        h = _moe_layer(h, t, router_group_w[l], router_group_b[l], router_expert_w[l], router_expert_b[l],
                       wgu_b, wd_b, l, ln_ffn_g[l][None, :], ln_ffn_b[l][None, :])
    return h.reshape(batch, seq, d)
```

```python
import functools

import jax
import jax.numpy as jnp
import numpy as np
from jax import lax
from jax.experimental import pallas as pl
from jax.experimental.pallas import tpu as pltpu

F32 = jnp.float32
BF16 = jnp.bfloat16

LANES = 128
SUBLANES = 8
VMEM_LIMIT_BYTES = 56 * 1024 * 1024

D_MODEL = 1024
ROW_TILES = D_MODEL // LANES
assert ROW_TILES == SUBLANES
WIDTH = D_MODEL // 4
N_HEADS = 4
N_BRANCH = 4
HEAD_DIM = WIDTH // N_HEADS
CHUNK = 128
SC_K = 3
LRU_K = 4
LRU_C = 8.0
ROPE_BASE = 10000.0
N_GROUPS = 4
EXPERTS_PER_GROUP = 8
N_EXPERTS = N_GROUPS * EXPERTS_PER_GROUP
D_EXPERT = D_MODEL // 4
LN_EPS = 1e-5
DEPTH = 2
ALPHA = (2.0 * DEPTH) ** 0.25

COL_SC_B, COL_SC_C, COL_SC_X, COL_LRU, COL_SG_U, COL_SG_V, COL_Q, COL_K, COL_V, COL_G = (
    i * WIDTH for i in range(10))
COL_GATES = 10 * WIDTH
N_IN = COL_GATES + 4 * D_MODEL

TS = 512
GATE_CHUNK = 256
SCAN_PAD = TS // 2
CONV_PAD = SUBLANES
TB_DISPATCH = 512
DISPATCH_UNROLL = 8
TM = 128
TILES_PER_STEP = 4
PAIRS_PER_GROUP = EXPERTS_PER_GROUP * (EXPERTS_PER_GROUP - 1) // 2
N_CLASSES = N_GROUPS * PAIRS_PER_GROUP
assert N_CLASSES <= LANES


def _sigmoid(x):
    return 0.5 * jnp.tanh(0.5 * x) + 0.5


def _dot(a, b):
    return jnp.dot(a, b, preferred_element_type=F32)


def _load_token_major(ref, rows, lead=()):
    return jnp.concatenate(
        [ref[lead + (pl.ds(c, rows, stride=ROW_TILES), slice(None))] for c in range(ROW_TILES)], axis=1)


def _store_token_major(ref, y, rows, lead=()):
    for c in range(ROW_TILES):
        ref[lead + (pl.ds(c, rows, stride=ROW_TILES), slice(None))] = y[:, c * LANES:(c + 1) * LANES]


def _layer_norm_rows(y, g, b):
    mu = jnp.mean(y, axis=-1, keepdims=True)
    yc = y - mu
    var = jnp.mean(yc * yc, axis=-1, keepdims=True)
    return yc * lax.rsqrt(var + LN_EPS) * g + b


def _rope_table_kernel(pos_ref, freq_ref, cos_ref, sin_ref):
    ang = pos_ref[...].astype(F32) * freq_ref[...]
    lane = lax.broadcasted_iota(jnp.int32, ang.shape, 1)
    first_half = (lane % HEAD_DIM) < (HEAD_DIM // 2)
    cos_ref[...] = jnp.cos(ang)
    s = jnp.sin(ang)
    sin_ref[...] = jnp.where(first_half, -s, s)


def _rope_tables(positions):
    t = positions.size
    half = HEAD_DIM // 2
    inv_freq = ROPE_BASE ** (-jnp.arange(half, dtype=F32) / half)
    freq_row = jnp.tile(inv_freq, LANES // half)[None, :]
    rows = 1024
    return pl.pallas_call(
        _rope_table_kernel,
        out_shape=(jax.ShapeDtypeStruct((t, LANES), F32), jax.ShapeDtypeStruct((t, LANES), F32)),
        grid=(t // rows,),
        in_specs=[pl.BlockSpec((rows, 1), lambda i: (i, 0)),
                  pl.BlockSpec((1, LANES), lambda i: (0, 0))],
        out_specs=(pl.BlockSpec((rows, LANES), lambda i: (i, 0)),
                   pl.BlockSpec((rows, LANES), lambda i: (i, 0))),
        name="rope_tables",
    )(positions.reshape(t, 1), freq_row)


def _mixer_kernel(h_ref, cos_ref, sin_ref, w_in_ref, scw_ref, scb_ref, lcw_ref, lcb_ref,
                  wri_ref, bri_ref, lam_ref, sgg_ref, ws_ref, bs_ref, rg_ref,
                  dec_ref, qd_ref, kd_ref, cd_ref, avg_ref, bp_ref, wo_ref, lng_ref, lnb_ref,
                  rwhi_ref, rwlo_ref, rb_ref,
                  out_ref, meta_ref, cnt_ref,
                  ubuf, xbuf, hcar, sbd, abuf, hbuf, gbuf, rcount, tri, *, token_major_in):
    step = pl.program_id(1)

    @pl.when((pl.program_id(0) == 0) & (step == 0))
    def _():
        rcount[...] = jnp.zeros_like(rcount)
        r_i = lax.broadcasted_iota(jnp.int32, (TS, TS), 0)
        c_i = lax.broadcasted_iota(jnp.int32, (TS, TS), 1)
        tri[...] = jnp.where(r_i > c_i, 1.0, 0.0).astype(BF16)

    @pl.when(step == 0)
    def _():
        ubuf[0:CONV_PAD, :] = jnp.zeros((CONV_PAD, WIDTH), F32)
        xbuf[0:CONV_PAD, :] = jnp.zeros((CONV_PAD, WIDTH), F32)
        hcar[...] = jnp.zeros_like(hcar)
        sbd[...] = jnp.zeros_like(sbd)
        abuf[0:SCAN_PAD, :] = jnp.ones((SCAN_PAD, WIDTH), F32)
        hbuf[0:SCAN_PAD, :] = jnp.zeros((SCAN_PAD, WIDTH), F32)

    h = _load_token_major(h_ref, TS) if token_major_in else h_ref[...]
    hb = h.astype(BF16)

    def proj(col, width=WIDTH):
        return _dot(hb, w_in_ref[0, :, col:col + width])

    pending_gate_cols = list(range(0, N_BRANCH * D_MODEL, GATE_CHUNK))

    def emit_gates(n_chunks=1):
        for _ in range(n_chunks):
            if pending_gate_cols:
                col = pending_gate_cols.pop(0)
                gbuf[:, col:col + GATE_CHUNK] = _sigmoid(proj(COL_GATES + col, GATE_CHUNK))

    lane = lax.broadcasted_iota(jnp.int32, (CHUNK, WIDTH), 1)
    head_of_lane = lane // HEAD_DIM
    avg = avg_ref[...]

    def group_standardize(x):
        mean = _dot(x.astype(BF16), avg)
        xc = x - mean
        var = _dot((xc * xc).astype(BF16), avg)
        return xc * lax.rsqrt(var + LN_EPS)

    sc_b, sc_c, sc_x = proj(COL_SC_B), proj(COL_SC_C), proj(COL_SC_X)
    u = sc_c * sc_x
    ubuf[CONV_PAD:CONV_PAD + TS, :] = u
    conv = scw_ref[2:3, :] * u + scb_ref[...]
    for j in range(SC_K - 1):
        back = SC_K - 1 - j
        conv = conv + scw_ref[j:j + 1, :] * ubuf[CONV_PAD - back:CONV_PAD - back + TS, :]
    ubuf[0:CONV_PAD, :] = ubuf[TS:TS + CONV_PAD, :]
    branch_a = sc_b * conv
    emit_gates()

    lx = proj(COL_LRU)
    xbuf[CONV_PAD:CONV_PAD + TS, :] = lx
    xc = lcw_ref[LRU_K - 1:LRU_K, :] * lx + lcb_ref[...]
    for j in range(LRU_K - 1):
        back = LRU_K - 1 - j
        xc = xc + lcw_ref[j:j + 1, :] * xbuf[CONV_PAD - back:CONV_PAD - back + TS, :]
    xbuf[0:CONV_PAD, :] = xbuf[TS:TS + CONV_PAD, :]
    emit_gates()
    ri = _dot(xc.astype(BF16), wri_ref[...]) + bri_ref[...]
    r = _sigmoid(ri[:, :WIDTH])
    ig = _sigmoid(ri[:, WIDTH:])
    neg_lam = -lam_ref[...]
    softplus = jnp.maximum(neg_lam, 0.0) + jnp.log1p(jnp.exp(-jnp.abs(neg_lam)))
    log_a = (-LRU_C) * r * softplus
    a = jnp.exp(log_a)
    th = jnp.tanh(log_a)
    uu = jnp.sqrt((-2.0) * th / (1.0 - th)) * (ig * xc)
    row = lax.broadcasted_iota(jnp.int32, (TS, WIDTH), 0)
    uu = uu + jnp.where(row == 0, a * hcar[...], 0.0)
    abuf[SCAN_PAD:SCAN_PAD + TS, :] = a
    hbuf[SCAN_PAD:SCAN_PAD + TS, :] = uu
    shift = 1
    while shift < TS:
        a_cur = abuf[SCAN_PAD:SCAN_PAD + TS, :]
        h_prev = hbuf[SCAN_PAD - shift:SCAN_PAD - shift + TS, :]
        h_new = a_cur * h_prev + hbuf[SCAN_PAD:SCAN_PAD + TS, :]
        if shift * 2 < TS:
            a_new = a_cur * abuf[SCAN_PAD - shift:SCAN_PAD - shift + TS, :]
            abuf[SCAN_PAD:SCAN_PAD + TS, :] = a_new
        hbuf[SCAN_PAD:SCAN_PAD + TS, :] = h_new
        emit_gates()
        shift *= 2
    branch_b = hbuf[SCAN_PAD:SCAN_PAD + TS, :]
    hcar[...] = hbuf[SCAN_PAD + TS - 1:SCAN_PAD + TS, :]

    gu = jax.nn.gelu(proj(COL_SG_U))
    gv = jax.nn.gelu(proj(COL_SG_V))
    emit_gates()
    vn = (group_standardize(gv) * sgg_ref[...]).astype(BF16)
    trow = lax.broadcasted_iota(jnp.int32, (CHUNK, CHUNK), 0)
    tcol = lax.broadcasted_iota(jnp.int32, (CHUNK, CHUNK), 1)
    w_causal = [jnp.where(trow >= tcol, ws_ref[g], 0.0).astype(BF16) for g in range(N_HEADS)]
    sv_chunks = []
    for c in range(TS // CHUNK):
        vch = vn[c * CHUNK:(c + 1) * CHUNK, :]
        sv = bs_ref[...]
        for g in range(N_HEADS):
            sv = sv + jnp.where(head_of_lane == g, _dot(w_causal[g], vch), 0.0)
        sv_chunks.append(sv)
    branch_c = gu * jnp.concatenate(sv_chunks, axis=0)
    emit_gates()

    cos = jnp.concatenate([cos_ref[...]] * (WIDTH // LANES), axis=1)
    sin = jnp.concatenate([sin_ref[...]] * (WIDTH // LANES), axis=1)
    lane_ts = lax.broadcasted_iota(jnp.int32, (TS, WIDTH), 1)
    first_half = (lane_ts % HEAD_DIM) < (HEAD_DIM // 2)

    def rope(t):
        swapped = jnp.where(first_half,
                            pltpu.roll(t, WIDTH - HEAD_DIM // 2, 1),
                            pltpu.roll(t, HEAD_DIM // 2, 1))
        return t * cos + swapped * sin

    q = rope(proj(COL_Q))
    k = rope(proj(COL_K)) * (HEAD_DIM ** -0.5)
    emit_gates()
    v = proj(COL_V)
    zg = proj(COL_G)
    bd_mask = (lax.broadcasted_iota(jnp.int32, (WIDTH, WIDTH), 0) // HEAD_DIM
               == lax.broadcasted_iota(jnp.int32, (WIDTH, WIDTH), 1) // HEAD_DIM)
    o_chunks = []
    for c in range(TS // CHUNK):
        sl = slice(c * CHUNK, (c + 1) * CHUNK)
        qc, kc, vc = q[sl, :], k[sl, :], v[sl, :]
        qcb = qc.astype(BF16)
        kstack = jnp.concatenate(
            [jnp.where(head_of_lane == hh, kc, 0.0) for hh in range(N_HEADS)], axis=0).astype(BF16)
        vstack = jnp.concatenate(
            [jnp.where(head_of_lane == hh, vc, 0.0) for hh in range(N_HEADS)], axis=0).astype(BF16)
        scores = lax.dot_general(qcb, kstack, (((1,), (1,)), ((), ())),
                                 preferred_element_type=F32)
        inner = _dot((scores * dec_ref[...]).astype(BF16), vstack)
        state = sbd[...]
        cross = _dot(qcb, state.astype(BF16)) * qd_ref[...]
        o_chunks.append(inner + cross)
        kdec = (kc * kd_ref[...]).astype(BF16)
        kv = lax.dot_general(kdec, vc.astype(BF16), (((0,), (0,)), ((), ())),
                             preferred_element_type=F32)
        sbd[...] = cd_ref[...] * state + jnp.where(bd_mask, kv, 0.0)
        emit_gates()
    o = jnp.concatenate(o_chunks, axis=0)
    branch_d = (zg * _sigmoid(zg)) * (group_standardize(o) * rg_ref[...])

    emit_gates(len(pending_gate_cols))
    merged = jnp.zeros((TS, D_MODEL), F32)
    for b_idx, branch in enumerate((branch_a, branch_b, branch_c, branch_d)):
        gate = gbuf[:, b_idx * D_MODEL:(b_idx + 1) * D_MODEL]
        merged = merged + gate * _dot(branch.astype(BF16), bp_ref[0, b_idx])
    mix = _dot(merged.astype(BF16), wo_ref[0])
    h_mid = _layer_norm_rows(ALPHA * h + mix, lng_ref[...], lnb_ref[...])
    _store_token_major(out_ref, h_mid, TS)
    _route(h_mid, rwhi_ref, rwlo_ref, rb_ref, tri, rcount, meta_ref, cnt_ref)


def _const_spec(arr, layer):
    if layer is None:
        return pl.BlockSpec(arr.shape, lambda b, i, _nd=arr.ndim: (0,) * _nd, pipeline_mode=pl.Buffered(1))
    return pl.BlockSpec((1,) + arr.shape[1:], lambda b, i, _nd=arr.ndim: (layer,) + (0,) * (_nd - 1),
                        pipeline_mode=pl.Buffered(1))


def _mixer_layer(h, cos_t, sin_t, consts, batch, seq, token_major_in):
    t = batch * seq
    steps = seq // TS
    row_map = lambda b, i: (b * steps + i, 0)
    tm_spec = pl.BlockSpec((TS * ROW_TILES, LANES), row_map)
    in_specs = [tm_spec if token_major_in else pl.BlockSpec((TS, D_MODEL), row_map),
                pl.BlockSpec((TS, LANES), row_map),
                pl.BlockSpec((TS, LANES), row_map)]
    in_specs += [_const_spec(arr, layer) for arr, layer in consts]
    return pl.pallas_call(
        functools.partial(_mixer_kernel, token_major_in=token_major_in),
        out_shape=(jax.ShapeDtypeStruct((t * ROW_TILES, LANES), F32),
                   jax.ShapeDtypeStruct((SUBLANES, t), jnp.int32),
                   jax.ShapeDtypeStruct((SUBLANES, LANES), jnp.int32)),
        grid=(batch, steps),
        in_specs=in_specs,
        out_specs=(tm_spec,
                   pl.BlockSpec((SUBLANES, TS), lambda b, i: (0, b * steps + i)),
                   pl.BlockSpec((SUBLANES, LANES), lambda b, i: (0, 0))),
        scratch_shapes=[
            pltpu.VMEM((CONV_PAD + TS, WIDTH), F32),
            pltpu.VMEM((CONV_PAD + TS, WIDTH), F32),
            pltpu.VMEM((1, WIDTH), F32),
            pltpu.VMEM((WIDTH, WIDTH), F32),
            pltpu.VMEM((SCAN_PAD + TS, WIDTH), F32),
            pltpu.VMEM((SCAN_PAD + TS, WIDTH), F32),
            pltpu.VMEM((TS, N_BRANCH * D_MODEL), F32),
            pltpu.VMEM((1, LANES), F32),
            pltpu.VMEM((TS, TS), BF16),
        ],
        compiler_params=pltpu.CompilerParams(
            dimension_semantics=("arbitrary", "arbitrary"),
            vmem_limit_bytes=VMEM_LIMIT_BYTES),
        name="mixer_layer",
    )(h, cos_t, sin_t, *[arr for arr, _ in consts])


def _block_diag(w):
    heads, d, e = w.shape
    eye = jnp.eye(heads, dtype=w.dtype)
    return (eye[:, None, :, None] * w[:, :, None, :]).reshape(heads * d, heads * e)


def _retention_tables():
    f32 = np.float32
    log_gamma = np.log1p(-np.exp2(f32(-5.0) - np.arange(N_HEADS, dtype=f32))).astype(f32)
    pos = np.arange(CHUNK, dtype=f32)
    diff = pos[:, None] - pos[None, :]
    decay = np.where(diff >= 0, np.exp(np.maximum(diff, f32(0)) * log_gamma[:, None, None]), f32(0)).astype(f32)
    dec_all = np.transpose(decay, (1, 0, 2)).reshape(CHUNK, N_HEADS * CHUNK)
    q_decay = np.exp((pos + f32(1))[:, None] * log_gamma).astype(f32)
    k_decay = np.exp((f32(CHUNK - 1) - pos)[:, None] * log_gamma).astype(f32)
    qd_tab = np.repeat(q_decay, HEAD_DIM, axis=1)
    kd_tab = np.repeat(k_decay, HEAD_DIM, axis=1)
    chunk_decay = np.repeat(np.exp(f32(CHUNK) * log_gamma).astype(f32), HEAD_DIM)
    cd_tab = np.ascontiguousarray(np.broadcast_to(chunk_decay[:, None], (WIDTH, WIDTH)))
    return tuple(jnp.asarray(a, F32) for a in (dec_all, qd_tab, kd_tab, cd_tab))


def _route(h, whi_ref, wlo_ref, b_ref, tri_ref, count_ref, meta_ref, cnt_ref):
    h_hi = h.astype(BF16)
    h_lo = (h - h_hi.astype(F32)).astype(BF16)
    whi = whi_ref[...]
    logits = _dot(h_hi, whi) + _dot(h_lo, whi) + _dot(h_hi, wlo_ref[...]) + b_ref[...]
    lane = lax.broadcasted_iota(jnp.int32, logits.shape, 1)
    big = jnp.int32(2 ** 30)
    neg_inf = F32(-jnp.inf)

    gl = jnp.where(lane < N_GROUPS, logits, neg_inf)
    gmax = jnp.max(gl, axis=-1, keepdims=True)
    g_idx = jnp.min(jnp.where(gl == gmax, lane, big), axis=-1, keepdims=True)

    e_lane = lane - N_GROUPS
    in_group = (e_lane >= g_idx * EXPERTS_PER_GROUP) & (e_lane < (g_idx + 1) * EXPERTS_PER_GROUP)
    el = jnp.where(in_group, logits, neg_inf)
    m1 = jnp.max(el, axis=-1, keepdims=True)
    i1 = jnp.min(jnp.where(el == m1, lane, big), axis=-1, keepdims=True)
    el2 = jnp.where(lane == i1, neg_inf, el)
    m2 = jnp.max(el2, axis=-1, keepdims=True)
    i2 = jnp.min(jnp.where(el2 == m2, lane, big), axis=-1, keepdims=True)
    base = (g_idx * EXPERTS_PER_GROUP + N_GROUPS).astype(F32)
    lo = jnp.minimum(i1, i2).astype(F32) - base
    hi = jnp.maximum(i1, i2).astype(F32) - base
    pair = lo * (2.0 * EXPERTS_PER_GROUP - 1.0 - lo) * 0.5 + (hi - lo - 1.0)
    cls = (g_idx.astype(F32) * PAIRS_PER_GROUP + pair).astype(jnp.int32)

    onehot = jnp.where(lane == cls, 1.0, 0.0)
    before = _dot(tri_ref[...], onehot.astype(BF16)) + count_ref[...]
    rank = jnp.sum(jnp.where(lane == cls, before, 0.0), axis=-1, keepdims=True)
    count_ref[...] = count_ref[...] + jnp.sum(onehot, axis=0, keepdims=True)

    meta = jnp.where(lane == 0, cls.astype(F32), 0.0) + jnp.where(lane == 1, rank, 0.0)
    meta_ref[...] = jnp.transpose(meta)[0:SUBLANES, :].astype(jnp.int32)
    cnt_ref[...] = jnp.broadcast_to(count_ref[...], cnt_ref.shape).astype(jnp.int32)


def _token_rows(tok):
    if isinstance(tok, int):
        return pl.ds(tok * ROW_TILES, ROW_TILES)
    return pl.ds(pl.multiple_of(tok * ROW_TILES, ROW_TILES), ROW_TILES)


def _dispatch_kernel(offs_ref, cnt_ref, nused_ref, cls_ref, rank_ref, h_ref, dst0_hbm, xs_hbm, dst_hbm,
                     dst_ref, zbuf, row_sem, map_sem, zero_sem, *, n_tiles):
    step = pl.program_id(0)
    pad_bits = [1 << b for b in reversed(range(TM.bit_length() - 1))]

    def class_padding(c):
        cnt = cnt_ref[c]
        return offs_ref[c] + cnt, (TM - cnt % TM) % TM

    def zero_rows(first_row, n_rows):
        start = first_row * ROW_TILES
        if not isinstance(start, int):
            start = pl.multiple_of(start, ROW_TILES)
        return pltpu.make_async_copy(zbuf.at[pl.ds(0, n_rows * ROW_TILES), :],
                                     xs_hbm.at[pl.ds(start, n_rows * ROW_TILES), :], zero_sem)

    @pl.when(step == 0)
    def _():
        load = pltpu.make_async_copy(dst0_hbm, dst_ref, map_sem)
        load.start()
        load.wait()

        zbuf[...] = jnp.zeros_like(zbuf)

        def fill_class(c, _):
            first_row, n_pad = class_padding(c)
            for bit in pad_bits:
                @pl.when((n_pad & bit) != 0)
                def _(bit=bit):
                    zero_rows(first_row + (n_pad & ~(2 * bit - 1)), bit).start()
            return 0

        lax.fori_loop(0, N_CLASSES, fill_class, 0)

        def fill_tile(j, _):
            zero_rows(j * TM, TM).start()
            return 0

        lax.fori_loop(nused_ref[0], n_tiles, fill_tile, 0)

    def send(g, _):
        for k in range(DISPATCH_UNROLL):
            r = g * DISPATCH_UNROLL + k
            p = offs_ref[cls_ref[r]] + rank_ref[r]
            dst_ref[p] = step * TB_DISPATCH + r
            pltpu.make_async_copy(h_ref.at[_token_rows(r), :], xs_hbm.at[_token_rows(p), :],
                                  row_sem).start(priority=k % 2)
        return 0

    lax.fori_loop(0, TB_DISPATCH // DISPATCH_UNROLL, send, 0)

    pltpu.make_async_copy(h_ref, xs_hbm.at[pl.ds(0, TB_DISPATCH * ROW_TILES), :], row_sem).wait()

    @pl.when(step == pl.num_programs(0) - 1)
    def _():
        store = pltpu.make_async_copy(dst_ref, dst_hbm, map_sem)
        store.start()
        store.wait()

        def drain_class(c, _):
            _, n_pad = class_padding(c)

            @pl.when(n_pad > 0)
            def _():
                zero_rows(0, n_pad).wait()
            return 0

        lax.fori_loop(0, N_CLASSES, drain_class, 0)

        def drain_tile(j, _):
            zero_rows(0, TM).wait()
            return 0

        lax.fori_loop(nused_ref[0], n_tiles, drain_tile, 0)


def _dispatch(h, offs, cnt, n_used, cls, rank, n_rows):
    t = cls.shape[0]
    dst0 = jnp.zeros((n_rows,), jnp.int32)
    smem_blk = pl.BlockSpec((TB_DISPATCH,), lambda i, *_: (i,), memory_space=pltpu.SMEM)
    hbm = pl.BlockSpec(memory_space=pl.ANY)
    return pl.pallas_call(
        functools.partial(_dispatch_kernel, n_tiles=n_rows // TM),
        out_shape=(jax.ShapeDtypeStruct((n_rows * ROW_TILES, LANES), F32),
                   jax.ShapeDtypeStruct((n_rows,), jnp.int32)),
        grid_spec=pltpu.PrefetchScalarGridSpec(
            num_scalar_prefetch=3,
            grid=(t // TB_DISPATCH,),
            in_specs=[smem_blk, smem_blk,
                      pl.BlockSpec((TB_DISPATCH * ROW_TILES, LANES), lambda i, *_: (i, 0)),
                      hbm],
            out_specs=(hbm, hbm),
            scratch_shapes=[pltpu.SMEM((n_rows,), jnp.int32),
                            pltpu.VMEM((TM * ROW_TILES, LANES), F32),
                            pltpu.SemaphoreType.DMA,
                            pltpu.SemaphoreType.DMA,
                            pltpu.SemaphoreType.DMA]),
        compiler_params=pltpu.CompilerParams(dimension_semantics=("arbitrary",)),
        name="moe_dispatch",
    )(offs, cnt, n_used, cls, rank, h, dst0)


def _expert_kernel(ea_ref, eb_ref, nvalid_ref, nused_ref, *refs):
    k_tiles = TILES_PER_STEP
    dst_refs, x_refs = refs[:k_tiles], refs[k_tiles:2 * k_tiles]
    wr_ref, br_ref = refs[2 * k_tiles:2 * k_tiles + 2]
    w_refs = refs[2 * k_tiles + 2:6 * k_tiles + 2]
    lng_ref, lnb_ref, out_hbm, ybuf, ssem = refs[6 * k_tiles + 2:]
    step = pl.program_id(0)
    n_used = nused_ref[0]
    first_tile = step * k_tiles
    slot = step % 2
    other = 1 - slot

    def step_rows(s):
        base = jnp.maximum(s, 0) * k_tiles
        return sum(nvalid_ref[base + k] for k in range(k_tiles))

    def start_rows(k, n, s):
        for r in range(TM):
            @pl.when(r < n)
            def _(r=r):
                pltpu.make_async_copy(ybuf.at[s, _token_rows(k * TM + r), :],
                                      out_hbm.at[_token_rows(dst_refs[k][r]), :],
                                      ssem.at[s]).start(priority=r % 2)

    def wait_rows(n, s):
        @pl.when(n > 0)
        def _():
            rows = pl.ds(0, n * ROW_TILES)
            pltpu.make_async_copy(ybuf.at[s, rows, :], out_hbm.at[rows, :], ssem.at[s]).wait()

    def tile_result(k):
        x_ref = x_refs[k]
        wgua_ref, wda_ref, wgub_ref, wdb_ref = w_refs[4 * k:4 * k + 4]
        xb = _load_token_major(x_ref, TM).astype(BF16)

        logits = _dot(xb, wr_ref[...]) + br_ref[...]
        lane = lax.broadcasted_iota(jnp.int32, logits.shape, 1)
        ea, eb = ea_ref[first_tile + k], eb_ref[first_tile + k]
        group = ea // EXPERTS_PER_GROUP
        gl = jnp.where(lane < N_GROUPS, logits, F32(-jnp.inf))
        gmax = jnp.max(gl, axis=-1, keepdims=True)
        gsum = jnp.sum(jnp.where(lane < N_GROUPS, jnp.exp(gl - gmax), 0.0), axis=-1, keepdims=True)
        lg = jnp.sum(jnp.where(lane == group, logits, 0.0), axis=-1, keepdims=True)
        g_top_p = jnp.exp(lg - gmax) / gsum
        la = jnp.sum(jnp.where(lane == ea + N_GROUPS, logits, 0.0), axis=-1, keepdims=True)
        lb = jnp.sum(jnp.where(lane == eb + N_GROUPS, logits, 0.0), axis=-1, keepdims=True)
        w_a = g_top_p / (1.0 + jnp.exp(lb - la))
        w_b = g_top_p / (1.0 + jnp.exp(la - lb))

        def hidden(wgu_ref, weight):
            gu = _dot(xb, wgu_ref[0, 0])
            gate, up = gu[:, :D_EXPERT], gu[:, D_EXPERT:]
            return ((gate * _sigmoid(gate)) * up * weight).astype(BF16)

        ffn = _dot(hidden(wgua_ref, w_a), wda_ref[0, 0]) + _dot(hidden(wgub_ref, w_b), wdb_ref[0, 0])
        return _layer_norm_rows(ALPHA * _load_token_major(x_ref, TM) + ffn, lng_ref[...], lnb_ref[...])

    @pl.when(first_tile < n_used)
    def _():
        results = [tile_result(k) for k in range(k_tiles)]

        @pl.when(step >= 2)
        def _():
            wait_rows(step_rows(step - 2), slot)

        for k in range(k_tiles):
            for c in range(ROW_TILES):
                ybuf[slot, pl.ds(k * TM * ROW_TILES + c, TM, stride=ROW_TILES), :] = (
                    results[k][:, c * LANES:(c + 1) * LANES])
            start_rows(k, nvalid_ref[first_tile + k], slot)

        @pl.when(first_tile + k_tiles >= n_used)
        def _():
            @pl.when(step >= 1)
            def _():
                wait_rows(step_rows(step - 1), other)

            wait_rows(step_rows(step), slot)


def _experts(xs, t, dst, tile_ea, tile_eb, tile_nvalid, n_used, w_router, b_router, wgu, wd, layer, ln_g, ln_b):
    k_tiles = TILES_PER_STEP
    n_tiles = dst.shape[0] // TM
    assert n_tiles % k_tiles == 0
    const = lambda shape: pl.BlockSpec(shape, lambda i, ea, eb, nv, nu: (0,) * len(shape))
    dst_spec = lambda k: pl.BlockSpec((TM,), lambda i, ea, eb, nv, nu: (i * k_tiles + k,),
                                      memory_space=pltpu.SMEM)
    x_spec = lambda k: pl.BlockSpec((TM * ROW_TILES, LANES), lambda i, ea, eb, nv, nu: (i * k_tiles + k, 0))
    wgu_spec = lambda k, which: pl.BlockSpec(
        (1, 1, D_MODEL, 2 * D_EXPERT),
        lambda i, ea, eb, nv, nu: (layer, (ea, eb)[which][i * k_tiles + k], 0, 0))
    wd_spec = lambda k, which: pl.BlockSpec(
        (1, 1, D_EXPERT, D_MODEL),
        lambda i, ea, eb, nv, nu: (layer, (ea, eb)[which][i * k_tiles + k], 0, 0))
    tiles = range(k_tiles)
    weight_specs, weight_args = [], []
    for k in tiles:
        weight_specs += [wgu_spec(k, 0), wd_spec(k, 0), wgu_spec(k, 1), wd_spec(k, 1)]
        weight_args += [wgu, wd, wgu, wd]
    grid_spec = pltpu.PrefetchScalarGridSpec(
        num_scalar_prefetch=4,
        grid=(n_tiles // k_tiles,),
        in_specs=([dst_spec(k) for k in tiles] + [x_spec(k) for k in tiles]
                  + [const((D_MODEL, LANES)), const((1, LANES))] + weight_specs
                  + [const((1, D_MODEL)), const((1, D_MODEL))]),
        out_specs=pl.BlockSpec(memory_space=pl.ANY),
        scratch_shapes=[pltpu.VMEM((2, k_tiles * TM * ROW_TILES, LANES), F32),
                        pltpu.SemaphoreType.DMA((2,))],
    )
    return pl.pallas_call(
        _expert_kernel,
        out_shape=jax.ShapeDtypeStruct((t * ROW_TILES, LANES), F32),
        grid_spec=grid_spec,
        compiler_params=pltpu.CompilerParams(dimension_semantics=("arbitrary",),
                                             vmem_limit_bytes=VMEM_LIMIT_BYTES),
        name="moe_experts",
    )(tile_ea, tile_eb, tile_nvalid, n_used, *([dst] * k_tiles), *([xs] * k_tiles),
      w_router, b_router, *weight_args, ln_g, ln_b)


def _class_expert_tables():
    first, second = [], []
    for g in range(N_GROUPS):
        for lo in range(EXPERTS_PER_GROUP):
            for hi in range(lo + 1, EXPERTS_PER_GROUP):
                first.append(g * EXPERTS_PER_GROUP + lo)
                second.append(g * EXPERTS_PER_GROUP + hi)
    return jnp.asarray(np.array(first, np.int32)), jnp.asarray(np.array(second, np.int32))


def _router_weights(wg, bg, we, be):
    w_router = jnp.zeros((D_MODEL, LANES), F32)
    w_router = w_router.at[:, :N_GROUPS].set(wg).at[:, N_GROUPS:N_GROUPS + N_EXPERTS].set(we)
    b_router = jnp.zeros((1, LANES), F32)
    b_router = b_router.at[0, :N_GROUPS].set(bg).at[0, N_GROUPS:N_GROUPS + N_EXPERTS].set(be)
    w_hi = w_router.astype(BF16)
    w_lo = (w_router - w_hi.astype(F32)).astype(BF16)
    return w_hi, w_lo, b_router


def _moe_layer(h, t, meta, counts, w_hi, b_router, wgu, wd, layer, ln_g, ln_b):
    n_tiles = t // TM + N_CLASSES
    n_rows = n_tiles * TM

    cnt = counts[0]
    tiles_per = (cnt + TM - 1) // TM
    tile_end = jnp.cumsum(tiles_per)
    offs = (tile_end - tiles_per) * TM
    n_used = tile_end[-1]
    tile_id = jnp.arange(n_tiles, dtype=jnp.int32)
    tile_cls = jnp.sum((jnp.minimum(tile_id, n_used - 1)[:, None] >= tile_end[None, :N_CLASSES])
                       .astype(jnp.int32), axis=1)
    seg_tile = tile_id - (tile_end - tiles_per)[tile_cls]
    tile_nvalid = jnp.where(tile_id < n_used, jnp.clip(cnt[tile_cls] - seg_tile * TM, 0, TM), 0)
    first, second = _class_expert_tables()
    xs, dst = _dispatch(h, offs, cnt, n_used.reshape(1), meta[0], meta[1], n_rows)
    return _experts(xs, t, dst, first[tile_cls], second[tile_cls], tile_nvalid, n_used.reshape(1),
                    w_hi, b_router, wgu, wd, layer, ln_g, ln_b)


def kernel(x, positions, w_in, sc_conv_w, sc_conv_b, lru_conv_w, lru_conv_b, lru_w_r, lru_b_r,
           lru_w_i, lru_b_i, lru_lambda, sg_norm_g, sg_w_s, sg_b_s, ret_norm_g, branch_proj, w_out,
           ln_mix_g, ln_mix_b, router_group_w, router_group_b, router_expert_w, router_expert_b,
           exp_w_gate, exp_w_up, exp_w_down, ln_ffn_g, ln_ffn_b):
    batch, seq, d = x.shape
    assert d == D_MODEL and seq % TS == 0 and w_in.shape[-1] == N_IN
    depth = w_in.shape[0]
    t = batch * seq
    cos_t, sin_t = _rope_tables(positions)
    dec_all, qd_tab, kd_tab, cd_tab = _retention_tables()
    avg = jnp.asarray(np.kron(np.eye(N_HEADS, dtype=np.float32),
                              np.full((HEAD_DIM, HEAD_DIM), 1.0 / HEAD_DIM, np.float32)), BF16)

    w_in_b, bp_b, wo_b = w_in.astype(BF16), branch_proj.astype(BF16), w_out.astype(BF16)
    wgu_b = jnp.concatenate([exp_w_gate.astype(BF16), exp_w_up.astype(BF16)], axis=-1)
    wd_b = exp_w_down.astype(BF16)

    h = x.reshape(t, d)
    for l in range(depth):
        w_ri = jnp.concatenate([_block_diag(lru_w_r[l]), _block_diag(lru_w_i[l])], axis=1).astype(BF16)
        b_ri = jnp.concatenate([lru_b_r[l], lru_b_i[l]])[None, :]
        bs_tab = jnp.repeat(sg_b_s[l].T, HEAD_DIM, axis=1)
        per_call = lambda *arrs: [(a, None) for a in arrs]
        w_hi, w_lo, b_router = _router_weights(router_group_w[l], router_group_b[l],
                                               router_expert_w[l], router_expert_b[l])
        consts = (
            [(w_in_b, l)]
            + per_call(sc_conv_w[l], sc_conv_b[l][None, :], lru_conv_w[l], lru_conv_b[l][None, :],
                       w_ri, b_ri, lru_lambda[l][None, :], sg_norm_g[l][None, :], sg_w_s[l], bs_tab,
                       ret_norm_g[l][None, :], dec_all, qd_tab, kd_tab, cd_tab, avg)
            + [(bp_b, l), (wo_b, l)]
            + per_call(ln_mix_g[l][None, :], ln_mix_b[l][None, :], w_hi, w_lo, b_router))
        h, meta, counts = _mixer_layer(h, cos_t, sin_t, consts, batch, seq, token_major_in=l > 0)
        h = _moe_layer(h, t, meta, counts, w_hi, b_router, wgu_b, wd_b, l,
                       ln_ffn_g[l][None, :], ln_ffn_b[l][None, :])
    return h.reshape(batch, seq, d)
```

```python
import functools

import jax
import jax.numpy as jnp
import numpy as np
from jax import lax
from jax.experimental import pallas as pl
from jax.experimental.pallas import tpu as pltpu

F32 = jnp.float32
BF16 = jnp.bfloat16

LANES = 128
SUBLANES = 8
VMEM_LIMIT_BYTES = 56 * 1024 * 1024

D_MODEL = 1024
ROW_TILES = D_MODEL // LANES
assert ROW_TILES == SUBLANES
WIDTH = D_MODEL // 4
N_HEADS = 4
N_BRANCH = 4
HEAD_DIM = WIDTH // N_HEADS
CHUNK = 128
SC_K = 3
LRU_K = 4
LRU_C = 8.0
ROPE_BASE = 10000.0
N_GROUPS = 4
EXPERTS_PER_GROUP = 8
N_EXPERTS = N_GROUPS * EXPERTS_PER_GROUP
D_EXPERT = D_MODEL // 4
LN_EPS = 1e-5
DEPTH = 2
ALPHA = (2.0 * DEPTH) ** 0.25

COL_SC_B, COL_SC_C, COL_SC_X, COL_LRU, COL_SG_U, COL_SG_V, COL_Q, COL_K, COL_V, COL_G = (
    i * WIDTH for i in range(10))
COL_GATES = 10 * WIDTH
N_IN = COL_GATES + 4 * D_MODEL

TS = 512
GATE_CHUNK = 256
SCAN_PAD = TS // 2
CONV_PAD = SUBLANES
TB_DISPATCH = 512
DISPATCH_UNROLL = 8
TM = 128
TILES_PER_STEP = 4
PAIRS_PER_GROUP = EXPERTS_PER_GROUP * (EXPERTS_PER_GROUP - 1) // 2
N_CLASSES = N_GROUPS * PAIRS_PER_GROUP
assert N_CLASSES <= LANES


def _sigmoid(x):
    return 0.5 * jnp.tanh(0.5 * x) + 0.5


def _dot(a, b):
    return jnp.dot(a, b, preferred_element_type=F32)


def _load_token_major(ref, rows, lead=()):
    return jnp.concatenate(
        [ref[lead + (pl.ds(c, rows, stride=ROW_TILES), slice(None))] for c in range(ROW_TILES)], axis=1)


def _store_token_major(ref, y, rows, lead=()):
    for c in range(ROW_TILES):
        ref[lead + (pl.ds(c, rows, stride=ROW_TILES), slice(None))] = y[:, c * LANES:(c + 1) * LANES]


def _layer_norm_rows(y, g, b):
    mu = jnp.mean(y, axis=-1, keepdims=True)
    yc = y - mu
    var = jnp.mean(yc * yc, axis=-1, keepdims=True)
    return yc * lax.rsqrt(var + LN_EPS) * g + b


def _rope_table_kernel(pos_ref, freq_ref, cos_ref, sin_ref):
    ang = pos_ref[...].astype(F32) * freq_ref[...]
    lane = lax.broadcasted_iota(jnp.int32, ang.shape, 1)
    first_half = (lane % HEAD_DIM) < (HEAD_DIM // 2)
    cos_ref[...] = jnp.cos(ang)
    s = jnp.sin(ang)
    sin_ref[...] = jnp.where(first_half, -s, s)


def _rope_tables(positions):
    t = positions.size
    half = HEAD_DIM // 2
    inv_freq = ROPE_BASE ** (-jnp.arange(half, dtype=F32) / half)
    freq_row = jnp.tile(inv_freq, LANES // half)[None, :]
    rows = 1024
    return pl.pallas_call(
        _rope_table_kernel,
        out_shape=(jax.ShapeDtypeStruct((t, LANES), F32), jax.ShapeDtypeStruct((t, LANES), F32)),
        grid=(t // rows,),
        in_specs=[pl.BlockSpec((rows, 1), lambda i: (i, 0)),
                  pl.BlockSpec((1, LANES), lambda i: (0, 0))],
        out_specs=(pl.BlockSpec((rows, LANES), lambda i: (i, 0)),
                   pl.BlockSpec((rows, LANES), lambda i: (i, 0))),
        name="rope_tables",
    )(positions.reshape(t, 1), freq_row)


def _mixer_kernel(h_ref, cos_ref, sin_ref, w_in_ref, scw_ref, scb_ref, lcw_ref, lcb_ref,
                  wri_ref, bri_ref, lam_ref, sgg_ref, ws_ref, bs_ref, rg_ref,
                  dec_ref, qd_ref, kd_ref, cd_ref, avg_ref, bp_ref, wo_ref, lng_ref, lnb_ref,
                  rwhi_ref, rwlo_ref, rb_ref,
                  out_ref, meta_ref, cnt_ref,
                  ubuf, xbuf, hcar, sbd, abuf, hbuf, gbuf, rcount, tri, *, token_major_in):
    step = pl.program_id(1)

    @pl.when((pl.program_id(0) == 0) & (step == 0))
    def _():
        rcount[...] = jnp.zeros_like(rcount)
        r_i = lax.broadcasted_iota(jnp.int32, (TS, TS), 0)
        c_i = lax.broadcasted_iota(jnp.int32, (TS, TS), 1)
        tri[...] = jnp.where(r_i > c_i, 1.0, 0.0).astype(BF16)

    @pl.when(step == 0)
    def _():
        ubuf[0:CONV_PAD, :] = jnp.zeros((CONV_PAD, WIDTH), F32)
        xbuf[0:CONV_PAD, :] = jnp.zeros((CONV_PAD, WIDTH), F32)
        hcar[...] = jnp.zeros_like(hcar)
        sbd[...] = jnp.zeros_like(sbd)
        abuf[0:SCAN_PAD, :] = jnp.ones((SCAN_PAD, WIDTH), F32)
        hbuf[0:SCAN_PAD, :] = jnp.zeros((SCAN_PAD, WIDTH), F32)

    h = _load_token_major(h_ref, TS) if token_major_in else h_ref[...]
    hb = h.astype(BF16)

    def proj(col, width=WIDTH):
        return _dot(hb, w_in_ref[0, :, col:col + width])

    pending_gate_cols = list(range(0, N_BRANCH * D_MODEL, GATE_CHUNK))

    def emit_gates(n_chunks=1):
        for _ in range(n_chunks):
            if pending_gate_cols:
                col = pending_gate_cols.pop(0)
                gbuf[:, col:col + GATE_CHUNK] = _sigmoid(proj(COL_GATES + col, GATE_CHUNK))

    lane = lax.broadcasted_iota(jnp.int32, (CHUNK, WIDTH), 1)
    head_of_lane = lane // HEAD_DIM
    avg = avg_ref[...]

    def group_standardize(x):
        mean = _dot(x.astype(BF16), avg)
        xc = x - mean
        var = _dot((xc * xc).astype(BF16), avg)
        return xc * lax.rsqrt(var + LN_EPS)

    sc_b, sc_c, sc_x = proj(COL_SC_B), proj(COL_SC_C), proj(COL_SC_X)
    u = sc_c * sc_x
    ubuf[CONV_PAD:CONV_PAD + TS, :] = u
    conv = scw_ref[2:3, :] * u + scb_ref[...]
    for j in range(SC_K - 1):
        back = SC_K - 1 - j
        conv = conv + scw_ref[j:j + 1, :] * ubuf[CONV_PAD - back:CONV_PAD - back + TS, :]
    ubuf[0:CONV_PAD, :] = ubuf[TS:TS + CONV_PAD, :]
    branch_a = sc_b * conv
    emit_gates()

    lx = proj(COL_LRU)
    xbuf[CONV_PAD:CONV_PAD + TS, :] = lx
    xc = lcw_ref[LRU_K - 1:LRU_K, :] * lx + lcb_ref[...]
    for j in range(LRU_K - 1):
        back = LRU_K - 1 - j
        xc = xc + lcw_ref[j:j + 1, :] * xbuf[CONV_PAD - back:CONV_PAD - back + TS, :]
    xbuf[0:CONV_PAD, :] = xbuf[TS:TS + CONV_PAD, :]
    emit_gates()
    ri = _dot(xc.astype(BF16), wri_ref[...]) + bri_ref[...]
    r = _sigmoid(ri[:, :WIDTH])
    ig = _sigmoid(ri[:, WIDTH:])
    neg_lam = -lam_ref[...]
    softplus = jnp.maximum(neg_lam, 0.0) + jnp.log1p(jnp.exp(-jnp.abs(neg_lam)))
    log_a = (-LRU_C) * r * softplus
    a = jnp.exp(log_a)
    th = jnp.tanh(log_a)
    uu = jnp.sqrt((-2.0) * th / (1.0 - th)) * (ig * xc)
    row = lax.broadcasted_iota(jnp.int32, (TS, WIDTH), 0)
    uu = uu + jnp.where(row == 0, a * hcar[...], 0.0)
    abuf[SCAN_PAD:SCAN_PAD + TS, :] = a
    hbuf[SCAN_PAD:SCAN_PAD + TS, :] = uu
    shift = 1
    while shift < TS:
        a_cur = abuf[SCAN_PAD:SCAN_PAD + TS, :]
        h_prev = hbuf[SCAN_PAD - shift:SCAN_PAD - shift + TS, :]
        h_new = a_cur * h_prev + hbuf[SCAN_PAD:SCAN_PAD + TS, :]
        if shift * 2 < TS:
            a_new = a_cur * abuf[SCAN_PAD - shift:SCAN_PAD - shift + TS, :]
            abuf[SCAN_PAD:SCAN_PAD + TS, :] = a_new
        hbuf[SCAN_PAD:SCAN_PAD + TS, :] = h_new
        emit_gates()
        shift *= 2
    branch_b = hbuf[SCAN_PAD:SCAN_PAD + TS, :]
    hcar[...] = hbuf[SCAN_PAD + TS - 1:SCAN_PAD + TS, :]

    gu = jax.nn.gelu(proj(COL_SG_U))
    gv = jax.nn.gelu(proj(COL_SG_V))
    emit_gates()
    vn = (group_standardize(gv) * sgg_ref[...]).astype(BF16)
    trow = lax.broadcasted_iota(jnp.int32, (CHUNK, CHUNK), 0)
    tcol = lax.broadcasted_iota(jnp.int32, (CHUNK, CHUNK), 1)
    w_causal = [jnp.where(trow >= tcol, ws_ref[g], 0.0).astype(BF16) for g in range(N_HEADS)]
    sv_chunks = []
    for c in range(TS // CHUNK):
        vch = vn[c * CHUNK:(c + 1) * CHUNK, :]
        sv = bs_ref[...]
        for g in range(N_HEADS):
            sv = sv + jnp.where(head_of_lane == g, _dot(w_causal[g], vch), 0.0)
        sv_chunks.append(sv)
    branch_c = gu * jnp.concatenate(sv_chunks, axis=0)
    emit_gates()

    cos = jnp.concatenate([cos_ref[...]] * (WIDTH // LANES), axis=1)
    sin = jnp.concatenate([sin_ref[...]] * (WIDTH // LANES), axis=1)
    lane_ts = lax.broadcasted_iota(jnp.int32, (TS, WIDTH), 1)
    first_half = (lane_ts % HEAD_DIM) < (HEAD_DIM // 2)

    def rope(t):
        swapped = jnp.where(first_half,
                            pltpu.roll(t, WIDTH - HEAD_DIM // 2, 1),
                            pltpu.roll(t, HEAD_DIM // 2, 1))
        return t * cos + swapped * sin

    q = rope(proj(COL_Q))
    k = rope(proj(COL_K)) * (HEAD_DIM ** -0.5)
    emit_gates()
    v = proj(COL_V)
    zg = proj(COL_G)
    bd_mask = (lax.broadcasted_iota(jnp.int32, (WIDTH, WIDTH), 0) // HEAD_DIM
               == lax.broadcasted_iota(jnp.int32, (WIDTH, WIDTH), 1) // HEAD_DIM)
    o_chunks = []
    for c in range(TS // CHUNK):
        sl = slice(c * CHUNK, (c + 1) * CHUNK)
        qc, kc, vc = q[sl, :], k[sl, :], v[sl, :]
        qcb = qc.astype(BF16)
        kstack = jnp.concatenate(
            [jnp.where(head_of_lane == hh, kc, 0.0) for hh in range(N_HEADS)], axis=0).astype(BF16)
        vstack = jnp.concatenate(
            [jnp.where(head_of_lane == hh, vc, 0.0) for hh in range(N_HEADS)], axis=0).astype(BF16)
        scores = lax.dot_general(qcb, kstack, (((1,), (1,)), ((), ())),
                                 preferred_element_type=F32)
        inner = _dot((scores * dec_ref[...]).astype(BF16), vstack)
        state = sbd[...]
        cross = _dot(qcb, state.astype(BF16)) * qd_ref[...]
        o_chunks.append(inner + cross)
        kdec = (kc * kd_ref[...]).astype(BF16)
        kv = lax.dot_general(kdec, vc.astype(BF16), (((0,), (0,)), ((), ())),
                             preferred_element_type=F32)
        sbd[...] = cd_ref[...] * state + jnp.where(bd_mask, kv, 0.0)
        emit_gates()
    o = jnp.concatenate(o_chunks, axis=0)
    branch_d = (zg * _sigmoid(zg)) * (group_standardize(o) * rg_ref[...])

    emit_gates(len(pending_gate_cols))
    merged = jnp.zeros((TS, D_MODEL), F32)
    for b_idx, branch in enumerate((branch_a, branch_b, branch_c, branch_d)):
        gate = gbuf[:, b_idx * D_MODEL:(b_idx + 1) * D_MODEL]
        merged = merged + gate * _dot(branch.astype(BF16), bp_ref[0, b_idx])
    mix = _dot(merged.astype(BF16), wo_ref[0])
    h_mid = _layer_norm_rows(ALPHA * h + mix, lng_ref[...], lnb_ref[...])
    _store_token_major(out_ref, h_mid, TS)
    _route(h_mid, rwhi_ref, rwlo_ref, rb_ref, tri, rcount, meta_ref, cnt_ref)


def _const_spec(arr, layer):
    if layer is None:
        return pl.BlockSpec(arr.shape, lambda b, i, _nd=arr.ndim: (0,) * _nd, pipeline_mode=pl.Buffered(1))
    return pl.BlockSpec((1,) + arr.shape[1:], lambda b, i, _nd=arr.ndim: (layer,) + (0,) * (_nd - 1),
                        pipeline_mode=pl.Buffered(1))


def _mixer_layer(h, cos_t, sin_t, consts, batch, seq, token_major_in):
    t = batch * seq
    steps = seq // TS
    row_map = lambda b, i: (b * steps + i, 0)
    tm_spec = pl.BlockSpec((TS * ROW_TILES, LANES), row_map)
    in_specs = [tm_spec if token_major_in else pl.BlockSpec((TS, D_MODEL), row_map),
                pl.BlockSpec((TS, LANES), row_map),
                pl.BlockSpec((TS, LANES), row_map)]
    in_specs += [_const_spec(arr, layer) for arr, layer in consts]
    return pl.pallas_call(
        functools.partial(_mixer_kernel, token_major_in=token_major_in),
        out_shape=(jax.ShapeDtypeStruct((t * ROW_TILES, LANES), F32),
                   jax.ShapeDtypeStruct((SUBLANES, t), jnp.int32),
                   jax.ShapeDtypeStruct((SUBLANES, LANES), jnp.int32)),
        grid=(batch, steps),
        in_specs=in_specs,
        out_specs=(tm_spec,
                   pl.BlockSpec((SUBLANES, TS), lambda b, i: (0, b * steps + i)),
                   pl.BlockSpec((SUBLANES, LANES), lambda b, i: (0, 0))),
        scratch_shapes=[
            pltpu.VMEM((CONV_PAD + TS, WIDTH), F32),
            pltpu.VMEM((CONV_PAD + TS, WIDTH), F32),
            pltpu.VMEM((1, WIDTH), F32),
            pltpu.VMEM((WIDTH, WIDTH), F32),
            pltpu.VMEM((SCAN_PAD + TS, WIDTH), F32),
            pltpu.VMEM((SCAN_PAD + TS, WIDTH), F32),
            pltpu.VMEM((TS, N_BRANCH * D_MODEL), F32),
            pltpu.VMEM((1, LANES), F32),
            pltpu.VMEM((TS, TS), BF16),
        ],
        compiler_params=pltpu.CompilerParams(
            dimension_semantics=("arbitrary", "arbitrary"),
            vmem_limit_bytes=VMEM_LIMIT_BYTES),
        name="mixer_layer",
    )(h, cos_t, sin_t, *[arr for arr, _ in consts])


def _block_diag(w):
    heads, d, e = w.shape
    eye = jnp.eye(heads, dtype=w.dtype)
    return (eye[:, None, :, None] * w[:, :, None, :]).reshape(heads * d, heads * e)


def _retention_tables():
    f32 = np.float32
    log_gamma = np.log1p(-np.exp2(f32(-5.0) - np.arange(N_HEADS, dtype=f32))).astype(f32)
    pos = np.arange(CHUNK, dtype=f32)
    diff = pos[:, None] - pos[None, :]
    decay = np.where(diff >= 0, np.exp(np.maximum(diff, f32(0)) * log_gamma[:, None, None]), f32(0)).astype(f32)
    dec_all = np.transpose(decay, (1, 0, 2)).reshape(CHUNK, N_HEADS * CHUNK)
    q_decay = np.exp((pos + f32(1))[:, None] * log_gamma).astype(f32)
    k_decay = np.exp((f32(CHUNK - 1) - pos)[:, None] * log_gamma).astype(f32)
    qd_tab = np.repeat(q_decay, HEAD_DIM, axis=1)
    kd_tab = np.repeat(k_decay, HEAD_DIM, axis=1)
    chunk_decay = np.repeat(np.exp(f32(CHUNK) * log_gamma).astype(f32), HEAD_DIM)
    cd_tab = np.ascontiguousarray(np.broadcast_to(chunk_decay[:, None], (WIDTH, WIDTH)))
    return tuple(jnp.asarray(a, F32) for a in (dec_all, qd_tab, kd_tab, cd_tab))


def _route(h, whi_ref, wlo_ref, b_ref, tri_ref, count_ref, meta_ref, cnt_ref):
    h_hi = h.astype(BF16)
    h_lo = (h - h_hi.astype(F32)).astype(BF16)
    whi = whi_ref[...]
    logits = _dot(h_hi, whi) + _dot(h_lo, whi) + _dot(h_hi, wlo_ref[...]) + b_ref[...]
    lane = lax.broadcasted_iota(jnp.int32, logits.shape, 1)
    big = jnp.int32(2 ** 30)
    neg_inf = F32(-jnp.inf)

    gl = jnp.where(lane < N_GROUPS, logits, neg_inf)
    gmax = jnp.max(gl, axis=-1, keepdims=True)
    g_idx = jnp.min(jnp.where(gl == gmax, lane, big), axis=-1, keepdims=True)

    e_lane = lane - N_GROUPS
    in_group = (e_lane >= g_idx * EXPERTS_PER_GROUP) & (e_lane < (g_idx + 1) * EXPERTS_PER_GROUP)
    el = jnp.where(in_group, logits, neg_inf)
    m1 = jnp.max(el, axis=-1, keepdims=True)
    i1 = jnp.min(jnp.where(el == m1, lane, big), axis=-1, keepdims=True)
    el2 = jnp.where(lane == i1, neg_inf, el)
    m2 = jnp.max(el2, axis=-1, keepdims=True)
    i2 = jnp.min(jnp.where(el2 == m2, lane, big), axis=-1, keepdims=True)
    base = (g_idx * EXPERTS_PER_GROUP + N_GROUPS).astype(F32)
    lo = jnp.minimum(i1, i2).astype(F32) - base
    hi = jnp.maximum(i1, i2).astype(F32) - base
    pair = lo * (2.0 * EXPERTS_PER_GROUP - 1.0 - lo) * 0.5 + (hi - lo - 1.0)
    cls = (g_idx.astype(F32) * PAIRS_PER_GROUP + pair).astype(jnp.int32)

    onehot = jnp.where(lane == cls, 1.0, 0.0)
    before = _dot(tri_ref[...], onehot.astype(BF16)) + count_ref[...]
    rank = jnp.sum(jnp.where(lane == cls, before, 0.0), axis=-1, keepdims=True)
    count_ref[...] = count_ref[...] + jnp.sum(onehot, axis=0, keepdims=True)

    meta = jnp.where(lane == 0, cls.astype(F32), 0.0) + jnp.where(lane == 1, rank, 0.0)
    meta_ref[...] = jnp.transpose(meta)[0:SUBLANES, :].astype(jnp.int32)
    cnt_ref[...] = jnp.broadcast_to(count_ref[...], cnt_ref.shape).astype(jnp.int32)


def _token_rows(tok):
    if isinstance(tok, int):
        return pl.ds(tok * ROW_TILES, ROW_TILES)
    return pl.ds(pl.multiple_of(tok * ROW_TILES, ROW_TILES), ROW_TILES)


def _dispatch_kernel(offs_ref, cnt_ref, nused_ref, cls_ref, rank_ref, h_ref, dst0_hbm, xs_hbm, dst_hbm,
                     dst_ref, zbuf, row_sem, map_sem, zero_sem, *, n_tiles):
    step = pl.program_id(0)
    pad_bits = [1 << b for b in reversed(range(TM.bit_length() - 1))]

    def class_padding(c):
        cnt = cnt_ref[c]
        return offs_ref[c] + cnt, (TM - cnt % TM) % TM

    def zero_rows(first_row, n_rows):
        start = first_row * ROW_TILES
        if not isinstance(start, int):
            start = pl.multiple_of(start, ROW_TILES)
        return pltpu.make_async_copy(zbuf.at[pl.ds(0, n_rows * ROW_TILES), :],
                                     xs_hbm.at[pl.ds(start, n_rows * ROW_TILES), :], zero_sem)

    @pl.when(step == 0)
    def _():
        load = pltpu.make_async_copy(dst0_hbm, dst_ref, map_sem)
        load.start()
        load.wait()

        zbuf[...] = jnp.zeros_like(zbuf)

        def fill_class(c, _):
            first_row, n_pad = class_padding(c)
            for bit in pad_bits:
                @pl.when((n_pad & bit) != 0)
                def _(bit=bit):
                    zero_rows(first_row + (n_pad & ~(2 * bit - 1)), bit).start()
            return 0

        lax.fori_loop(0, N_CLASSES, fill_class, 0)

        def fill_tile(j, _):
            zero_rows(j * TM, TM).start()
            return 0

        lax.fori_loop(nused_ref[0], n_tiles, fill_tile, 0)

    def send(g, _):
        for k in range(DISPATCH_UNROLL):
            r = g * DISPATCH_UNROLL + k
            p = offs_ref[cls_ref[r]] + rank_ref[r]
            dst_ref[p] = step * TB_DISPATCH + r
            pltpu.make_async_copy(h_ref.at[_token_rows(r), :], xs_hbm.at[_token_rows(p), :],
                                  row_sem).start(priority=k % 2)
        return 0

    lax.fori_loop(0, TB_DISPATCH // DISPATCH_UNROLL, send, 0)

    pltpu.make_async_copy(h_ref, xs_hbm.at[pl.ds(0, TB_DISPATCH * ROW_TILES), :], row_sem).wait()

    @pl.when(step == pl.num_programs(0) - 1)
    def _():
        store = pltpu.make_async_copy(dst_ref, dst_hbm, map_sem)
        store.start()
        store.wait()

        def drain_class(c, _):
            _, n_pad = class_padding(c)

            @pl.when(n_pad > 0)
            def _():
                zero_rows(0, n_pad).wait()
            return 0

        lax.fori_loop(0, N_CLASSES, drain_class, 0)

        def drain_tile(j, _):
            zero_rows(0, TM).wait()
            return 0

        lax.fori_loop(nused_ref[0], n_tiles, drain_tile, 0)


def _dispatch(h, offs, cnt, n_used, cls, rank, n_rows):
    t = cls.shape[0]
    dst0 = jnp.zeros((n_rows,), jnp.int32)
    smem_blk = pl.BlockSpec((TB_DISPATCH,), lambda i, *_: (i,), memory_space=pltpu.SMEM)
    hbm = pl.BlockSpec(memory_space=pl.ANY)
    return pl.pallas_call(
        functools.partial(_dispatch_kernel, n_tiles=n_rows // TM),
        out_shape=(jax.ShapeDtypeStruct((n_rows * ROW_TILES, LANES), F32),
                   jax.ShapeDtypeStruct((n_rows,), jnp.int32)),
        grid_spec=pltpu.PrefetchScalarGridSpec(
            num_scalar_prefetch=3,
            grid=(t // TB_DISPATCH,),
            in_specs=[smem_blk, smem_blk,
                      pl.BlockSpec((TB_DISPATCH * ROW_TILES, LANES), lambda i, *_: (i, 0)),
                      hbm],
            out_specs=(hbm, hbm),
            scratch_shapes=[pltpu.SMEM((n_rows,), jnp.int32),
                            pltpu.VMEM((TM * ROW_TILES, LANES), F32),
                            pltpu.SemaphoreType.DMA,
                            pltpu.SemaphoreType.DMA,
                            pltpu.SemaphoreType.DMA]),
        compiler_params=pltpu.CompilerParams(dimension_semantics=("arbitrary",)),
        name="moe_dispatch",
    )(offs, cnt, n_used, cls, rank, h, dst0)


def _expert_kernel(ea_ref, eb_ref, nvalid_ref, nused_ref, *refs):
    k_tiles = TILES_PER_STEP
    dst_refs, x_refs = refs[:k_tiles], refs[k_tiles:2 * k_tiles]
    wr_ref, br_ref = refs[2 * k_tiles:2 * k_tiles + 2]
    w_refs = refs[2 * k_tiles + 2:8 * k_tiles + 2]
    lng_ref, lnb_ref, out_hbm, ybuf, ssem = refs[8 * k_tiles + 2:]
    step = pl.program_id(0)
    n_used = nused_ref[0]
    first_tile = step * k_tiles
    slot = step % 2
    other = 1 - slot

    def step_rows(s):
        base = jnp.maximum(s, 0) * k_tiles
        return sum(nvalid_ref[base + k] for k in range(k_tiles))

    def start_rows(k, n, s):
        for r in range(TM):
            @pl.when(r < n)
            def _(r=r):
                pltpu.make_async_copy(ybuf.at[s, _token_rows(k * TM + r), :],
                                      out_hbm.at[_token_rows(dst_refs[k][r]), :],
                                      ssem.at[s]).start(priority=r % 2)

    def wait_rows(n, s):
        @pl.when(n > 0)
        def _():
            rows = pl.ds(0, n * ROW_TILES)
            pltpu.make_async_copy(ybuf.at[s, rows, :], out_hbm.at[rows, :], ssem.at[s]).wait()

    def tile_result(k):
        x_ref = x_refs[k]
        wga_ref, wua_ref, wda_ref, wgb_ref, wub_ref, wdb_ref = w_refs[6 * k:6 * k + 6]
        xb = _load_token_major(x_ref, TM).astype(BF16)

        logits = _dot(xb, wr_ref[...]) + br_ref[...]
        lane = lax.broadcasted_iota(jnp.int32, logits.shape, 1)
        ea, eb = ea_ref[first_tile + k], eb_ref[first_tile + k]
        group = ea // EXPERTS_PER_GROUP
        gl = jnp.where(lane < N_GROUPS, logits, F32(-jnp.inf))
        gmax = jnp.max(gl, axis=-1, keepdims=True)
        gsum = jnp.sum(jnp.where(lane < N_GROUPS, jnp.exp(gl - gmax), 0.0), axis=-1, keepdims=True)
        lg = jnp.sum(jnp.where(lane == group, logits, 0.0), axis=-1, keepdims=True)
        g_top_p = jnp.exp(lg - gmax) / gsum
        la = jnp.sum(jnp.where(lane == ea + N_GROUPS, logits, 0.0), axis=-1, keepdims=True)
        lb = jnp.sum(jnp.where(lane == eb + N_GROUPS, logits, 0.0), axis=-1, keepdims=True)
        w_a = g_top_p / (1.0 + jnp.exp(lb - la))
        w_b = g_top_p / (1.0 + jnp.exp(la - lb))

        def hidden(wg_ref, wu_ref, weight):
            gate, up = _dot(xb, wg_ref[0, 0]), _dot(xb, wu_ref[0, 0])
            return ((gate * _sigmoid(gate)) * up * weight).astype(BF16)

        ffn = (_dot(hidden(wga_ref, wua_ref, w_a), wda_ref[0, 0])
               + _dot(hidden(wgb_ref, wub_ref, w_b), wdb_ref[0, 0]))
        return _layer_norm_rows(ALPHA * _load_token_major(x_ref, TM) + ffn, lng_ref[...], lnb_ref[...])

    @pl.when(first_tile < n_used)
    def _():
        results = [tile_result(k) for k in range(k_tiles)]

        @pl.when(step >= 2)
        def _():
            wait_rows(step_rows(step - 2), slot)

        for k in range(k_tiles):
            for c in range(ROW_TILES):
                ybuf[slot, pl.ds(k * TM * ROW_TILES + c, TM, stride=ROW_TILES), :] = (
                    results[k][:, c * LANES:(c + 1) * LANES])
            start_rows(k, nvalid_ref[first_tile + k], slot)

        @pl.when(first_tile + k_tiles >= n_used)
        def _():
            @pl.when(step >= 1)
            def _():
                wait_rows(step_rows(step - 1), other)

            wait_rows(step_rows(step), slot)


def _experts(xs, t, dst, tile_ea, tile_eb, tile_nvalid, n_used, w_router, b_router, wg, wu, wd, layer,
             ln_g, ln_b):
    k_tiles = TILES_PER_STEP
    n_tiles = dst.shape[0] // TM
    assert n_tiles % k_tiles == 0
    const = lambda shape: pl.BlockSpec(shape, lambda i, ea, eb, nv, nu: (0,) * len(shape))
    dst_spec = lambda k: pl.BlockSpec((TM,), lambda i, ea, eb, nv, nu: (i * k_tiles + k,),
                                      memory_space=pltpu.SMEM)
    x_spec = lambda k: pl.BlockSpec((TM * ROW_TILES, LANES), lambda i, ea, eb, nv, nu: (i * k_tiles + k, 0))
    wgu_spec = lambda k, which: pl.BlockSpec(
        (1, 1, D_MODEL, D_EXPERT),
        lambda i, ea, eb, nv, nu: (layer, (ea, eb)[which][i * k_tiles + k], 0, 0))
    wd_spec = lambda k, which: pl.BlockSpec(
        (1, 1, D_EXPERT, D_MODEL),
        lambda i, ea, eb, nv, nu: (layer, (ea, eb)[which][i * k_tiles + k], 0, 0))
    tiles = range(k_tiles)
    weight_specs, weight_args = [], []
    for k in tiles:
        weight_specs += [wgu_spec(k, 0), wgu_spec(k, 0), wd_spec(k, 0),
                         wgu_spec(k, 1), wgu_spec(k, 1), wd_spec(k, 1)]
        weight_args += [wg, wu, wd, wg, wu, wd]
    grid_spec = pltpu.PrefetchScalarGridSpec(
        num_scalar_prefetch=4,
        grid=(n_tiles // k_tiles,),
        in_specs=([dst_spec(k) for k in tiles] + [x_spec(k) for k in tiles]
                  + [const((D_MODEL, LANES)), const((1, LANES))] + weight_specs
                  + [const((1, D_MODEL)), const((1, D_MODEL))]),
        out_specs=pl.BlockSpec(memory_space=pl.ANY),
        scratch_shapes=[pltpu.VMEM((2, k_tiles * TM * ROW_TILES, LANES), F32),
                        pltpu.SemaphoreType.DMA((2,))],
    )
    return pl.pallas_call(
        _expert_kernel,
        out_shape=jax.ShapeDtypeStruct((t * ROW_TILES, LANES), F32),
        grid_spec=grid_spec,
        compiler_params=pltpu.CompilerParams(dimension_semantics=("arbitrary",),
                                             vmem_limit_bytes=VMEM_LIMIT_BYTES),
        name="moe_experts",
    )(tile_ea, tile_eb, tile_nvalid, n_used, *([dst] * k_tiles), *([xs] * k_tiles),
      w_router, b_router, *weight_args, ln_g, ln_b)


def _relayout_kernel(h_ref, out_ref):
    out_ref[...] = _load_token_major(h_ref, TS)


def _rows_from_token_major(h, t):
    return pl.pallas_call(
        _relayout_kernel,
        out_shape=jax.ShapeDtypeStruct((t, D_MODEL), F32),
        grid=(t // TS,),
        in_specs=[pl.BlockSpec((TS * ROW_TILES, LANES), lambda i: (i, 0))],
        out_specs=pl.BlockSpec((TS, D_MODEL), lambda i: (i, 0)),
        name="rows_from_token_major",
    )(h)


def _class_expert_tables():
    first, second = [], []
    for g in range(N_GROUPS):
        for lo in range(EXPERTS_PER_GROUP):
            for hi in range(lo + 1, EXPERTS_PER_GROUP):
                first.append(g * EXPERTS_PER_GROUP + lo)
                second.append(g * EXPERTS_PER_GROUP + hi)
    return jnp.asarray(np.array(first, np.int32)), jnp.asarray(np.array(second, np.int32))


def _router_weights(wg, bg, we, be):
    w_router = jnp.zeros((D_MODEL, LANES), F32)
    w_router = w_router.at[:, :N_GROUPS].set(wg).at[:, N_GROUPS:N_GROUPS + N_EXPERTS].set(we)
    b_router = jnp.zeros((1, LANES), F32)
    b_router = b_router.at[0, :N_GROUPS].set(bg).at[0, N_GROUPS:N_GROUPS + N_EXPERTS].set(be)
    w_hi = w_router.astype(BF16)
    w_lo = (w_router - w_hi.astype(F32)).astype(BF16)
    return w_hi, w_lo, b_router


def _moe_layer(h, t, meta, counts, w_hi, b_router, wg, wu, wd, layer, ln_g, ln_b):
    n_tiles = t // TM + N_CLASSES
    n_rows = n_tiles * TM

    cnt = counts[0]
    tiles_per = (cnt + TM - 1) // TM
    tile_end = jnp.cumsum(tiles_per)
    offs = (tile_end - tiles_per) * TM
    n_used = tile_end[-1]
    tile_id = jnp.arange(n_tiles, dtype=jnp.int32)
    tile_cls = jnp.sum((jnp.minimum(tile_id, n_used - 1)[:, None] >= tile_end[None, :N_CLASSES])
                       .astype(jnp.int32), axis=1)
    seg_tile = tile_id - (tile_end - tiles_per)[tile_cls]
    tile_nvalid = jnp.where(tile_id < n_used, jnp.clip(cnt[tile_cls] - seg_tile * TM, 0, TM), 0)
    first, second = _class_expert_tables()
    xs, dst = _dispatch(h, offs, cnt, n_used.reshape(1), meta[0], meta[1], n_rows)
    return _experts(xs, t, dst, first[tile_cls], second[tile_cls], tile_nvalid, n_used.reshape(1),
                    w_hi, b_router, wg, wu, wd, layer, ln_g, ln_b)


def kernel(x, positions, w_in, sc_conv_w, sc_conv_b, lru_conv_w, lru_conv_b, lru_w_r, lru_b_r,
           lru_w_i, lru_b_i, lru_lambda, sg_norm_g, sg_w_s, sg_b_s, ret_norm_g, branch_proj, w_out,
           ln_mix_g, ln_mix_b, router_group_w, router_group_b, router_expert_w, router_expert_b,
           exp_w_gate, exp_w_up, exp_w_down, ln_ffn_g, ln_ffn_b):
    batch, seq, d = x.shape
    assert d == D_MODEL and seq % TS == 0 and w_in.shape[-1] == N_IN
    depth = w_in.shape[0]
    t = batch * seq
    cos_t, sin_t = _rope_tables(positions)
    dec_all, qd_tab, kd_tab, cd_tab = _retention_tables()
    avg = jnp.asarray(np.kron(np.eye(N_HEADS, dtype=np.float32),
                              np.full((HEAD_DIM, HEAD_DIM), 1.0 / HEAD_DIM, np.float32)), BF16)

    w_in_b, bp_b, wo_b = w_in.astype(BF16), branch_proj.astype(BF16), w_out.astype(BF16)
    wg_b, wu_b, wd_b = exp_w_gate.astype(BF16), exp_w_up.astype(BF16), exp_w_down.astype(BF16)

    h = x.reshape(t, d)
    for l in range(depth):
        w_ri = jnp.concatenate([_block_diag(lru_w_r[l]), _block_diag(lru_w_i[l])], axis=1).astype(BF16)
        b_ri = jnp.concatenate([lru_b_r[l], lru_b_i[l]])[None, :]
        bs_tab = jnp.repeat(sg_b_s[l].T, HEAD_DIM, axis=1)
        per_call = lambda *arrs: [(a, None) for a in arrs]
        w_hi, w_lo, b_router = _router_weights(router_group_w[l], router_group_b[l],
                                               router_expert_w[l], router_expert_b[l])
        consts = (
            [(w_in_b, l)]
            + per_call(sc_conv_w[l], sc_conv_b[l][None, :], lru_conv_w[l], lru_conv_b[l][None, :],
                       w_ri, b_ri, lru_lambda[l][None, :], sg_norm_g[l][None, :], sg_w_s[l], bs_tab,
                       ret_norm_g[l][None, :], dec_all, qd_tab, kd_tab, cd_tab, avg)
            + [(bp_b, l), (wo_b, l)]
            + per_call(ln_mix_g[l][None, :], ln_mix_b[l][None, :], w_hi, w_lo, b_router))
        h, meta, counts = _mixer_layer(h, cos_t, sin_t, consts, batch, seq, token_major_in=l > 0)
        h = _moe_layer(h, t, meta, counts, w_hi, b_router, wg_b, wu_b, wd_b, l,
                       ln_ffn_g[l][None, :], ln_ffn_b[l][None, :])
    return _rows_from_token_major(h, t).reshape(batch, seq, d)
```

```python
import functools

import jax
import jax.numpy as jnp
import numpy as np
from jax import lax
from jax.experimental import pallas as pl
from jax.experimental.pallas import tpu as pltpu

F32 = jnp.float32
BF16 = jnp.bfloat16

LANES = 128
SUBLANES = 8
VMEM_LIMIT_BYTES = 56 * 1024 * 1024

D_MODEL = 1024
ROW_TILES = D_MODEL // LANES
assert ROW_TILES == SUBLANES
WIDTH = D_MODEL // 4
N_HEADS = 4
N_BRANCH = 4
HEAD_DIM = WIDTH // N_HEADS
CHUNK = 128
SC_K = 3
LRU_K = 4
LRU_C = 8.0
ROPE_BASE = 10000.0
N_GROUPS = 4
EXPERTS_PER_GROUP = 8
N_EXPERTS = N_GROUPS * EXPERTS_PER_GROUP
D_EXPERT = D_MODEL // 4
LN_EPS = 1e-5
DEPTH = 2
ALPHA = (2.0 * DEPTH) ** 0.25

COL_SC_B, COL_SC_C, COL_SC_X, COL_LRU, COL_SG_U, COL_SG_V, COL_Q, COL_K, COL_V, COL_G = (
    i * WIDTH for i in range(10))
COL_GATES = 10 * WIDTH
N_IN = COL_GATES + 4 * D_MODEL

TS = 512
GATE_CHUNK = 256
SCAN_PAD = TS // 2
CONV_PAD = SUBLANES
TB_DISPATCH = 512
DISPATCH_UNROLL = 8
TM = 128
TILES_PER_STEP = 4
PAIRS_PER_GROUP = EXPERTS_PER_GROUP * (EXPERTS_PER_GROUP - 1) // 2
N_CLASSES = N_GROUPS * PAIRS_PER_GROUP
assert N_CLASSES <= LANES


def _sigmoid(x):
    return 0.5 * jnp.tanh(0.5 * x) + 0.5


def _dot(a, b):
    return jnp.dot(a, b, preferred_element_type=F32)


def _load_token_major(ref, rows, lead=()):
    return jnp.concatenate(
        [ref[lead + (pl.ds(c, rows, stride=ROW_TILES), slice(None))] for c in range(ROW_TILES)], axis=1)


def _store_token_major(ref, y, rows, lead=()):
    for c in range(ROW_TILES):
        ref[lead + (pl.ds(c, rows, stride=ROW_TILES), slice(None))] = y[:, c * LANES:(c + 1) * LANES]


def _layer_norm_rows(y, g, b):
    mu = jnp.mean(y, axis=-1, keepdims=True)
    yc = y - mu
    var = jnp.mean(yc * yc, axis=-1, keepdims=True)
    return yc * lax.rsqrt(var + LN_EPS) * g + b


def _rope_table_kernel(pos_ref, freq_ref, cos_ref, sin_ref):
    ang = pos_ref[...].astype(F32) * freq_ref[...]
    lane = lax.broadcasted_iota(jnp.int32, ang.shape, 1)
    first_half = (lane % HEAD_DIM) < (HEAD_DIM // 2)
    cos_ref[...] = jnp.cos(ang)
    s = jnp.sin(ang)
    sin_ref[...] = jnp.where(first_half, -s, s)


def _rope_tables(positions):
    t = positions.size
    half = HEAD_DIM // 2
    inv_freq = ROPE_BASE ** (-jnp.arange(half, dtype=F32) / half)
    freq_row = jnp.tile(inv_freq, LANES // half)[None, :]
    rows = 1024
    return pl.pallas_call(
        _rope_table_kernel,
        out_shape=(jax.ShapeDtypeStruct((t, LANES), F32), jax.ShapeDtypeStruct((t, LANES), F32)),
        grid=(t // rows,),
        in_specs=[pl.BlockSpec((rows, 1), lambda i: (i, 0)),
                  pl.BlockSpec((1, LANES), lambda i: (0, 0))],
        out_specs=(pl.BlockSpec((rows, LANES), lambda i: (i, 0)),
                   pl.BlockSpec((rows, LANES), lambda i: (i, 0))),
        name="rope_tables",
    )(positions.reshape(t, 1), freq_row)


def _mixer_kernel(h_ref, cos_ref, sin_ref, w_in_ref, scw_ref, scb_ref, lcw_ref, lcb_ref,
                  wri_ref, bri_ref, lam_ref, sgg_ref, ws_ref, bs_ref, rg_ref,
                  dec_ref, qd_ref, kd_ref, cd_ref, avg_ref, bp_ref, wo_ref, lng_ref, lnb_ref,
                  rwhi_ref, rwlo_ref, rb_ref,
                  out_ref, meta_ref, cnt_ref,
                  ubuf, xbuf, hcar, sbd, abuf, hbuf, gbuf, rcount, tri, *, token_major_in):
    step = pl.program_id(1)

    @pl.when((pl.program_id(0) == 0) & (step == 0))
    def _():
        rcount[...] = jnp.zeros_like(rcount)
        r_i = lax.broadcasted_iota(jnp.int32, (TS, TS), 0)
        c_i = lax.broadcasted_iota(jnp.int32, (TS, TS), 1)
        tri[...] = jnp.where(r_i > c_i, 1.0, 0.0).astype(BF16)

    @pl.when(step == 0)
    def _():
        ubuf[0:CONV_PAD, :] = jnp.zeros((CONV_PAD, WIDTH), F32)
        xbuf[0:CONV_PAD, :] = jnp.zeros((CONV_PAD, WIDTH), F32)
        hcar[...] = jnp.zeros_like(hcar)
        sbd[...] = jnp.zeros_like(sbd)
        abuf[0:SCAN_PAD, :] = jnp.ones((SCAN_PAD, WIDTH), F32)
        hbuf[0:SCAN_PAD, :] = jnp.zeros((SCAN_PAD, WIDTH), F32)

    h = _load_token_major(h_ref, TS) if token_major_in else h_ref[...]
    hb = h.astype(BF16)

    def proj(col, width=WIDTH):
        return _dot(hb, w_in_ref[0, :, col:col + width])

    pending_gate_cols = list(range(0, N_BRANCH * D_MODEL, GATE_CHUNK))

    def emit_gates(n_chunks=1):
        for _ in range(n_chunks):
            if pending_gate_cols:
                col = pending_gate_cols.pop(0)
                gbuf[:, col:col + GATE_CHUNK] = _sigmoid(proj(COL_GATES + col, GATE_CHUNK))

    lane = lax.broadcasted_iota(jnp.int32, (CHUNK, WIDTH), 1)
    head_of_lane = lane // HEAD_DIM
    avg = avg_ref[...]

    def group_standardize(x):
        mean = _dot(x.astype(BF16), avg)
        xc = x - mean
        var = _dot((xc * xc).astype(BF16), avg)
        return xc * lax.rsqrt(var + LN_EPS)

    sc_b, sc_c, sc_x = proj(COL_SC_B), proj(COL_SC_C), proj(COL_SC_X)
    u = sc_c * sc_x
    ubuf[CONV_PAD:CONV_PAD + TS, :] = u
    conv = scw_ref[2:3, :] * u + scb_ref[...]
    for j in range(SC_K - 1):
        back = SC_K - 1 - j
        conv = conv + scw_ref[j:j + 1, :] * ubuf[CONV_PAD - back:CONV_PAD - back + TS, :]
    ubuf[0:CONV_PAD, :] = ubuf[TS:TS + CONV_PAD, :]
    branch_a = sc_b * conv
    emit_gates()

    lx = proj(COL_LRU)
    xbuf[CONV_PAD:CONV_PAD + TS, :] = lx
    xc = lcw_ref[LRU_K - 1:LRU_K, :] * lx + lcb_ref[...]
    for j in range(LRU_K - 1):
        back = LRU_K - 1 - j
        xc = xc + lcw_ref[j:j + 1, :] * xbuf[CONV_PAD - back:CONV_PAD - back + TS, :]
    xbuf[0:CONV_PAD, :] = xbuf[TS:TS + CONV_PAD, :]
    emit_gates()
    ri = _dot(xc.astype(BF16), wri_ref[...]) + bri_ref[...]
    r = _sigmoid(ri[:, :WIDTH])
    ig = _sigmoid(ri[:, WIDTH:])
    neg_lam = -lam_ref[...]
    softplus = jnp.maximum(neg_lam, 0.0) + jnp.log1p(jnp.exp(-jnp.abs(neg_lam)))
    log_a = (-LRU_C) * r * softplus
    a = jnp.exp(log_a)
    th = jnp.tanh(log_a)
    uu = jnp.sqrt((-2.0) * th / (1.0 - th)) * (ig * xc)
    row = lax.broadcasted_iota(jnp.int32, (TS, WIDTH), 0)
    uu = uu + jnp.where(row == 0, a * hcar[...], 0.0)
    abuf[SCAN_PAD:SCAN_PAD + TS, :] = a
    hbuf[SCAN_PAD:SCAN_PAD + TS, :] = uu
    shift = 1
    while shift < TS:
        a_cur = abuf[SCAN_PAD:SCAN_PAD + TS, :]
        h_prev = hbuf[SCAN_PAD - shift:SCAN_PAD - shift + TS, :]
        h_new = a_cur * h_prev + hbuf[SCAN_PAD:SCAN_PAD + TS, :]
        if shift * 2 < TS:
            a_new = a_cur * abuf[SCAN_PAD - shift:SCAN_PAD - shift + TS, :]
            abuf[SCAN_PAD:SCAN_PAD + TS, :] = a_new
        hbuf[SCAN_PAD:SCAN_PAD + TS, :] = h_new
        emit_gates()
        shift *= 2
    branch_b = hbuf[SCAN_PAD:SCAN_PAD + TS, :]
    hcar[...] = hbuf[SCAN_PAD + TS - 1:SCAN_PAD + TS, :]

    gu = jax.nn.gelu(proj(COL_SG_U))
    gv = jax.nn.gelu(proj(COL_SG_V))
    emit_gates()
    vn = (group_standardize(gv) * sgg_ref[...]).astype(BF16)
    trow = lax.broadcasted_iota(jnp.int32, (CHUNK, CHUNK), 0)
    tcol = lax.broadcasted_iota(jnp.int32, (CHUNK, CHUNK), 1)
    w_causal = [jnp.where(trow >= tcol, ws_ref[g], 0.0).astype(BF16) for g in range(N_HEADS)]
    sv_chunks = []
    for c in range(TS // CHUNK):
        vch = vn[c * CHUNK:(c + 1) * CHUNK, :]
        sv = bs_ref[...]
        for g in range(N_HEADS):
            sv = sv + jnp.where(head_of_lane == g, _dot(w_causal[g], vch), 0.0)
        sv_chunks.append(sv)
    branch_c = gu * jnp.concatenate(sv_chunks, axis=0)
    emit_gates()

    cos = jnp.concatenate([cos_ref[...]] * (WIDTH // LANES), axis=1)
    sin = jnp.concatenate([sin_ref[...]] * (WIDTH // LANES), axis=1)
    lane_ts = lax.broadcasted_iota(jnp.int32, (TS, WIDTH), 1)
    first_half = (lane_ts % HEAD_DIM) < (HEAD_DIM // 2)

    def rope(t):
        swapped = jnp.where(first_half,
                            pltpu.roll(t, WIDTH - HEAD_DIM // 2, 1),
                            pltpu.roll(t, HEAD_DIM // 2, 1))
        return t * cos + swapped * sin

    q = rope(proj(COL_Q))
    k = rope(proj(COL_K)) * (HEAD_DIM ** -0.5)
    emit_gates()
    v = proj(COL_V)
    zg = proj(COL_G)
    bd_mask = (lax.broadcasted_iota(jnp.int32, (WIDTH, WIDTH), 0) // HEAD_DIM
               == lax.broadcasted_iota(jnp.int32, (WIDTH, WIDTH), 1) // HEAD_DIM)
    o_chunks = []
    for c in range(TS // CHUNK):
        sl = slice(c * CHUNK, (c + 1) * CHUNK)
        qc, kc, vc = q[sl, :], k[sl, :], v[sl, :]
        qcb = qc.astype(BF16)
        kstack = jnp.concatenate(
            [jnp.where(head_of_lane == hh, kc, 0.0) for hh in range(N_HEADS)], axis=0).astype(BF16)
        vstack = jnp.concatenate(
            [jnp.where(head_of_lane == hh, vc, 0.0) for hh in range(N_HEADS)], axis=0).astype(BF16)
        scores = lax.dot_general(qcb, kstack, (((1,), (1,)), ((), ())),
                                 preferred_element_type=F32)
        inner = _dot((scores * dec_ref[...]).astype(BF16), vstack)
        state = sbd[...]
        cross = _dot(qcb, state.astype(BF16)) * qd_ref[...]
        o_chunks.append(inner + cross)
        kdec = (kc * kd_ref[...]).astype(BF16)
        kv = lax.dot_general(kdec, vc.astype(BF16), (((0,), (0,)), ((), ())),
                             preferred_element_type=F32)
        sbd[...] = cd_ref[...] * state + jnp.where(bd_mask, kv, 0.0)
        emit_gates()
    o = jnp.concatenate(o_chunks, axis=0)
    branch_d = (zg * _sigmoid(zg)) * (group_standardize(o) * rg_ref[...])

    emit_gates(len(pending_gate_cols))
    merged = jnp.zeros((TS, D_MODEL), F32)
    for b_idx, branch in enumerate((branch_a, branch_b, branch_c, branch_d)):
        gate = gbuf[:, b_idx * D_MODEL:(b_idx + 1) * D_MODEL]
        merged = merged + gate * _dot(branch.astype(BF16), bp_ref[0, b_idx])
    mix = _dot(merged.astype(BF16), wo_ref[0])
    h_mid = _layer_norm_rows(ALPHA * h + mix, lng_ref[...], lnb_ref[...])
    _store_token_major(out_ref, h_mid, TS)
    _route(h_mid, rwhi_ref, rwlo_ref, rb_ref, tri, rcount, meta_ref, cnt_ref)


def _const_spec(arr, layer):
    if layer is None:
        return pl.BlockSpec(arr.shape, lambda b, i, _nd=arr.ndim: (0,) * _nd, pipeline_mode=pl.Buffered(1))
    return pl.BlockSpec((1,) + arr.shape[1:], lambda b, i, _nd=arr.ndim: (layer,) + (0,) * (_nd - 1),
                        pipeline_mode=pl.Buffered(1))


def _mixer_layer(h, cos_t, sin_t, consts, batch, seq, token_major_in):
    t = batch * seq
    steps = seq // TS
    row_map = lambda b, i: (b * steps + i, 0)
    tm_spec = pl.BlockSpec((TS * ROW_TILES, LANES), row_map)
    in_specs = [tm_spec if token_major_in else pl.BlockSpec((TS, D_MODEL), row_map),
                pl.BlockSpec((TS, LANES), row_map),
                pl.BlockSpec((TS, LANES), row_map)]
    in_specs += [_const_spec(arr, layer) for arr, layer in consts]
    return pl.pallas_call(
        functools.partial(_mixer_kernel, token_major_in=token_major_in),
        out_shape=(jax.ShapeDtypeStruct((t * ROW_TILES, LANES), F32),
                   jax.ShapeDtypeStruct((SUBLANES, t), jnp.int32),
                   jax.ShapeDtypeStruct((SUBLANES, LANES), jnp.int32)),
        grid=(batch, steps),
        in_specs=in_specs,
        out_specs=(tm_spec,
                   pl.BlockSpec((SUBLANES, TS), lambda b, i: (0, b * steps + i)),
                   pl.BlockSpec((SUBLANES, LANES), lambda b, i: (0, 0))),
        scratch_shapes=[
            pltpu.VMEM((CONV_PAD + TS, WIDTH), F32),
            pltpu.VMEM((CONV_PAD + TS, WIDTH), F32),
            pltpu.VMEM((1, WIDTH), F32),
            pltpu.VMEM((WIDTH, WIDTH), F32),
            pltpu.VMEM((SCAN_PAD + TS, WIDTH), F32),
            pltpu.VMEM((SCAN_PAD + TS, WIDTH), F32),
            pltpu.VMEM((TS, N_BRANCH * D_MODEL), F32),
            pltpu.VMEM((1, LANES), F32),
            pltpu.VMEM((TS, TS), BF16),
        ],
        compiler_params=pltpu.CompilerParams(
            dimension_semantics=("arbitrary", "arbitrary"),
            vmem_limit_bytes=VMEM_LIMIT_BYTES),
        name="mixer_layer",
    )(h, cos_t, sin_t, *[arr for arr, _ in consts])


def _block_diag(w):
    heads, d, e = w.shape
    eye = jnp.eye(heads, dtype=w.dtype)
    return (eye[:, None, :, None] * w[:, :, None, :]).reshape(heads * d, heads * e)


def _retention_tables():
    f32 = np.float32
    log_gamma = np.log1p(-np.exp2(f32(-5.0) - np.arange(N_HEADS, dtype=f32))).astype(f32)
    pos = np.arange(CHUNK, dtype=f32)
    diff = pos[:, None] - pos[None, :]
    decay = np.where(diff >= 0, np.exp(np.maximum(diff, f32(0)) * log_gamma[:, None, None]), f32(0)).astype(f32)
    dec_all = np.transpose(decay, (1, 0, 2)).reshape(CHUNK, N_HEADS * CHUNK)
    q_decay = np.exp((pos + f32(1))[:, None] * log_gamma).astype(f32)
    k_decay = np.exp((f32(CHUNK - 1) - pos)[:, None] * log_gamma).astype(f32)
    qd_tab = np.repeat(q_decay, HEAD_DIM, axis=1)
    kd_tab = np.repeat(k_decay, HEAD_DIM, axis=1)
    chunk_decay = np.repeat(np.exp(f32(CHUNK) * log_gamma).astype(f32), HEAD_DIM)
    cd_tab = np.ascontiguousarray(np.broadcast_to(chunk_decay[:, None], (WIDTH, WIDTH)))
    return tuple(jnp.asarray(a, F32) for a in (dec_all, qd_tab, kd_tab, cd_tab))


def _route(h, whi_ref, wlo_ref, b_ref, tri_ref, count_ref, meta_ref, cnt_ref):
    h_hi = h.astype(BF16)
    h_lo = (h - h_hi.astype(F32)).astype(BF16)
    whi = whi_ref[...]
    logits = _dot(h_hi, whi) + _dot(h_lo, whi) + _dot(h_hi, wlo_ref[...]) + b_ref[...]
    lane = lax.broadcasted_iota(jnp.int32, logits.shape, 1).astype(F32)
    big = F32(2 ** 30)
    neg_inf = F32(-jnp.inf)

    gl = jnp.where(lane < N_GROUPS, logits, neg_inf)
    gmax = jnp.max(gl, axis=-1, keepdims=True)
    g_idx = jnp.min(jnp.where(gl == gmax, lane, big), axis=-1, keepdims=True)

    e_lane = lane - N_GROUPS
    in_group = (e_lane >= g_idx * EXPERTS_PER_GROUP) & (e_lane < (g_idx + 1) * EXPERTS_PER_GROUP)
    el = jnp.where(in_group, logits, neg_inf)
    m1 = jnp.max(el, axis=-1, keepdims=True)
    i1 = jnp.min(jnp.where(el == m1, lane, big), axis=-1, keepdims=True)
    el2 = jnp.where(lane == i1, neg_inf, el)
    m2 = jnp.max(el2, axis=-1, keepdims=True)
    i2 = jnp.min(jnp.where(el2 == m2, lane, big), axis=-1, keepdims=True)
    base = g_idx * EXPERTS_PER_GROUP + N_GROUPS
    lo = jnp.minimum(i1, i2) - base
    hi = jnp.maximum(i1, i2) - base
    pair = lo * (2.0 * EXPERTS_PER_GROUP - 1.0 - lo) * 0.5 + (hi - lo - 1.0)
    cls = g_idx * PAIRS_PER_GROUP + pair

    onehot = jnp.where(lane == cls, 1.0, 0.0)
    before = _dot(tri_ref[...], onehot.astype(BF16)) + count_ref[...]
    rank = jnp.sum(jnp.where(lane == cls, before, 0.0), axis=-1, keepdims=True)
    count_ref[...] = count_ref[...] + jnp.sum(onehot, axis=0, keepdims=True)

    meta = jnp.where(lane == 0, cls, 0.0) + jnp.where(lane == 1, rank, 0.0)
    meta_ref[...] = jnp.transpose(meta)[0:SUBLANES, :].astype(jnp.int32)
    cnt_ref[...] = jnp.broadcast_to(count_ref[...], cnt_ref.shape).astype(jnp.int32)


def _token_rows(tok):
    if isinstance(tok, int):
        return pl.ds(tok * ROW_TILES, ROW_TILES)
    return pl.ds(pl.multiple_of(tok * ROW_TILES, ROW_TILES), ROW_TILES)


def _dispatch_kernel(offs_ref, cnt_ref, nused_ref, cls_ref, rank_ref, h_ref, dst0_hbm, xs_hbm, dst_hbm,
                     dst_ref, zbuf, row_sem, map_sem, zero_sem, *, n_tiles):
    step = pl.program_id(0)
    pad_bits = [1 << b for b in reversed(range(TM.bit_length() - 1))]

    def class_padding(c):
        cnt = cnt_ref[c]
        return offs_ref[c] + cnt, (TM - cnt % TM) % TM

    def zero_rows(first_row, n_rows):
        start = first_row * ROW_TILES
        if not isinstance(start, int):
            start = pl.multiple_of(start, ROW_TILES)
        return pltpu.make_async_copy(zbuf.at[pl.ds(0, n_rows * ROW_TILES), :],
                                     xs_hbm.at[pl.ds(start, n_rows * ROW_TILES), :], zero_sem)

    @pl.when(step == 0)
    def _():
        load = pltpu.make_async_copy(dst0_hbm, dst_ref, map_sem)
        load.start()
        load.wait()

        zbuf[...] = jnp.zeros_like(zbuf)

        def fill_class(c, _):
            first_row, n_pad = class_padding(c)
            for bit in pad_bits:
                @pl.when((n_pad & bit) != 0)
                def _(bit=bit):
                    zero_rows(first_row + (n_pad & ~(2 * bit - 1)), bit).start()
            return 0

        lax.fori_loop(0, N_CLASSES, fill_class, 0)

        def fill_tile(j, _):
            zero_rows(j * TM, TM).start()
            return 0

        lax.fori_loop(nused_ref[0], n_tiles, fill_tile, 0)

    def send(g, _):
        for k in range(DISPATCH_UNROLL):
            r = g * DISPATCH_UNROLL + k
            p = offs_ref[cls_ref[r]] + rank_ref[r]
            dst_ref[p] = step * TB_DISPATCH + r
            pltpu.make_async_copy(h_ref.at[_token_rows(r), :], xs_hbm.at[_token_rows(p), :],
                                  row_sem).start(priority=k % 2)
        return 0

    lax.fori_loop(0, TB_DISPATCH // DISPATCH_UNROLL, send, 0)

    pltpu.make_async_copy(h_ref, xs_hbm.at[pl.ds(0, TB_DISPATCH * ROW_TILES), :], row_sem).wait()

    @pl.when(step == pl.num_programs(0) - 1)
    def _():
        store = pltpu.make_async_copy(dst_ref, dst_hbm, map_sem)
        store.start()
        store.wait()

        def drain_class(c, _):
            _, n_pad = class_padding(c)

            @pl.when(n_pad > 0)
            def _():
                zero_rows(0, n_pad).wait()
            return 0

        lax.fori_loop(0, N_CLASSES, drain_class, 0)

        def drain_tile(j, _):
            zero_rows(0, TM).wait()
            return 0

        lax.fori_loop(nused_ref[0], n_tiles, drain_tile, 0)


def _dispatch(h, offs, cnt, n_used, cls, rank, n_rows):
    t = cls.shape[0]
    dst0 = jnp.zeros((n_rows,), jnp.int32)
    smem_blk = pl.BlockSpec((TB_DISPATCH,), lambda i, *_: (i,), memory_space=pltpu.SMEM)
    hbm = pl.BlockSpec(memory_space=pl.ANY)
    return pl.pallas_call(
        functools.partial(_dispatch_kernel, n_tiles=n_rows // TM),
        out_shape=(jax.ShapeDtypeStruct((n_rows * ROW_TILES, LANES), F32),
                   jax.ShapeDtypeStruct((n_rows,), jnp.int32)),
        grid_spec=pltpu.PrefetchScalarGridSpec(
            num_scalar_prefetch=3,
            grid=(t // TB_DISPATCH,),
            in_specs=[smem_blk, smem_blk,
                      pl.BlockSpec((TB_DISPATCH * ROW_TILES, LANES), lambda i, *_: (i, 0)),
                      hbm],
            out_specs=(hbm, hbm),
            scratch_shapes=[pltpu.SMEM((n_rows,), jnp.int32),
                            pltpu.VMEM((TM * ROW_TILES, LANES), F32),
                            pltpu.SemaphoreType.DMA,
                            pltpu.SemaphoreType.DMA,
                            pltpu.SemaphoreType.DMA]),
        compiler_params=pltpu.CompilerParams(dimension_semantics=("arbitrary",)),
        name="moe_dispatch",
    )(offs, cnt, n_used, cls, rank, h, dst0)


def _expert_kernel(ea_ref, eb_ref, nvalid_ref, nused_ref, *refs):
    k_tiles = TILES_PER_STEP
    dst_refs, x_refs = refs[:k_tiles], refs[k_tiles:2 * k_tiles]
    wr_ref, br_ref = refs[2 * k_tiles:2 * k_tiles + 2]
    w_refs = refs[2 * k_tiles + 2:8 * k_tiles + 2]
    lng_ref, lnb_ref, out_hbm, ybuf, ssem = refs[8 * k_tiles + 2:]
    step = pl.program_id(0)
    n_used = nused_ref[0]
    first_tile = step * k_tiles
    slot = step % 2
    other = 1 - slot

    def step_rows(s):
        base = jnp.maximum(s, 0) * k_tiles
        return sum(nvalid_ref[base + k] for k in range(k_tiles))

    def start_rows(k, n, s):
        for r in range(TM):
            token = dst_refs[k][r]

            @pl.when(r < n)
            def _(r=r, token=token):
                pltpu.make_async_copy(ybuf.at[s, _token_rows(k * TM + r), :],
                                      out_hbm.at[_token_rows(token), :],
                                      ssem.at[s]).start(priority=r % 2)

    def wait_rows(n, s):
        @pl.when(n > 0)
        def _():
            rows = pl.ds(0, n * ROW_TILES)
            pltpu.make_async_copy(ybuf.at[s, rows, :], out_hbm.at[rows, :], ssem.at[s]).wait()

    def tile_result(k):
        x_ref = x_refs[k]
        wga_ref, wua_ref, wda_ref, wgb_ref, wub_ref, wdb_ref = w_refs[6 * k:6 * k + 6]
        xb = _load_token_major(x_ref, TM).astype(BF16)

        logits = _dot(xb, wr_ref[...]) + br_ref[...]
        lane = lax.broadcasted_iota(jnp.int32, logits.shape, 1)
        ea, eb = ea_ref[first_tile + k], eb_ref[first_tile + k]
        group = ea // EXPERTS_PER_GROUP
        gl = jnp.where(lane < N_GROUPS, logits, F32(-jnp.inf))
        gmax = jnp.max(gl, axis=-1, keepdims=True)
        gsum = jnp.sum(jnp.where(lane < N_GROUPS, jnp.exp(gl - gmax), 0.0), axis=-1, keepdims=True)
        lg = jnp.sum(jnp.where(lane == group, logits, 0.0), axis=-1, keepdims=True)
        g_top_p = jnp.exp(lg - gmax) / gsum
        la = jnp.sum(jnp.where(lane == ea + N_GROUPS, logits, 0.0), axis=-1, keepdims=True)
        lb = jnp.sum(jnp.where(lane == eb + N_GROUPS, logits, 0.0), axis=-1, keepdims=True)
        w_a = g_top_p / (1.0 + jnp.exp(lb - la))
        w_b = g_top_p / (1.0 + jnp.exp(la - lb))

        def hidden(wg_ref, wu_ref, weight):
            gate, up = _dot(xb, wg_ref[0, 0]), _dot(xb, wu_ref[0, 0])
            return ((gate * _sigmoid(gate)) * up * weight).astype(BF16)

        ffn = (_dot(hidden(wga_ref, wua_ref, w_a), wda_ref[0, 0])
               + _dot(hidden(wgb_ref, wub_ref, w_b), wdb_ref[0, 0]))
        return _layer_norm_rows(ALPHA * _load_token_major(x_ref, TM) + ffn, lng_ref[...], lnb_ref[...])

    @pl.when(first_tile < n_used)
    def _():
        @pl.when(step >= 2)
        def _():
            wait_rows(step_rows(step - 2), slot)

        for k in range(k_tiles):
            result = tile_result(k)
            for c in range(ROW_TILES):
                ybuf[slot, pl.ds(k * TM * ROW_TILES + c, TM, stride=ROW_TILES), :] = (
                    result[:, c * LANES:(c + 1) * LANES])
            start_rows(k, nvalid_ref[first_tile + k], slot)

        @pl.when(first_tile + k_tiles >= n_used)
        def _():
            @pl.when(step >= 1)
            def _():
                wait_rows(step_rows(step - 1), other)

            wait_rows(step_rows(step), slot)


def _experts(xs, t, dst, tile_ea, tile_eb, tile_nvalid, n_used, w_router, b_router, wg, wu, wd, layer,
             ln_g, ln_b):
    k_tiles = TILES_PER_STEP
    n_tiles = dst.shape[0] // TM
    assert n_tiles % k_tiles == 0
    const = lambda shape: pl.BlockSpec(shape, lambda i, ea, eb, nv, nu: (0,) * len(shape))
    dst_spec = lambda k: pl.BlockSpec((TM,), lambda i, ea, eb, nv, nu: (i * k_tiles + k,),
                                      memory_space=pltpu.SMEM)
    x_spec = lambda k: pl.BlockSpec((TM * ROW_TILES, LANES), lambda i, ea, eb, nv, nu: (i * k_tiles + k, 0))
    wgu_spec = lambda k, which: pl.BlockSpec(
        (1, 1, D_MODEL, D_EXPERT),
        lambda i, ea, eb, nv, nu: (layer, (ea, eb)[which][i * k_tiles + k], 0, 0))
    wd_spec = lambda k, which: pl.BlockSpec(
        (1, 1, D_EXPERT, D_MODEL),
        lambda i, ea, eb, nv, nu: (layer, (ea, eb)[which][i * k_tiles + k], 0, 0))
    tiles = range(k_tiles)
    weight_specs, weight_args = [], []
    for k in tiles:
        weight_specs += [wgu_spec(k, 0), wgu_spec(k, 0), wd_spec(k, 0),
                         wgu_spec(k, 1), wgu_spec(k, 1), wd_spec(k, 1)]
        weight_args += [wg, wu, wd, wg, wu, wd]
    grid_spec = pltpu.PrefetchScalarGridSpec(
        num_scalar_prefetch=4,
        grid=(n_tiles // k_tiles,),
        in_specs=([dst_spec(k) for k in tiles] + [x_spec(k) for k in tiles]
                  + [const((D_MODEL, LANES)), const((1, LANES))] + weight_specs
                  + [const((1, D_MODEL)), const((1, D_MODEL))]),
        out_specs=pl.BlockSpec(memory_space=pl.ANY),
        scratch_shapes=[pltpu.VMEM((2, k_tiles * TM * ROW_TILES, LANES), F32),
                        pltpu.SemaphoreType.DMA((2,))],
    )
    return pl.pallas_call(
        _expert_kernel,
        out_shape=jax.ShapeDtypeStruct((t * ROW_TILES, LANES), F32),
        grid_spec=grid_spec,
        compiler_params=pltpu.CompilerParams(dimension_semantics=("arbitrary",),
                                             vmem_limit_bytes=VMEM_LIMIT_BYTES),
        name="moe_experts",
    )(tile_ea, tile_eb, tile_nvalid, n_used, *([dst] * k_tiles), *([xs] * k_tiles),
      w_router, b_router, *weight_args, ln_g, ln_b)


def _relayout_kernel(h_ref, out_ref):
    out_ref[...] = _load_token_major(h_ref, TS)


def _rows_from_token_major(h, t):
    return pl.pallas_call(
        _relayout_kernel,
        out_shape=jax.ShapeDtypeStruct((t, D_MODEL), F32),
        grid=(t // TS,),
        in_specs=[pl.BlockSpec((TS * ROW_TILES, LANES), lambda i: (i, 0))],
        out_specs=pl.BlockSpec((TS, D_MODEL), lambda i: (i, 0)),
        name="rows_from_token_major",
    )(h)


def _class_expert_tables():
    first, second = [], []
    for g in range(N_GROUPS):
        for lo in range(EXPERTS_PER_GROUP):
            for hi in range(lo + 1, EXPERTS_PER_GROUP):
                first.append(g * EXPERTS_PER_GROUP + lo)
                second.append(g * EXPERTS_PER_GROUP + hi)
    return jnp.asarray(np.array(first, np.int32)), jnp.asarray(np.array(second, np.int32))


def _router_weights(wg, bg, we, be):
    w_router = jnp.zeros((D_MODEL, LANES), F32)
    w_router = w_router.at[:, :N_GROUPS].set(wg).at[:, N_GROUPS:N_GROUPS + N_EXPERTS].set(we)
    b_router = jnp.zeros((1, LANES), F32)
    b_router = b_router.at[0, :N_GROUPS].set(bg).at[0, N_GROUPS:N_GROUPS + N_EXPERTS].set(be)
    w_hi = w_router.astype(BF16)
    w_lo = (w_router - w_hi.astype(F32)).astype(BF16)
    return w_hi, w_lo, b_router


def _moe_layer(h, t, meta, counts, w_hi, b_router, wg, wu, wd, layer, ln_g, ln_b):
    n_tiles = t // TM + N_CLASSES
    n_rows = n_tiles * TM

    cnt = counts[0]
    tiles_per = (cnt + TM - 1) // TM
    tile_end = jnp.cumsum(tiles_per)
    offs = (tile_end - tiles_per) * TM
    n_used = tile_end[-1]
    tile_id = jnp.arange(n_tiles, dtype=jnp.int32)
    tile_cls = jnp.sum((jnp.minimum(tile_id, n_used - 1)[:, None] >= tile_end[None, :N_CLASSES])
                       .astype(jnp.int32), axis=1)
    seg_tile = tile_id - (tile_end - tiles_per)[tile_cls]
    tile_nvalid = jnp.where(tile_id < n_used, jnp.clip(cnt[tile_cls] - seg_tile * TM, 0, TM), 0)
    first, second = _class_expert_tables()
    xs, dst = _dispatch(h, offs, cnt, n_used.reshape(1), meta[0], meta[1], n_rows)
    return _experts(xs, t, dst, first[tile_cls], second[tile_cls], tile_nvalid, n_used.reshape(1),
                    w_hi, b_router, wg, wu, wd, layer, ln_g, ln_b)


def kernel(x, positions, w_in, sc_conv_w, sc_conv_b, lru_conv_w, lru_conv_b, lru_w_r, lru_b_r,
           lru_w_i, lru_b_i, lru_lambda, sg_norm_g, sg_w_s, sg_b_s, ret_norm_g, branch_proj, w_out,
           ln_mix_g, ln_mix_b, router_group_w, router_group_b, router_expert_w, router_expert_b,
           exp_w_gate, exp_w_up, exp_w_down, ln_ffn_g, ln_ffn_b):
    batch, seq, d = x.shape
    assert d == D_MODEL and seq % TS == 0 and w_in.shape[-1] == N_IN
    depth = w_in.shape[0]
    t = batch * seq
    cos_t, sin_t = _rope_tables(positions)
    dec_all, qd_tab, kd_tab, cd_tab = _retention_tables()
    avg = jnp.asarray(np.kron(np.eye(N_HEADS, dtype=np.float32),
                              np.full((HEAD_DIM, HEAD_DIM), 1.0 / HEAD_DIM, np.float32)), BF16)

    w_in_b, bp_b, wo_b = w_in.astype(BF16), branch_proj.astype(BF16), w_out.astype(BF16)
    wg_b, wu_b, wd_b = exp_w_gate.astype(BF16), exp_w_up.astype(BF16), exp_w_down.astype(BF16)

    h = x.reshape(t, d)
    for l in range(depth):
        w_ri = jnp.concatenate([_block_diag(lru_w_r[l]), _block_diag(lru_w_i[l])], axis=1).astype(BF16)
        b_ri = jnp.concatenate([lru_b_r[l], lru_b_i[l]])[None, :]
        bs_tab = jnp.repeat(sg_b_s[l].T, HEAD_DIM, axis=1)
        per_call = lambda *arrs: [(a, None) for a in arrs]
        w_hi, w_lo, b_router = _router_weights(router_group_w[l], router_group_b[l],
                                               router_expert_w[l], router_expert_b[l])
        consts = (
            [(w_in_b, l)]
            + per_call(sc_conv_w[l], sc_conv_b[l][None, :], lru_conv_w[l], lru_conv_b[l][None, :],
                       w_ri, b_ri, lru_lambda[l][None, :], sg_norm_g[l][None, :], sg_w_s[l], bs_tab,
                       ret_norm_g[l][None, :], dec_all, qd_tab, kd_tab, cd_tab, avg)
            + [(bp_b, l), (wo_b, l)]
            + per_call(ln_mix_g[l][None, :], ln_mix_b[l][None, :], w_hi, w_lo, b_router))
        h, meta, counts = _mixer_layer(h, cos_t, sin_t, consts, batch, seq, token_major_in=l > 0)
        h = _moe_layer(h, t, meta, counts, w_hi, b_router, wg_b, wu_b, wd_b, l,
                       ln_ffn_g[l][None, :], ln_ffn_b[l][None, :])
    return _rows_from_token_major(h, t).reshape(batch, seq, d)
```

```python
import functools

import jax
import jax.numpy as jnp
import numpy as np
from jax import lax
from jax.experimental import pallas as pl
from jax.experimental.pallas import tpu as pltpu

F32 = jnp.float32
BF16 = jnp.bfloat16

LANES = 128
SUBLANES = 8
VMEM_LIMIT_BYTES = 56 * 1024 * 1024

D_MODEL = 1024
ROW_TILES = D_MODEL // LANES
assert ROW_TILES == SUBLANES
WIDTH = D_MODEL // 4
N_HEADS = 4
N_BRANCH = 4
HEAD_DIM = WIDTH // N_HEADS
CHUNK = 128
SC_K = 3
LRU_K = 4
LRU_C = 8.0
ROPE_BASE = 10000.0
N_GROUPS = 4
EXPERTS_PER_GROUP = 8
N_EXPERTS = N_GROUPS * EXPERTS_PER_GROUP
D_EXPERT = D_MODEL // 4
LN_EPS = 1e-5
DEPTH = 2
ALPHA = (2.0 * DEPTH) ** 0.25

COL_SC_B, COL_SC_C, COL_SC_X, COL_LRU, COL_SG_U, COL_SG_V, COL_Q, COL_K, COL_V, COL_G = (
    i * WIDTH for i in range(10))
COL_GATES = 10 * WIDTH
N_IN = COL_GATES + 4 * D_MODEL

TS = 512
GATE_CHUNK = 256
SCAN_PAD = TS // 2
CONV_PAD = SUBLANES
TB_DISPATCH = 512
TM = 128
TILES_PER_STEP = 4
PAIRS_PER_GROUP = EXPERTS_PER_GROUP * (EXPERTS_PER_GROUP - 1) // 2
N_CLASSES = N_GROUPS * PAIRS_PER_GROUP
assert N_CLASSES <= LANES


def _sigmoid(x):
    return 0.5 * jnp.tanh(0.5 * x) + 0.5


def _dot(a, b):
    return jnp.dot(a, b, preferred_element_type=F32)


def _load_token_major(ref, rows, lead=()):
    return jnp.concatenate(
        [ref[lead + (pl.ds(c, rows, stride=ROW_TILES), slice(None))] for c in range(ROW_TILES)], axis=1)


def _store_token_major(ref, y, rows, lead=()):
    for c in range(ROW_TILES):
        ref[lead + (pl.ds(c, rows, stride=ROW_TILES), slice(None))] = y[:, c * LANES:(c + 1) * LANES]


def _layer_norm_rows(y, g, b):
    mu = jnp.mean(y, axis=-1, keepdims=True)
    yc = y - mu
    var = jnp.mean(yc * yc, axis=-1, keepdims=True)
    return yc * lax.rsqrt(var + LN_EPS) * g + b


def _rope_table_kernel(pos_ref, freq_ref, cos_ref, sin_ref):
    ang = pos_ref[...].astype(F32) * freq_ref[...]
    lane = lax.broadcasted_iota(jnp.int32, ang.shape, 1)
    first_half = (lane % HEAD_DIM) < (HEAD_DIM // 2)
    cos_ref[...] = jnp.cos(ang)
    s = jnp.sin(ang)
    sin_ref[...] = jnp.where(first_half, -s, s)


def _rope_tables(positions):
    t = positions.size
    half = HEAD_DIM // 2
    inv_freq = ROPE_BASE ** (-jnp.arange(half, dtype=F32) / half)
    freq_row = jnp.tile(inv_freq, LANES // half)[None, :]
    rows = 1024
    return pl.pallas_call(
        _rope_table_kernel,
        out_shape=(jax.ShapeDtypeStruct((t, LANES), F32), jax.ShapeDtypeStruct((t, LANES), F32)),
        grid=(t // rows,),
        in_specs=[pl.BlockSpec((rows, 1), lambda i: (i, 0)),
                  pl.BlockSpec((1, LANES), lambda i: (0, 0))],
        out_specs=(pl.BlockSpec((rows, LANES), lambda i: (i, 0)),
                   pl.BlockSpec((rows, LANES), lambda i: (i, 0))),
        name="rope_tables",
    )(positions.reshape(t, 1), freq_row)


def _mixer_kernel(h_ref, cos_ref, sin_ref, w_in_ref, scw_ref, scb_ref, lcw_ref, lcb_ref,
                  wri_ref, bri_ref, lam_ref, sgg_ref, ws_ref, bs_ref, rg_ref,
                  dec_ref, qd_ref, kd_ref, cd_ref, avg_ref, bp_ref, wo_ref, lng_ref, lnb_ref,
                  rwhi_ref, rwlo_ref, rb_ref,
                  out_ref, meta_ref, cnt_ref,
                  ubuf, xbuf, hcar, sbd, abuf, hbuf, gbuf, rcount, tri, *, token_major_in):
    step = pl.program_id(1)

    @pl.when((pl.program_id(0) == 0) & (step == 0))
    def _():
        rcount[...] = jnp.zeros_like(rcount)
        r_i = lax.broadcasted_iota(jnp.int32, (TS, TS), 0)
        c_i = lax.broadcasted_iota(jnp.int32, (TS, TS), 1)
        tri[...] = jnp.where(r_i > c_i, 1.0, 0.0).astype(BF16)

    @pl.when(step == 0)
    def _():
        ubuf[0:CONV_PAD, :] = jnp.zeros((CONV_PAD, WIDTH), F32)
        xbuf[0:CONV_PAD, :] = jnp.zeros((CONV_PAD, WIDTH), F32)
        hcar[...] = jnp.zeros_like(hcar)
        sbd[...] = jnp.zeros_like(sbd)
        abuf[0:SCAN_PAD, :] = jnp.ones((SCAN_PAD, WIDTH), F32)
        hbuf[0:SCAN_PAD, :] = jnp.zeros((SCAN_PAD, WIDTH), F32)

    h = _load_token_major(h_ref, TS) if token_major_in else h_ref[...]
    hb = h.astype(BF16)

    def proj(col, width=WIDTH):
        return _dot(hb, w_in_ref[0, :, col:col + width])

    pending_gate_cols = list(range(0, N_BRANCH * D_MODEL, GATE_CHUNK))

    def emit_gates(n_chunks=1):
        for _ in range(n_chunks):
            if pending_gate_cols:
                col = pending_gate_cols.pop(0)
                gbuf[:, col:col + GATE_CHUNK] = _sigmoid(proj(COL_GATES + col, GATE_CHUNK))

    lane = lax.broadcasted_iota(jnp.int32, (CHUNK, WIDTH), 1)
    head_of_lane = lane // HEAD_DIM
    avg = avg_ref[...]

    def group_standardize(x):
        mean = _dot(x.astype(BF16), avg)
        xc = x - mean
        var = _dot((xc * xc).astype(BF16), avg)
        return xc * lax.rsqrt(var + LN_EPS)

    sc_b, sc_c, sc_x = proj(COL_SC_B), proj(COL_SC_C), proj(COL_SC_X)
    u = sc_c * sc_x
    ubuf[CONV_PAD:CONV_PAD + TS, :] = u
    conv = scw_ref[2:3, :] * u + scb_ref[...]
    for j in range(SC_K - 1):
        back = SC_K - 1 - j
        conv = conv + scw_ref[j:j + 1, :] * ubuf[CONV_PAD - back:CONV_PAD - back + TS, :]
    ubuf[0:CONV_PAD, :] = ubuf[TS:TS + CONV_PAD, :]
    branch_a = sc_b * conv
    emit_gates()

    lx = proj(COL_LRU)
    xbuf[CONV_PAD:CONV_PAD + TS, :] = lx
    xc = lcw_ref[LRU_K - 1:LRU_K, :] * lx + lcb_ref[...]
    for j in range(LRU_K - 1):
        back = LRU_K - 1 - j
        xc = xc + lcw_ref[j:j + 1, :] * xbuf[CONV_PAD - back:CONV_PAD - back + TS, :]
    xbuf[0:CONV_PAD, :] = xbuf[TS:TS + CONV_PAD, :]
    emit_gates()
    ri = _dot(xc.astype(BF16), wri_ref[...]) + bri_ref[...]
    r = _sigmoid(ri[:, :WIDTH])
    ig = _sigmoid(ri[:, WIDTH:])
    neg_lam = -lam_ref[...]
    softplus = jnp.maximum(neg_lam, 0.0) + jnp.log1p(jnp.exp(-jnp.abs(neg_lam)))
    log_a = (-LRU_C) * r * softplus
    a = jnp.exp(log_a)
    th = jnp.tanh(log_a)
    uu = jnp.sqrt((-2.0) * th / (1.0 - th)) * (ig * xc)
    row = lax.broadcasted_iota(jnp.int32, (TS, WIDTH), 0)
    uu = uu + jnp.where(row == 0, a * hcar[...], 0.0)
    abuf[SCAN_PAD:SCAN_PAD + TS, :] = a
    hbuf[SCAN_PAD:SCAN_PAD + TS, :] = uu
    shift = 1
    while shift < TS:
        a_cur = abuf[SCAN_PAD:SCAN_PAD + TS, :]
        h_prev = hbuf[SCAN_PAD - shift:SCAN_PAD - shift + TS, :]
        h_new = a_cur * h_prev + hbuf[SCAN_PAD:SCAN_PAD + TS, :]
        if shift * 2 < TS:
            a_new = a_cur * abuf[SCAN_PAD - shift:SCAN_PAD - shift + TS, :]
            abuf[SCAN_PAD:SCAN_PAD + TS, :] = a_new
        hbuf[SCAN_PAD:SCAN_PAD + TS, :] = h_new
        emit_gates()
        shift *= 2
    branch_b = hbuf[SCAN_PAD:SCAN_PAD + TS, :]
    hcar[...] = hbuf[SCAN_PAD + TS - 1:SCAN_PAD + TS, :]

    gu = jax.nn.gelu(proj(COL_SG_U))
    gv = jax.nn.gelu(proj(COL_SG_V))
    emit_gates()
    vn = (group_standardize(gv) * sgg_ref[...]).astype(BF16)
    trow = lax.broadcasted_iota(jnp.int32, (CHUNK, CHUNK), 0)
    tcol = lax.broadcasted_iota(jnp.int32, (CHUNK, CHUNK), 1)
    w_causal = [jnp.where(trow >= tcol, ws_ref[g], 0.0).astype(BF16) for g in range(N_HEADS)]
    sv_chunks = []
    for c in range(TS // CHUNK):
        vch = vn[c * CHUNK:(c + 1) * CHUNK, :]
        sv = bs_ref[...]
        for g in range(N_HEADS):
            sv = sv + jnp.where(head_of_lane == g, _dot(w_causal[g], vch), 0.0)
        sv_chunks.append(sv)
    branch_c = gu * jnp.concatenate(sv_chunks, axis=0)
    emit_gates()

    cos = jnp.concatenate([cos_ref[...]] * (WIDTH // LANES), axis=1)
    sin = jnp.concatenate([sin_ref[...]] * (WIDTH // LANES), axis=1)
    lane_ts = lax.broadcasted_iota(jnp.int32, (TS, WIDTH), 1)
    first_half = (lane_ts % HEAD_DIM) < (HEAD_DIM // 2)

    def rope(t):
        swapped = jnp.where(first_half,
                            pltpu.roll(t, WIDTH - HEAD_DIM // 2, 1),
                            pltpu.roll(t, HEAD_DIM // 2, 1))
        return t * cos + swapped * sin

    q = rope(proj(COL_Q))
    k = rope(proj(COL_K)) * (HEAD_DIM ** -0.5)
    emit_gates()
    v = proj(COL_V)
    zg = proj(COL_G)
    bd_mask = (lax.broadcasted_iota(jnp.int32, (WIDTH, WIDTH), 0) // HEAD_DIM
               == lax.broadcasted_iota(jnp.int32, (WIDTH, WIDTH), 1) // HEAD_DIM)
    o_chunks = []
    for c in range(TS // CHUNK):
        sl = slice(c * CHUNK, (c + 1) * CHUNK)
        qc, kc, vc = q[sl, :], k[sl, :], v[sl, :]
        qcb = qc.astype(BF16)
        kstack = jnp.concatenate(
            [jnp.where(head_of_lane == hh, kc, 0.0) for hh in range(N_HEADS)], axis=0).astype(BF16)
        vstack = jnp.concatenate(
            [jnp.where(head_of_lane == hh, vc, 0.0) for hh in range(N_HEADS)], axis=0).astype(BF16)
        scores = lax.dot_general(qcb, kstack, (((1,), (1,)), ((), ())),
                                 preferred_element_type=F32)
        inner = _dot((scores * dec_ref[...]).astype(BF16), vstack)
        state = sbd[...]
        cross = _dot(qcb, state.astype(BF16)) * qd_ref[...]
        o_chunks.append(inner + cross)
        kdec = (kc * kd_ref[...]).astype(BF16)
        kv = lax.dot_general(kdec, vc.astype(BF16), (((0,), (0,)), ((), ())),
                             preferred_element_type=F32)
        sbd[...] = cd_ref[...] * state + jnp.where(bd_mask, kv, 0.0)
        emit_gates()
    o = jnp.concatenate(o_chunks, axis=0)
    branch_d = (zg * _sigmoid(zg)) * (group_standardize(o) * rg_ref[...])

    emit_gates(len(pending_gate_cols))
    merged = jnp.zeros((TS, D_MODEL), F32)
    for b_idx, branch in enumerate((branch_a, branch_b, branch_c, branch_d)):
        gate = gbuf[:, b_idx * D_MODEL:(b_idx + 1) * D_MODEL]
        merged = merged + gate * _dot(branch.astype(BF16), bp_ref[0, b_idx])
    mix = _dot(merged.astype(BF16), wo_ref[0])
    h_mid = _layer_norm_rows(ALPHA * h + mix, lng_ref[...], lnb_ref[...])
    _store_token_major(out_ref, h_mid, TS)
    _route(h_mid, rwhi_ref, rwlo_ref, rb_ref, tri, rcount, meta_ref, cnt_ref)


def _const_spec(arr, layer):
    if layer is None:
        return pl.BlockSpec(arr.shape, lambda b, i, _nd=arr.ndim: (0,) * _nd, pipeline_mode=pl.Buffered(1))
    return pl.BlockSpec((1,) + arr.shape[1:], lambda b, i, _nd=arr.ndim: (layer,) + (0,) * (_nd - 1),
                        pipeline_mode=pl.Buffered(1))


def _mixer_layer(h, cos_t, sin_t, consts, batch, seq, token_major_in):
    t = batch * seq
    steps = seq // TS
    row_map = lambda b, i: (b * steps + i, 0)
    tm_spec = pl.BlockSpec((TS * ROW_TILES, LANES), row_map)
    in_specs = [tm_spec if token_major_in else pl.BlockSpec((TS, D_MODEL), row_map),
                pl.BlockSpec((TS, LANES), row_map),
                pl.BlockSpec((TS, LANES), row_map)]
    in_specs += [_const_spec(arr, layer) for arr, layer in consts]
    return pl.pallas_call(
        functools.partial(_mixer_kernel, token_major_in=token_major_in),
        out_shape=(jax.ShapeDtypeStruct((t * ROW_TILES, LANES), F32),
                   jax.ShapeDtypeStruct((SUBLANES, t), jnp.int32),
                   jax.ShapeDtypeStruct((SUBLANES, LANES), jnp.int32)),
        grid=(batch, steps),
        in_specs=in_specs,
        out_specs=(tm_spec,
                   pl.BlockSpec((SUBLANES, TS), lambda b, i: (0, b * steps + i)),
                   pl.BlockSpec((SUBLANES, LANES), lambda b, i: (0, 0))),
        scratch_shapes=[
            pltpu.VMEM((CONV_PAD + TS, WIDTH), F32),
            pltpu.VMEM((CONV_PAD + TS, WIDTH), F32),
            pltpu.VMEM((1, WIDTH), F32),
            pltpu.VMEM((WIDTH, WIDTH), F32),
            pltpu.VMEM((SCAN_PAD + TS, WIDTH), F32),
            pltpu.VMEM((SCAN_PAD + TS, WIDTH), F32),
            pltpu.VMEM((TS, N_BRANCH * D_MODEL), F32),
            pltpu.VMEM((1, LANES), F32),
            pltpu.VMEM((TS, TS), BF16),
        ],
        compiler_params=pltpu.CompilerParams(
            dimension_semantics=("arbitrary", "arbitrary"),
            vmem_limit_bytes=VMEM_LIMIT_BYTES),
        name="mixer_layer",
    )(h, cos_t, sin_t, *[arr for arr, _ in consts])


def _block_diag(w):
    heads, d, e = w.shape
    eye = jnp.eye(heads, dtype=w.dtype)
    return (eye[:, None, :, None] * w[:, :, None, :]).reshape(heads * d, heads * e)


def _retention_tables():
    f32 = np.float32
    log_gamma = np.log1p(-np.exp2(f32(-5.0) - np.arange(N_HEADS, dtype=f32))).astype(f32)
    pos = np.arange(CHUNK, dtype=f32)
    diff = pos[:, None] - pos[None, :]
    decay = np.where(diff >= 0, np.exp(np.maximum(diff, f32(0)) * log_gamma[:, None, None]), f32(0)).astype(f32)
    dec_all = np.transpose(decay, (1, 0, 2)).reshape(CHUNK, N_HEADS * CHUNK)
    q_decay = np.exp((pos + f32(1))[:, None] * log_gamma).astype(f32)
    k_decay = np.exp((f32(CHUNK - 1) - pos)[:, None] * log_gamma).astype(f32)
    qd_tab = np.repeat(q_decay, HEAD_DIM, axis=1)
    kd_tab = np.repeat(k_decay, HEAD_DIM, axis=1)
    chunk_decay = np.repeat(np.exp(f32(CHUNK) * log_gamma).astype(f32), HEAD_DIM)
    cd_tab = np.ascontiguousarray(np.broadcast_to(chunk_decay[:, None], (WIDTH, WIDTH)))
    return tuple(jnp.asarray(a, F32) for a in (dec_all, qd_tab, kd_tab, cd_tab))


def _route(h, whi_ref, wlo_ref, b_ref, tri_ref, count_ref, meta_ref, cnt_ref):
    h_hi = h.astype(BF16)
    h_lo = (h - h_hi.astype(F32)).astype(BF16)
    whi = whi_ref[...]
    logits = _dot(h_hi, whi) + _dot(h_lo, whi) + _dot(h_hi, wlo_ref[...]) + b_ref[...]
    lane = lax.broadcasted_iota(jnp.int32, logits.shape, 1).astype(F32)
    big = F32(2 ** 30)
    neg_inf = F32(-jnp.inf)

    gl = jnp.where(lane < N_GROUPS, logits, neg_inf)
    gmax = jnp.max(gl, axis=-1, keepdims=True)
    g_idx = jnp.min(jnp.where(gl == gmax, lane, big), axis=-1, keepdims=True)

    e_lane = lane - N_GROUPS
    in_group = (e_lane >= g_idx * EXPERTS_PER_GROUP) & (e_lane < (g_idx + 1) * EXPERTS_PER_GROUP)
    el = jnp.where(in_group, logits, neg_inf)
    m1 = jnp.max(el, axis=-1, keepdims=True)
    i1 = jnp.min(jnp.where(el == m1, lane, big), axis=-1, keepdims=True)
    el2 = jnp.where(lane == i1, neg_inf, el)
    m2 = jnp.max(el2, axis=-1, keepdims=True)
    i2 = jnp.min(jnp.where(el2 == m2, lane, big), axis=-1, keepdims=True)
    base = g_idx * EXPERTS_PER_GROUP + N_GROUPS
    lo = jnp.minimum(i1, i2) - base
    hi = jnp.maximum(i1, i2) - base
    pair = lo * (2.0 * EXPERTS_PER_GROUP - 1.0 - lo) * 0.5 + (hi - lo - 1.0)
    cls = g_idx * PAIRS_PER_GROUP + pair

    onehot = jnp.where(lane == cls, 1.0, 0.0)
    before = _dot(tri_ref[...], onehot.astype(BF16)) + count_ref[...]
    rank = jnp.sum(jnp.where(lane == cls, before, 0.0), axis=-1, keepdims=True)
    count_ref[...] = count_ref[...] + jnp.sum(onehot, axis=0, keepdims=True)

    meta = jnp.where(lane == 0, cls, 0.0) + jnp.where(lane == 1, rank, 0.0)
    meta_ref[...] = jnp.transpose(meta)[0:SUBLANES, :].astype(jnp.int32)
    cnt_ref[...] = jnp.broadcast_to(count_ref[...], cnt_ref.shape).astype(jnp.int32)


def _token_rows(tok):
    if isinstance(tok, int):
        return pl.ds(tok * ROW_TILES, ROW_TILES)
    return pl.ds(pl.multiple_of(tok * ROW_TILES, ROW_TILES), ROW_TILES)


def _dispatch_kernel(offs_ref, cnt_ref, nused_ref, cls_ref, rank_ref, h_ref, dst0_hbm, xs_hbm, dst_hbm,
                     dst_ref, zbuf, row_sem, map_sem, zero_sem, *, n_tiles):
    step = pl.program_id(0)
    pad_bits = [1 << b for b in reversed(range(TM.bit_length() - 1))]

    def class_padding(c):
        cnt = cnt_ref[c]
        return offs_ref[c] + cnt, (TM - cnt % TM) % TM

    def zero_rows(first_row, n_rows):
        start = first_row * ROW_TILES
        if not isinstance(start, int):
            start = pl.multiple_of(start, ROW_TILES)
        return pltpu.make_async_copy(zbuf.at[pl.ds(0, n_rows * ROW_TILES), :],
                                     xs_hbm.at[pl.ds(start, n_rows * ROW_TILES), :], zero_sem)

    @pl.when(step == 0)
    def _():
        load = pltpu.make_async_copy(dst0_hbm, dst_ref, map_sem)
        load.start()
        load.wait()

        zbuf[...] = jnp.zeros_like(zbuf)

        def fill_class(c, _):
            first_row, n_pad = class_padding(c)
            for bit in pad_bits:
                @pl.when((n_pad & bit) != 0)
                def _(bit=bit):
                    zero_rows(first_row + (n_pad & ~(2 * bit - 1)), bit).start()
            return 0

        lax.fori_loop(0, N_CLASSES, fill_class, 0)

        def fill_tile(j, _):
            zero_rows(j * TM, TM).start()
            return 0

        lax.fori_loop(nused_ref[0], n_tiles, fill_tile, 0)

    first_token = step * TB_DISPATCH
    for r in range(TB_DISPATCH):
        p = offs_ref[cls_ref[r]] + rank_ref[r]
        dst_ref[p] = first_token + r
        pltpu.make_async_copy(h_ref.at[_token_rows(r), :], xs_hbm.at[_token_rows(p), :],
                              row_sem).start(priority=r % 2)

    pltpu.make_async_copy(h_ref, xs_hbm.at[pl.ds(0, TB_DISPATCH * ROW_TILES), :], row_sem).wait()

    @pl.when(step == pl.num_programs(0) - 1)
    def _():
        store = pltpu.make_async_copy(dst_ref, dst_hbm, map_sem)
        store.start()
        store.wait()

        def drain_class(c, _):
            _, n_pad = class_padding(c)

            @pl.when(n_pad > 0)
            def _():
                zero_rows(0, n_pad).wait()
            return 0

        lax.fori_loop(0, N_CLASSES, drain_class, 0)

        def drain_tile(j, _):
            zero_rows(0, TM).wait()
            return 0

        lax.fori_loop(nused_ref[0], n_tiles, drain_tile, 0)


def _dispatch(h, offs, cnt, n_used, cls, rank, n_rows):
    t = cls.shape[0]
    dst0 = jnp.zeros((n_rows,), jnp.int32)
    smem_blk = pl.BlockSpec((TB_DISPATCH,), lambda i, *_: (i,), memory_space=pltpu.SMEM)
    hbm = pl.BlockSpec(memory_space=pl.ANY)
    return pl.pallas_call(
        functools.partial(_dispatch_kernel, n_tiles=n_rows // TM),
        out_shape=(jax.ShapeDtypeStruct((n_rows * ROW_TILES, LANES), F32),
                   jax.ShapeDtypeStruct((n_rows,), jnp.int32)),
        grid_spec=pltpu.PrefetchScalarGridSpec(
            num_scalar_prefetch=3,
            grid=(t // TB_DISPATCH,),
            in_specs=[smem_blk, smem_blk,
                      pl.BlockSpec((TB_DISPATCH * ROW_TILES, LANES), lambda i, *_: (i, 0)),
                      hbm],
            out_specs=(hbm, hbm),
            scratch_shapes=[pltpu.SMEM((n_rows,), jnp.int32),
                            pltpu.VMEM((TM * ROW_TILES, LANES), F32),
                            pltpu.SemaphoreType.DMA,
                            pltpu.SemaphoreType.DMA,
                            pltpu.SemaphoreType.DMA]),
        compiler_params=pltpu.CompilerParams(dimension_semantics=("arbitrary",)),
        name="moe_dispatch",
    )(offs, cnt, n_used, cls, rank, h, dst0)


def _expert_kernel(ea_ref, eb_ref, nvalid_ref, nused_ref, *refs):
    k_tiles = TILES_PER_STEP
    dst_refs, x_refs = refs[:k_tiles], refs[k_tiles:2 * k_tiles]
    wr_ref, br_ref = refs[2 * k_tiles:2 * k_tiles + 2]
    w_refs = refs[2 * k_tiles + 2:8 * k_tiles + 2]
    lng_ref, lnb_ref, out_hbm, ybuf, ssem = refs[8 * k_tiles + 2:]
    step = pl.program_id(0)
    n_used = nused_ref[0]
    first_tile = step * k_tiles
    slot = step % 2
    other = 1 - slot

    def step_rows(s):
        base = jnp.maximum(s, 0) * k_tiles
        return sum(nvalid_ref[base + k] for k in range(k_tiles))

    def start_rows(k, n, s):
        for r in range(TM):
            token = dst_refs[k][r]

            @pl.when(r < n)
            def _(r=r, token=token):
                pltpu.make_async_copy(ybuf.at[s, _token_rows(k * TM + r), :],
                                      out_hbm.at[_token_rows(token), :],
                                      ssem.at[s]).start(priority=r % 2)

    def wait_rows(n, s):
        @pl.when(n > 0)
        def _():
            rows = pl.ds(0, n * ROW_TILES)
            pltpu.make_async_copy(ybuf.at[s, rows, :], out_hbm.at[rows, :], ssem.at[s]).wait()

    def tile_result(k):
        x_ref = x_refs[k]
        wga_ref, wua_ref, wda_ref, wgb_ref, wub_ref, wdb_ref = w_refs[6 * k:6 * k + 6]
        xb = _load_token_major(x_ref, TM).astype(BF16)

        logits = _dot(xb, wr_ref[...]) + br_ref[...]
        lane = lax.broadcasted_iota(jnp.int32, logits.shape, 1)
        ea, eb = ea_ref[first_tile + k], eb_ref[first_tile + k]
        group = ea // EXPERTS_PER_GROUP
        gl = jnp.where(lane < N_GROUPS, logits, F32(-jnp.inf))
        gmax = jnp.max(gl, axis=-1, keepdims=True)
        gsum = jnp.sum(jnp.where(lane < N_GROUPS, jnp.exp(gl - gmax), 0.0), axis=-1, keepdims=True)
        lg = jnp.sum(jnp.where(lane == group, logits, 0.0), axis=-1, keepdims=True)
        g_top_p = jnp.exp(lg - gmax) / gsum
        la = jnp.sum(jnp.where(lane == ea + N_GROUPS, logits, 0.0), axis=-1, keepdims=True)
        lb = jnp.sum(jnp.where(lane == eb + N_GROUPS, logits, 0.0), axis=-1, keepdims=True)
        w_a = g_top_p / (1.0 + jnp.exp(lb - la))
        w_b = g_top_p / (1.0 + jnp.exp(la - lb))

        def hidden(wg_ref, wu_ref, weight):
            gate, up = _dot(xb, wg_ref[0, 0]), _dot(xb, wu_ref[0, 0])
            return ((gate * _sigmoid(gate)) * up * weight).astype(BF16)

        ffn = (_dot(hidden(wga_ref, wua_ref, w_a), wda_ref[0, 0])
               + _dot(hidden(wgb_ref, wub_ref, w_b), wdb_ref[0, 0]))
        return _layer_norm_rows(ALPHA * _load_token_major(x_ref, TM) + ffn, lng_ref[...], lnb_ref[...])

    @pl.when(first_tile < n_used)
    def _():
        @pl.when(step >= 2)
        def _():
            wait_rows(step_rows(step - 2), slot)

        for k in range(k_tiles):
            result = tile_result(k)
            for c in range(ROW_TILES):
                ybuf[slot, pl.ds(k * TM * ROW_TILES + c, TM, stride=ROW_TILES), :] = (
                    result[:, c * LANES:(c + 1) * LANES])
            start_rows(k, nvalid_ref[first_tile + k], slot)

        @pl.when(first_tile + k_tiles >= n_used)
        def _():
            @pl.when(step >= 1)
            def _():
                wait_rows(step_rows(step - 1), other)

            wait_rows(step_rows(step), slot)


def _experts(xs, t, dst, tile_ea, tile_eb, tile_nvalid, n_used, w_router, b_router, wg, wu, wd, layer,
             ln_g, ln_b):
    k_tiles = TILES_PER_STEP
    n_tiles = dst.shape[0] // TM
    assert n_tiles % k_tiles == 0
    const = lambda shape: pl.BlockSpec(shape, lambda i, ea, eb, nv, nu: (0,) * len(shape))
    dst_spec = lambda k: pl.BlockSpec((TM,), lambda i, ea, eb, nv, nu: (i * k_tiles + k,),
                                      memory_space=pltpu.SMEM)
    x_spec = lambda k: pl.BlockSpec((TM * ROW_TILES, LANES), lambda i, ea, eb, nv, nu: (i * k_tiles + k, 0))
    wgu_spec = lambda k, which: pl.BlockSpec(
        (1, 1, D_MODEL, D_EXPERT),
        lambda i, ea, eb, nv, nu: (layer, (ea, eb)[which][i * k_tiles + k], 0, 0))
    wd_spec = lambda k, which: pl.BlockSpec(
        (1, 1, D_EXPERT, D_MODEL),
        lambda i, ea, eb, nv, nu: (layer, (ea, eb)[which][i * k_tiles + k], 0, 0))
    tiles = range(k_tiles)
    weight_specs, weight_args = [], []
    for k in tiles:
        weight_specs += [wgu_spec(k, 0), wgu_spec(k, 0), wd_spec(k, 0),
                         wgu_spec(k, 1), wgu_spec(k, 1), wd_spec(k, 1)]
        weight_args += [wg, wu, wd, wg, wu, wd]
    grid_spec = pltpu.PrefetchScalarGridSpec(
        num_scalar_prefetch=4,
        grid=(n_tiles // k_tiles,),
        in_specs=([dst_spec(k) for k in tiles] + [x_spec(k) for k in tiles]
                  + [const((D_MODEL, LANES)), const((1, LANES))] + weight_specs
                  + [const((1, D_MODEL)), const((1, D_MODEL))]),
        out_specs=pl.BlockSpec(memory_space=pl.ANY),
        scratch_shapes=[pltpu.VMEM((2, k_tiles * TM * ROW_TILES, LANES), F32),
                        pltpu.SemaphoreType.DMA((2,))],
    )
    return pl.pallas_call(
        _expert_kernel,
        out_shape=jax.ShapeDtypeStruct((t * ROW_TILES, LANES), F32),
        grid_spec=grid_spec,
        compiler_params=pltpu.CompilerParams(dimension_semantics=("arbitrary",),
                                             vmem_limit_bytes=VMEM_LIMIT_BYTES),
        name="moe_experts",
    )(tile_ea, tile_eb, tile_nvalid, n_used, *([dst] * k_tiles), *([xs] * k_tiles),
      w_router, b_router, *weight_args, ln_g, ln_b)


def _relayout_kernel(h_ref, out_ref):
    out_ref[...] = _load_token_major(h_ref, TS)


def _rows_from_token_major(h, t):
    return pl.pallas_call(
        _relayout_kernel,
        out_shape=jax.ShapeDtypeStruct((t, D_MODEL), F32),
        grid=(t // TS,),
        in_specs=[pl.BlockSpec((TS * ROW_TILES, LANES), lambda i: (i, 0))],
        out_specs=pl.BlockSpec((TS, D_MODEL), lambda i: (i, 0)),
        name="rows_from_token_major",
    )(h)


def _class_expert_tables():
    first, second = [], []
    for g in range(N_GROUPS):
        for lo in range(EXPERTS_PER_GROUP):
            for hi in range(lo + 1, EXPERTS_PER_GROUP):
                first.append(g * EXPERTS_PER_GROUP + lo)
                second.append(g * EXPERTS_PER_GROUP + hi)
    return jnp.asarray(np.array(first, np.int32)), jnp.asarray(np.array(second, np.int32))


def _router_weights(wg, bg, we, be):
    w_router = jnp.zeros((D_MODEL, LANES), F32)
    w_router = w_router.at[:, :N_GROUPS].set(wg).at[:, N_GROUPS:N_GROUPS + N_EXPERTS].set(we)
    b_router = jnp.zeros((1, LANES), F32)
    b_router = b_router.at[0, :N_GROUPS].set(bg).at[0, N_GROUPS:N_GROUPS + N_EXPERTS].set(be)
    w_hi = w_router.astype(BF16)
    w_lo = (w_router - w_hi.astype(F32)).astype(BF16)
    return w_hi, w_lo, b_router


def _moe_layer(h, t, meta, counts, w_hi, b_router, wg, wu, wd, layer, ln_g, ln_b):
    n_tiles = t // TM + N_CLASSES
    n_rows = n_tiles * TM

    cnt = counts[0]
    tiles_per = (cnt + TM - 1) // TM
    tile_end = jnp.cumsum(tiles_per)
    offs = (tile_end - tiles_per) * TM
    n_used = tile_end[-1]
    tile_id = jnp.arange(n_tiles, dtype=jnp.int32)
    tile_cls = jnp.sum((jnp.minimum(tile_id, n_used - 1)[:, None] >= tile_end[None, :N_CLASSES])
                       .astype(jnp.int32), axis=1)
    seg_tile = tile_id - (tile_end - tiles_per)[tile_cls]
    tile_nvalid = jnp.where(tile_id < n_used, jnp.clip(cnt[tile_cls] - seg_tile * TM, 0, TM), 0)
    first, second = _class_expert_tables()
    xs, dst = _dispatch(h, offs, cnt, n_used.reshape(1), meta[0], meta[1], n_rows)
    return _experts(xs, t, dst, first[tile_cls], second[tile_cls], tile_nvalid, n_used.reshape(1),
                    w_hi, b_router, wg, wu, wd, layer, ln_g, ln_b)


def kernel(x, positions, w_in, sc_conv_w, sc_conv_b, lru_conv_w, lru_conv_b, lru_w_r, lru_b_r,
           lru_w_i, lru_b_i, lru_lambda, sg_norm_g, sg_w_s, sg_b_s, ret_norm_g, branch_proj, w_out,
           ln_mix_g, ln_mix_b, router_group_w, router_group_b, router_expert_w, router_expert_b,
           exp_w_gate, exp_w_up, exp_w_down, ln_ffn_g, ln_ffn_b):
    batch, seq, d = x.shape
    assert d == D_MODEL and seq % TS == 0 and w_in.shape[-1] == N_IN
    depth = w_in.shape[0]
    t = batch * seq
    cos_t, sin_t = _rope_tables(positions)
    dec_all, qd_tab, kd_tab, cd_tab = _retention_tables()
    avg = jnp.asarray(np.kron(np.eye(N_HEADS, dtype=np.float32),
                              np.full((HEAD_DIM, HEAD_DIM), 1.0 / HEAD_DIM, np.float32)), BF16)

    w_in_b, bp_b, wo_b = w_in.astype(BF16), branch_proj.astype(BF16), w_out.astype(BF16)
    wg_b, wu_b, wd_b = exp_w_gate.astype(BF16), exp_w_up.astype(BF16), exp_w_down.astype(BF16)

    h = x.reshape(t, d)
    for l in range(depth):
        w_ri = jnp.concatenate([_block_diag(lru_w_r[l]), _block_diag(lru_w_i[l])], axis=1).astype(BF16)
        b_ri = jnp.concatenate([lru_b_r[l], lru_b_i[l]])[None, :]
        bs_tab = jnp.repeat(sg_b_s[l].T, HEAD_DIM, axis=1)
        per_call = lambda *arrs: [(a, None) for a in arrs]
        w_hi, w_lo, b_router = _router_weights(router_group_w[l], router_group_b[l],
                                               router_expert_w[l], router_expert_b[l])
        consts = (
            [(w_in_b, l)]
            + per_call(sc_conv_w[l], sc_conv_b[l][None, :], lru_conv_w[l], lru_conv_b[l][None, :],
                       w_ri, b_ri, lru_lambda[l][None, :], sg_norm_g[l][None, :], sg_w_s[l], bs_tab,
                       ret_norm_g[l][None, :], dec_all, qd_tab, kd_tab, cd_tab, avg)
            + [(bp_b, l), (wo_b, l)]
            + per_call(ln_mix_g[l][None, :], ln_mix_b[l][None, :], w_hi, w_lo, b_router))
        h, meta, counts = _mixer_layer(h, cos_t, sin_t, consts, batch, seq, token_major_in=l > 0)
        h = _moe_layer(h, t, meta, counts, w_hi, b_router, wg_b, wu_b, wd_b, l,
                       ln_ffn_g[l][None, :], ln_ffn_b[l][None, :])
    return _rows_from_token_major(h, t).reshape(batch, seq, d)
```

```python
import functools

import jax
import jax.numpy as jnp
import numpy as np
from jax import lax
from jax.experimental import pallas as pl
from jax.experimental.pallas import tpu as pltpu

F32 = jnp.float32
BF16 = jnp.bfloat16

LANES = 128
SUBLANES = 8
VMEM_LIMIT_BYTES = 56 * 1024 * 1024

D_MODEL = 1024
ROW_TILES = D_MODEL // LANES
assert ROW_TILES == SUBLANES
WIDTH = D_MODEL // 4
N_HEADS = 4
N_BRANCH = 4
HEAD_DIM = WIDTH // N_HEADS
CHUNK = 128
SC_K = 3
LRU_K = 4
LRU_C = 8.0
ROPE_BASE = 10000.0
N_GROUPS = 4
EXPERTS_PER_GROUP = 8
N_EXPERTS = N_GROUPS * EXPERTS_PER_GROUP
D_EXPERT = D_MODEL // 4
LN_EPS = 1e-5
DEPTH = 2
ALPHA = (2.0 * DEPTH) ** 0.25

COL_SC_B, COL_SC_C, COL_SC_X, COL_LRU, COL_SG_U, COL_SG_V, COL_Q, COL_K, COL_V, COL_G = (
    i * WIDTH for i in range(10))
COL_GATES = 10 * WIDTH
N_IN = COL_GATES + 4 * D_MODEL

TS = 512
GATE_CHUNK = 256
SCAN_PAD = TS // 2
CONV_PAD = SUBLANES
TB_DISPATCH = 512
TM = 128
TILES_PER_STEP = 4
PAIRS_PER_GROUP = EXPERTS_PER_GROUP * (EXPERTS_PER_GROUP - 1) // 2
N_CLASSES = N_GROUPS * PAIRS_PER_GROUP
assert N_CLASSES <= LANES


def _sigmoid(x):
    return 0.5 * jnp.tanh(0.5 * x) + 0.5


def _dot(a, b):
    return jnp.dot(a, b, preferred_element_type=F32)


def _load_token_major(ref, rows, lead=()):
    return jnp.concatenate(
        [ref[lead + (pl.ds(c, rows, stride=ROW_TILES), slice(None))] for c in range(ROW_TILES)], axis=1)


def _store_token_major(ref, y, rows, lead=()):
    for c in range(ROW_TILES):
        ref[lead + (pl.ds(c, rows, stride=ROW_TILES), slice(None))] = y[:, c * LANES:(c + 1) * LANES]


def _layer_norm_rows(y, g, b):
    mu = jnp.mean(y, axis=-1, keepdims=True)
    yc = y - mu
    var = jnp.mean(yc * yc, axis=-1, keepdims=True)
    return yc * lax.rsqrt(var + LN_EPS) * g + b


def _rope_table_kernel(pos_ref, freq_ref, cos_ref, sin_ref):
    ang = pos_ref[...].astype(F32) * freq_ref[...]
    lane = lax.broadcasted_iota(jnp.int32, ang.shape, 1)
    first_half = (lane % HEAD_DIM) < (HEAD_DIM // 2)
    cos_ref[...] = jnp.cos(ang)
    s = jnp.sin(ang)
    sin_ref[...] = jnp.where(first_half, -s, s)


def _rope_tables(positions):
    t = positions.size
    half = HEAD_DIM // 2
    inv_freq = ROPE_BASE ** (-jnp.arange(half, dtype=F32) / half)
    freq_row = jnp.tile(inv_freq, LANES // half)[None, :]
    rows = 1024
    return pl.pallas_call(
        _rope_table_kernel,
        out_shape=(jax.ShapeDtypeStruct((t, LANES), F32), jax.ShapeDtypeStruct((t, LANES), F32)),
        grid=(t // rows,),
        in_specs=[pl.BlockSpec((rows, 1), lambda i: (i, 0)),
                  pl.BlockSpec((1, LANES), lambda i: (0, 0))],
        out_specs=(pl.BlockSpec((rows, LANES), lambda i: (i, 0)),
                   pl.BlockSpec((rows, LANES), lambda i: (i, 0))),
        name="rope_tables",
    )(positions.reshape(t, 1), freq_row)


def _mixer_kernel(h_ref, cos_ref, sin_ref, w_in_ref, scw_ref, scb_ref, lcw_ref, lcb_ref,
                  wri_ref, bri_ref, lam_ref, sgg_ref, ws_ref, bs_ref, rg_ref,
                  dec_ref, qd_ref, kd_ref, cd_ref, avg_ref, bp_ref, wo_ref, lng_ref, lnb_ref,
                  rwhi_ref, rwlo_ref, rb_ref,
                  out_ref, meta_ref, cnt_ref,
                  ubuf, xbuf, hcar, sbd, abuf, hbuf, gbuf, rcount, tri, *, token_major_in):
    step = pl.program_id(1)

    @pl.when((pl.program_id(0) == 0) & (step == 0))
    def _():
        rcount[...] = jnp.zeros_like(rcount)
        r_i = lax.broadcasted_iota(jnp.int32, (TS, TS), 0)
        c_i = lax.broadcasted_iota(jnp.int32, (TS, TS), 1)
        tri[...] = jnp.where(r_i > c_i, 1.0, 0.0).astype(BF16)

    @pl.when(step == 0)
    def _():
        ubuf[0:CONV_PAD, :] = jnp.zeros((CONV_PAD, WIDTH), F32)
        xbuf[0:CONV_PAD, :] = jnp.zeros((CONV_PAD, WIDTH), F32)
        hcar[...] = jnp.zeros_like(hcar)
        sbd[...] = jnp.zeros_like(sbd)
        abuf[0:SCAN_PAD, :] = jnp.ones((SCAN_PAD, WIDTH), F32)
        hbuf[0:SCAN_PAD, :] = jnp.zeros((SCAN_PAD, WIDTH), F32)

    h = _load_token_major(h_ref, TS) if token_major_in else h_ref[...]
    hb = h.astype(BF16)

    def proj(col, width=WIDTH):
        return _dot(hb, w_in_ref[0, :, col:col + width])

    pending_gate_cols = list(range(0, N_BRANCH * D_MODEL, GATE_CHUNK))

    def emit_gates(n_chunks=1):
        for _ in range(n_chunks):
            if pending_gate_cols:
                col = pending_gate_cols.pop(0)
                gbuf[:, col:col + GATE_CHUNK] = _sigmoid(proj(COL_GATES + col, GATE_CHUNK))

    lane = lax.broadcasted_iota(jnp.int32, (CHUNK, WIDTH), 1)
    head_of_lane = lane // HEAD_DIM
    avg = avg_ref[...]

    def group_standardize(x):
        mean = _dot(x.astype(BF16), avg)
        xc = x - mean
        var = _dot((xc * xc).astype(BF16), avg)
        return xc * lax.rsqrt(var + LN_EPS)

    sc_b, sc_c, sc_x = proj(COL_SC_B), proj(COL_SC_C), proj(COL_SC_X)
    u = sc_c * sc_x
    ubuf[CONV_PAD:CONV_PAD + TS, :] = u
    conv = scw_ref[2:3, :] * u + scb_ref[...]
    for j in range(SC_K - 1):
        back = SC_K - 1 - j
        conv = conv + scw_ref[j:j + 1, :] * ubuf[CONV_PAD - back:CONV_PAD - back + TS, :]
    ubuf[0:CONV_PAD, :] = ubuf[TS:TS + CONV_PAD, :]
    branch_a = sc_b * conv
    emit_gates()

    lx = proj(COL_LRU)
    xbuf[CONV_PAD:CONV_PAD + TS, :] = lx
    xc = lcw_ref[LRU_K - 1:LRU_K, :] * lx + lcb_ref[...]
    for j in range(LRU_K - 1):
        back = LRU_K - 1 - j
        xc = xc + lcw_ref[j:j + 1, :] * xbuf[CONV_PAD - back:CONV_PAD - back + TS, :]
    xbuf[0:CONV_PAD, :] = xbuf[TS:TS + CONV_PAD, :]
    emit_gates()
    ri = _dot(xc.astype(BF16), wri_ref[...]) + bri_ref[...]
    r = _sigmoid(ri[:, :WIDTH])
    ig = _sigmoid(ri[:, WIDTH:])
    neg_lam = -lam_ref[...]
    softplus = jnp.maximum(neg_lam, 0.0) + jnp.log1p(jnp.exp(-jnp.abs(neg_lam)))
    log_a = (-LRU_C) * r * softplus
    a = jnp.exp(log_a)
    th = jnp.tanh(log_a)
    uu = jnp.sqrt((-2.0) * th / (1.0 - th)) * (ig * xc)
    row = lax.broadcasted_iota(jnp.int32, (TS, WIDTH), 0)
    uu = uu + jnp.where(row == 0, a * hcar[...], 0.0)
    abuf[SCAN_PAD:SCAN_PAD + TS, :] = a
    hbuf[SCAN_PAD:SCAN_PAD + TS, :] = uu
    shift = 1
    while shift < TS:
        a_cur = abuf[SCAN_PAD:SCAN_PAD + TS, :]
        h_prev = hbuf[SCAN_PAD - shift:SCAN_PAD - shift + TS, :]
        h_new = a_cur * h_prev + hbuf[SCAN_PAD:SCAN_PAD + TS, :]
        if shift * 2 < TS:
            a_new = a_cur * abuf[SCAN_PAD - shift:SCAN_PAD - shift + TS, :]
            abuf[SCAN_PAD:SCAN_PAD + TS, :] = a_new
        hbuf[SCAN_PAD:SCAN_PAD + TS, :] = h_new
        emit_gates()
        shift *= 2
    branch_b = hbuf[SCAN_PAD:SCAN_PAD + TS, :]
    hcar[...] = hbuf[SCAN_PAD + TS - 1:SCAN_PAD + TS, :]

    gu = jax.nn.gelu(proj(COL_SG_U))
    gv = jax.nn.gelu(proj(COL_SG_V))
    emit_gates()
    vn = (group_standardize(gv) * sgg_ref[...]).astype(BF16)
    trow = lax.broadcasted_iota(jnp.int32, (CHUNK, CHUNK), 0)
    tcol = lax.broadcasted_iota(jnp.int32, (CHUNK, CHUNK), 1)
    w_causal = [jnp.where(trow >= tcol, ws_ref[g], 0.0).astype(BF16) for g in range(N_HEADS)]
    sv_chunks = []
    for c in range(TS // CHUNK):
        vch = vn[c * CHUNK:(c + 1) * CHUNK, :]
        sv = bs_ref[...]
        for g in range(N_HEADS):
            sv = sv + jnp.where(head_of_lane == g, _dot(w_causal[g], vch), 0.0)
        sv_chunks.append(sv)
    branch_c = gu * jnp.concatenate(sv_chunks, axis=0)
    emit_gates()

    cos = jnp.concatenate([cos_ref[...]] * (WIDTH // LANES), axis=1)
    sin = jnp.concatenate([sin_ref[...]] * (WIDTH // LANES), axis=1)
    lane_ts = lax.broadcasted_iota(jnp.int32, (TS, WIDTH), 1)
    first_half = (lane_ts % HEAD_DIM) < (HEAD_DIM // 2)

    def rope(t):
        swapped = jnp.where(first_half,
                            pltpu.roll(t, WIDTH - HEAD_DIM // 2, 1),
                            pltpu.roll(t, HEAD_DIM // 2, 1))
        return t * cos + swapped * sin

    q = rope(proj(COL_Q))
    k = rope(proj(COL_K)) * (HEAD_DIM ** -0.5)
    emit_gates()
    v = proj(COL_V)
    zg = proj(COL_G)
    bd_mask = (lax.broadcasted_iota(jnp.int32, (WIDTH, WIDTH), 0) // HEAD_DIM
               == lax.broadcasted_iota(jnp.int32, (WIDTH, WIDTH), 1) // HEAD_DIM)
    o_chunks = []
    for c in range(TS // CHUNK):
        sl = slice(c * CHUNK, (c + 1) * CHUNK)
        qc, kc, vc = q[sl, :], k[sl, :], v[sl, :]
        qcb = qc.astype(BF16)
        kstack = jnp.concatenate(
            [jnp.where(head_of_lane == hh, kc, 0.0) for hh in range(N_HEADS)], axis=0).astype(BF16)
        vstack = jnp.concatenate(
            [jnp.where(head_of_lane == hh, vc, 0.0) for hh in range(N_HEADS)], axis=0).astype(BF16)
        scores = lax.dot_general(qcb, kstack, (((1,), (1,)), ((), ())),
                                 preferred_element_type=F32)
        inner = _dot((scores * dec_ref[...]).astype(BF16), vstack)
        state = sbd[...]
        cross = _dot(qcb, state.astype(BF16)) * qd_ref[...]
        o_chunks.append(inner + cross)
        kdec = (kc * kd_ref[...]).astype(BF16)
        kv = lax.dot_general(kdec, vc.astype(BF16), (((0,), (0,)), ((), ())),
                             preferred_element_type=F32)
        sbd[...] = cd_ref[...] * state + jnp.where(bd_mask, kv, 0.0)
        emit_gates()
    o = jnp.concatenate(o_chunks, axis=0)
    branch_d = (zg * _sigmoid(zg)) * (group_standardize(o) * rg_ref[...])

    emit_gates(len(pending_gate_cols))
    branches = [b.astype(BF16) for b in (branch_a, branch_b, branch_c, branch_d)]
    half = D_MODEL // 2
    mix = jnp.zeros((TS, D_MODEL), F32)
    for lo in range(0, D_MODEL, half):
        merged = jnp.zeros((TS, half), F32)
        for b_idx, branch in enumerate(branches):
            gate = gbuf[:, b_idx * D_MODEL + lo:b_idx * D_MODEL + lo + half]
            merged = merged + gate * _dot(branch, bp_ref[0, b_idx, :, lo:lo + half])
        mix = mix + _dot(merged.astype(BF16), wo_ref[0, lo:lo + half, :])
    h_mid = _layer_norm_rows(ALPHA * h + mix, lng_ref[...], lnb_ref[...])
    _store_token_major(out_ref, h_mid, TS)
    _route(h_mid, rwhi_ref, rwlo_ref, rb_ref, tri, rcount, meta_ref, cnt_ref)


def _const_spec(arr, layer):
    if layer is None:
        return pl.BlockSpec(arr.shape, lambda b, i, _nd=arr.ndim: (0,) * _nd, pipeline_mode=pl.Buffered(1))
    return pl.BlockSpec((1,) + arr.shape[1:], lambda b, i, _nd=arr.ndim: (layer,) + (0,) * (_nd - 1),
                        pipeline_mode=pl.Buffered(1))


def _mixer_layer(h, cos_t, sin_t, consts, batch, seq, token_major_in):
    t = batch * seq
    steps = seq // TS
    row_map = lambda b, i: (b * steps + i, 0)
    tm_spec = pl.BlockSpec((TS * ROW_TILES, LANES), row_map)
    in_specs = [tm_spec if token_major_in else pl.BlockSpec((TS, D_MODEL), row_map),
                pl.BlockSpec((TS, LANES), row_map),
                pl.BlockSpec((TS, LANES), row_map)]
    in_specs += [_const_spec(arr, layer) for arr, layer in consts]
    return pl.pallas_call(
        functools.partial(_mixer_kernel, token_major_in=token_major_in),
        out_shape=(jax.ShapeDtypeStruct((t * ROW_TILES, LANES), F32),
                   jax.ShapeDtypeStruct((SUBLANES, t), jnp.int32),
                   jax.ShapeDtypeStruct((SUBLANES, LANES), jnp.int32)),
        grid=(batch, steps),
        in_specs=in_specs,
        out_specs=(tm_spec,
                   pl.BlockSpec((SUBLANES, TS), lambda b, i: (0, b * steps + i)),
                   pl.BlockSpec((SUBLANES, LANES), lambda b, i: (0, 0))),
        scratch_shapes=[
            pltpu.VMEM((CONV_PAD + TS, WIDTH), F32),
            pltpu.VMEM((CONV_PAD + TS, WIDTH), F32),
            pltpu.VMEM((1, WIDTH), F32),
            pltpu.VMEM((WIDTH, WIDTH), F32),
            pltpu.VMEM((SCAN_PAD + TS, WIDTH), F32),
            pltpu.VMEM((SCAN_PAD + TS, WIDTH), F32),
            pltpu.VMEM((TS, N_BRANCH * D_MODEL), F32),
            pltpu.VMEM((1, LANES), F32),
            pltpu.VMEM((TS, TS), BF16),
        ],
        compiler_params=pltpu.CompilerParams(
            dimension_semantics=("arbitrary", "arbitrary"),
            vmem_limit_bytes=VMEM_LIMIT_BYTES),
        name="mixer_layer",
    )(h, cos_t, sin_t, *[arr for arr, _ in consts])


def _block_diag(w):
    heads, d, e = w.shape
    eye = jnp.eye(heads, dtype=w.dtype)
    return (eye[:, None, :, None] * w[:, :, None, :]).reshape(heads * d, heads * e)


def _retention_tables():
    f32 = np.float32
    log_gamma = np.log1p(-np.exp2(f32(-5.0) - np.arange(N_HEADS, dtype=f32))).astype(f32)
    pos = np.arange(CHUNK, dtype=f32)
    diff = pos[:, None] - pos[None, :]
    decay = np.where(diff >= 0, np.exp(np.maximum(diff, f32(0)) * log_gamma[:, None, None]), f32(0)).astype(f32)
    dec_all = np.transpose(decay, (1, 0, 2)).reshape(CHUNK, N_HEADS * CHUNK)
    q_decay = np.exp((pos + f32(1))[:, None] * log_gamma).astype(f32)
    k_decay = np.exp((f32(CHUNK - 1) - pos)[:, None] * log_gamma).astype(f32)
    qd_tab = np.repeat(q_decay, HEAD_DIM, axis=1)
    kd_tab = np.repeat(k_decay, HEAD_DIM, axis=1)
    chunk_decay = np.repeat(np.exp(f32(CHUNK) * log_gamma).astype(f32), HEAD_DIM)
    cd_tab = np.ascontiguousarray(np.broadcast_to(chunk_decay[:, None], (WIDTH, WIDTH)))
    return tuple(jnp.asarray(a, F32) for a in (dec_all, qd_tab, kd_tab, cd_tab))


def _route(h, whi_ref, wlo_ref, b_ref, tri_ref, count_ref, meta_ref, cnt_ref):
    h_hi = h.astype(BF16)
    h_lo = (h - h_hi.astype(F32)).astype(BF16)
    whi = whi_ref[...]
    logits = _dot(h_hi, whi) + _dot(h_lo, whi) + _dot(h_hi, wlo_ref[...]) + b_ref[...]
    lane = lax.broadcasted_iota(jnp.int32, logits.shape, 1).astype(F32)
    big = F32(2 ** 30)
    neg_inf = F32(-jnp.inf)

    gl = jnp.where(lane < N_GROUPS, logits, neg_inf)
    gmax = jnp.max(gl, axis=-1, keepdims=True)
    g_idx = jnp.min(jnp.where(gl == gmax, lane, big), axis=-1, keepdims=True)

    e_lane = lane - N_GROUPS
    in_group = (e_lane >= g_idx * EXPERTS_PER_GROUP) & (e_lane < (g_idx + 1) * EXPERTS_PER_GROUP)
    el = jnp.where(in_group, logits, neg_inf)
    m1 = jnp.max(el, axis=-1, keepdims=True)
    i1 = jnp.min(jnp.where(el == m1, lane, big), axis=-1, keepdims=True)
    el2 = jnp.where(lane == i1, neg_inf, el)
    m2 = jnp.max(el2, axis=-1, keepdims=True)
    i2 = jnp.min(jnp.where(el2 == m2, lane, big), axis=-1, keepdims=True)
    base = g_idx * EXPERTS_PER_GROUP + N_GROUPS
    lo = jnp.minimum(i1, i2) - base
    hi = jnp.maximum(i1, i2) - base
    pair = lo * (2.0 * EXPERTS_PER_GROUP - 1.0 - lo) * 0.5 + (hi - lo - 1.0)
    cls = g_idx * PAIRS_PER_GROUP + pair

    onehot = jnp.where(lane == cls, 1.0, 0.0)
    before = _dot(tri_ref[...], onehot.astype(BF16)) + count_ref[...]
    rank = jnp.sum(jnp.where(lane == cls, before, 0.0), axis=-1, keepdims=True)
    count_ref[...] = count_ref[...] + jnp.sum(onehot, axis=0, keepdims=True)

    meta = jnp.where(lane == 0, cls, 0.0) + jnp.where(lane == 1, rank, 0.0)
    meta_ref[...] = jnp.transpose(meta)[0:SUBLANES, :].astype(jnp.int32)
    cnt_ref[...] = jnp.broadcast_to(count_ref[...], cnt_ref.shape).astype(jnp.int32)


def _token_rows(tok):
    if isinstance(tok, int):
        return pl.ds(tok * ROW_TILES, ROW_TILES)
    return pl.ds(pl.multiple_of(tok * ROW_TILES, ROW_TILES), ROW_TILES)


def _dispatch_kernel(offs_ref, cnt_ref, nused_ref, cls_ref, rank_ref, h_ref, dst0_hbm, xs_hbm, dst_hbm,
                     dst_ref, zbuf, row_sem, map_sem, zero_sem, *, n_tiles):
    step = pl.program_id(0)
    pad_bits = [1 << b for b in reversed(range(TM.bit_length() - 1))]

    def class_padding(c):
        cnt = cnt_ref[c]
        return offs_ref[c] + cnt, (TM - cnt % TM) % TM

    def zero_rows(first_row, n_rows):
        start = first_row * ROW_TILES
        if not isinstance(start, int):
            start = pl.multiple_of(start, ROW_TILES)
        return pltpu.make_async_copy(zbuf.at[pl.ds(0, n_rows * ROW_TILES), :],
                                     xs_hbm.at[pl.ds(start, n_rows * ROW_TILES), :], zero_sem)

    @pl.when(step == 0)
    def _():
        load = pltpu.make_async_copy(dst0_hbm, dst_ref, map_sem)
        load.start()
        load.wait()

        zbuf[...] = jnp.zeros_like(zbuf)

        def fill_class(c, _):
            first_row, n_pad = class_padding(c)
            for bit in pad_bits:
                @pl.when((n_pad & bit) != 0)
                def _(bit=bit):
                    zero_rows(first_row + (n_pad & ~(2 * bit - 1)), bit).start()
            return 0

        lax.fori_loop(0, N_CLASSES, fill_class, 0)

        def fill_tile(j, _):
            zero_rows(j * TM, TM).start()
            return 0

        lax.fori_loop(nused_ref[0], n_tiles, fill_tile, 0)

    first_token = step * TB_DISPATCH
    for r in range(TB_DISPATCH):
        p = offs_ref[cls_ref[r]] + rank_ref[r]
        dst_ref[p] = first_token + r
        pltpu.make_async_copy(h_ref.at[_token_rows(r), :], xs_hbm.at[_token_rows(p), :],
                              row_sem).start(priority=r % 2)

    pltpu.make_async_copy(h_ref, xs_hbm.at[pl.ds(0, TB_DISPATCH * ROW_TILES), :], row_sem).wait()

    @pl.when(step == pl.num_programs(0) - 1)
    def _():
        store = pltpu.make_async_copy(dst_ref, dst_hbm, map_sem)
        store.start()
        store.wait()

        def drain_class(c, _):
            _, n_pad = class_padding(c)

            @pl.when(n_pad > 0)
            def _():
                zero_rows(0, n_pad).wait()
            return 0

        lax.fori_loop(0, N_CLASSES, drain_class, 0)

        def drain_tile(j, _):
            zero_rows(0, TM).wait()
            return 0

        lax.fori_loop(nused_ref[0], n_tiles, drain_tile, 0)


def _dispatch(h, offs, cnt, n_used, cls, rank, n_rows):
    t = cls.shape[0]
    dst0 = jnp.zeros((n_rows,), jnp.int32)
    smem_blk = pl.BlockSpec((TB_DISPATCH,), lambda i, *_: (i,), memory_space=pltpu.SMEM)
    hbm = pl.BlockSpec(memory_space=pl.ANY)
    return pl.pallas_call(
        functools.partial(_dispatch_kernel, n_tiles=n_rows // TM),
        out_shape=(jax.ShapeDtypeStruct((n_rows * ROW_TILES, LANES), F32),
                   jax.ShapeDtypeStruct((n_rows,), jnp.int32)),
        grid_spec=pltpu.PrefetchScalarGridSpec(
            num_scalar_prefetch=3,
            grid=(t // TB_DISPATCH,),
            in_specs=[smem_blk, smem_blk,
                      pl.BlockSpec((TB_DISPATCH * ROW_TILES, LANES), lambda i, *_: (i, 0)),
                      hbm],
            out_specs=(hbm, hbm),
            scratch_shapes=[pltpu.SMEM((n_rows,), jnp.int32),
                            pltpu.VMEM((TM * ROW_TILES, LANES), F32),
                            pltpu.SemaphoreType.DMA,
                            pltpu.SemaphoreType.DMA,
                            pltpu.SemaphoreType.DMA]),
        compiler_params=pltpu.CompilerParams(dimension_semantics=("arbitrary",)),
        name="moe_dispatch",
    )(offs, cnt, n_used, cls, rank, h, dst0)


def _expert_kernel(ea_ref, eb_ref, nvalid_ref, nused_ref, *refs):
    k_tiles = TILES_PER_STEP
    dst_refs, x_refs = refs[:k_tiles], refs[k_tiles:2 * k_tiles]
    wr_ref, br_ref = refs[2 * k_tiles:2 * k_tiles + 2]
    w_refs = refs[2 * k_tiles + 2:8 * k_tiles + 2]
    lng_ref, lnb_ref, out_hbm, ybuf, ssem = refs[8 * k_tiles + 2:]
    step = pl.program_id(0)
    n_used = nused_ref[0]
    first_tile = step * k_tiles
    slot = step % 2
    other = 1 - slot

    def step_rows(s):
        base = jnp.maximum(s, 0) * k_tiles
        return sum(nvalid_ref[base + k] for k in range(k_tiles))

    def start_rows(k, n, s):
        for r in range(TM):
            token = dst_refs[k][r]

            @pl.when(r < n)
            def _(r=r, token=token):
                pltpu.make_async_copy(ybuf.at[s, _token_rows(k * TM + r), :],
                                      out_hbm.at[_token_rows(token), :],
                                      ssem.at[s]).start(priority=r % 2)

    def wait_rows(n, s):
        @pl.when(n > 0)
        def _():
            rows = pl.ds(0, n * ROW_TILES)
            pltpu.make_async_copy(ybuf.at[s, rows, :], out_hbm.at[rows, :], ssem.at[s]).wait()

    def tile_result(k):
        x_ref = x_refs[k]
        wga_ref, wua_ref, wda_ref, wgb_ref, wub_ref, wdb_ref = w_refs[6 * k:6 * k + 6]
        xb = _load_token_major(x_ref, TM).astype(BF16)

        logits = _dot(xb, wr_ref[...]) + br_ref[...]
        lane = lax.broadcasted_iota(jnp.int32, logits.shape, 1)
        ea, eb = ea_ref[first_tile + k], eb_ref[first_tile + k]
        group = ea // EXPERTS_PER_GROUP
        gl = jnp.where(lane < N_GROUPS, logits, F32(-jnp.inf))
        gmax = jnp.max(gl, axis=-1, keepdims=True)
        gsum = jnp.sum(jnp.where(lane < N_GROUPS, jnp.exp(gl - gmax), 0.0), axis=-1, keepdims=True)
        lg = jnp.sum(jnp.where(lane == group, logits, 0.0), axis=-1, keepdims=True)
        g_top_p = jnp.exp(lg - gmax) / gsum
        la = jnp.sum(jnp.where(lane == ea + N_GROUPS, logits, 0.0), axis=-1, keepdims=True)
        lb = jnp.sum(jnp.where(lane == eb + N_GROUPS, logits, 0.0), axis=-1, keepdims=True)
        w_a = g_top_p / (1.0 + jnp.exp(lb - la))
        w_b = g_top_p / (1.0 + jnp.exp(la - lb))

        def hidden(wg_ref, wu_ref, weight):
            gate, up = _dot(xb, wg_ref[0, 0]), _dot(xb, wu_ref[0, 0])
            return ((gate * _sigmoid(gate)) * up * weight).astype(BF16)

        ffn = (_dot(hidden(wga_ref, wua_ref, w_a), wda_ref[0, 0])
               + _dot(hidden(wgb_ref, wub_ref, w_b), wdb_ref[0, 0]))
        return _layer_norm_rows(ALPHA * _load_token_major(x_ref, TM) + ffn, lng_ref[...], lnb_ref[...])

    @pl.when(first_tile < n_used)
    def _():
        @pl.when(step >= 2)
        def _():
            wait_rows(step_rows(step - 2), slot)

        for k in range(k_tiles):
            result = tile_result(k)
            for c in range(ROW_TILES):
                ybuf[slot, pl.ds(k * TM * ROW_TILES + c, TM, stride=ROW_TILES), :] = (
                    result[:, c * LANES:(c + 1) * LANES])
            start_rows(k, nvalid_ref[first_tile + k], slot)

        @pl.when(first_tile + k_tiles >= n_used)
        def _():
            @pl.when(step >= 1)
            def _():
                wait_rows(step_rows(step - 1), other)

            wait_rows(step_rows(step), slot)


def _experts(xs, t, dst, tile_ea, tile_eb, tile_nvalid, n_used, w_router, b_router, wg, wu, wd, layer,
             ln_g, ln_b):
    k_tiles = TILES_PER_STEP
    n_tiles = dst.shape[0] // TM
    assert n_tiles % k_tiles == 0
    const = lambda shape: pl.BlockSpec(shape, lambda i, ea, eb, nv, nu: (0,) * len(shape))
    dst_spec = lambda k: pl.BlockSpec((TM,), lambda i, ea, eb, nv, nu: (i * k_tiles + k,),
                                      memory_space=pltpu.SMEM)
    x_spec = lambda k: pl.BlockSpec((TM * ROW_TILES, LANES), lambda i, ea, eb, nv, nu: (i * k_tiles + k, 0))
    wgu_spec = lambda k, which: pl.BlockSpec(
        (1, 1, D_MODEL, D_EXPERT),
        lambda i, ea, eb, nv, nu: (layer, (ea, eb)[which][i * k_tiles + k], 0, 0))
    wd_spec = lambda k, which: pl.BlockSpec(
        (1, 1, D_EXPERT, D_MODEL),
        lambda i, ea, eb, nv, nu: (layer, (ea, eb)[which][i * k_tiles + k], 0, 0))
    tiles = range(k_tiles)
    weight_specs, weight_args = [], []
    for k in tiles:
        weight_specs += [wgu_spec(k, 0), wgu_spec(k, 0), wd_spec(k, 0),
                         wgu_spec(k, 1), wgu_spec(k, 1), wd_spec(k, 1)]
        weight_args += [wg, wu, wd, wg, wu, wd]
    grid_spec = pltpu.PrefetchScalarGridSpec(
        num_scalar_prefetch=4,
        grid=(n_tiles // k_tiles,),
        in_specs=([dst_spec(k) for k in tiles] + [x_spec(k) for k in tiles]
                  + [const((D_MODEL, LANES)), const((1, LANES))] + weight_specs
                  + [const((1, D_MODEL)), const((1, D_MODEL))]),
        out_specs=pl.BlockSpec(memory_space=pl.ANY),
        scratch_shapes=[pltpu.VMEM((2, k_tiles * TM * ROW_TILES, LANES), F32),
                        pltpu.SemaphoreType.DMA((2,))],
    )
    return pl.pallas_call(
        _expert_kernel,
        out_shape=jax.ShapeDtypeStruct((t * ROW_TILES, LANES), F32),
        grid_spec=grid_spec,
        compiler_params=pltpu.CompilerParams(dimension_semantics=("arbitrary",),
                                             vmem_limit_bytes=VMEM_LIMIT_BYTES),
        name="moe_experts",
    )(tile_ea, tile_eb, tile_nvalid, n_used, *([dst] * k_tiles), *([xs] * k_tiles),
      w_router, b_router, *weight_args, ln_g, ln_b)


def _relayout_kernel(h_ref, out_ref):
    out_ref[...] = _load_token_major(h_ref, TS)


def _rows_from_token_major(h, t):
    return pl.pallas_call(
        _relayout_kernel,
        out_shape=jax.ShapeDtypeStruct((t, D_MODEL), F32),
        grid=(t // TS,),
        in_specs=[pl.BlockSpec((TS * ROW_TILES, LANES), lambda i: (i, 0))],
        out_specs=pl.BlockSpec((TS, D_MODEL), lambda i: (i, 0)),
        name="rows_from_token_major",
    )(h)


def _class_expert_tables():
    first, second = [], []
    for g in range(N_GROUPS):
        for lo in range(EXPERTS_PER_GROUP):
            for hi in range(lo + 1, EXPERTS_PER_GROUP):
                first.append(g * EXPERTS_PER_GROUP + lo)
                second.append(g * EXPERTS_PER_GROUP + hi)
    return jnp.asarray(np.array(first, np.int32)), jnp.asarray(np.array(second, np.int32))


def _router_weights(wg, bg, we, be):
    w_router = jnp.zeros((D_MODEL, LANES), F32)
    w_router = w_router.at[:, :N_GROUPS].set(wg).at[:, N_GROUPS:N_GROUPS + N_EXPERTS].set(we)
    b_router = jnp.zeros((1, LANES), F32)
    b_router = b_router.at[0, :N_GROUPS].set(bg).at[0, N_GROUPS:N_GROUPS + N_EXPERTS].set(be)
    w_hi = w_router.astype(BF16)
    w_lo = (w_router - w_hi.astype(F32)).astype(BF16)
    return w_hi, w_lo, b_router


def _moe_layer(h, t, meta, counts, w_hi, b_router, wg, wu, wd, layer, ln_g, ln_b):
    n_tiles = t // TM + N_CLASSES
    n_rows = n_tiles * TM

    cnt = counts[0]
    tiles_per = (cnt + TM - 1) // TM
    tile_end = jnp.cumsum(tiles_per)
    offs = (tile_end - tiles_per) * TM
    n_used = tile_end[-1]
    tile_id = jnp.arange(n_tiles, dtype=jnp.int32)
    tile_cls = jnp.sum((jnp.minimum(tile_id, n_used - 1)[:, None] >= tile_end[None, :N_CLASSES])
                       .astype(jnp.int32), axis=1)
    seg_tile = tile_id - (tile_end - tiles_per)[tile_cls]
    tile_nvalid = jnp.where(tile_id < n_used, jnp.clip(cnt[tile_cls] - seg_tile * TM, 0, TM), 0)
    first, second = _class_expert_tables()
    xs, dst = _dispatch(h, offs, cnt, n_used.reshape(1), meta[0], meta[1], n_rows)
    return _experts(xs, t, dst, first[tile_cls], second[tile_cls], tile_nvalid, n_used.reshape(1),
                    w_hi, b_router, wg, wu, wd, layer, ln_g, ln_b)


def kernel(x, positions, w_in, sc_conv_w, sc_conv_b, lru_conv_w, lru_conv_b, lru_w_r, lru_b_r,
           lru_w_i, lru_b_i, lru_lambda, sg_norm_g, sg_w_s, sg_b_s, ret_norm_g, branch_proj, w_out,
           ln_mix_g, ln_mix_b, router_group_w, router_group_b, router_expert_w, router_expert_b,
           exp_w_gate, exp_w_up, exp_w_down, ln_ffn_g, ln_ffn_b):
    batch, seq, d = x.shape
    assert d == D_MODEL and seq % TS == 0 and w_in.shape[-1] == N_IN
    depth = w_in.shape[0]
    t = batch * seq
    cos_t, sin_t = _rope_tables(positions)
    dec_all, qd_tab, kd_tab, cd_tab = _retention_tables()
    avg = jnp.asarray(np.kron(np.eye(N_HEADS, dtype=np.float32),
                              np.full((HEAD_DIM, HEAD_DIM), 1.0 / HEAD_DIM, np.float32)), BF16)

    w_in_b, bp_b, wo_b = w_in.astype(BF16), branch_proj.astype(BF16), w_out.astype(BF16)
    wg_b, wu_b, wd_b = exp_w_gate.astype(BF16), exp_w_up.astype(BF16), exp_w_down.astype(BF16)

    h = x.reshape(t, d)
    for l in range(depth):
        w_ri = jnp.concatenate([_block_diag(lru_w_r[l]), _block_diag(lru_w_i[l])], axis=1).astype(BF16)
        b_ri = jnp.concatenate([lru_b_r[l], lru_b_i[l]])[None, :]
        bs_tab = jnp.repeat(sg_b_s[l].T, HEAD_DIM, axis=1)
        per_call = lambda *arrs: [(a, None) for a in arrs]
        w_hi, w_lo, b_router = _router_weights(router_group_w[l], router_group_b[l],
                                               router_expert_w[l], router_expert_b[l])
        consts = (
            [(w_in_b, l)]
            + per_call(sc_conv_w[l], sc_conv_b[l][None, :], lru_conv_w[l], lru_conv_b[l][None, :],
                       w_ri, b_ri, lru_lambda[l][None, :], sg_norm_g[l][None, :], sg_w_s[l], bs_tab,
                       ret_norm_g[l][None, :], dec_all, qd_tab, kd_tab, cd_tab, avg)
            + [(bp_b, l), (wo_b, l)]
            + per_call(ln_mix_g[l][None, :], ln_mix_b[l][None, :], w_hi, w_lo, b_router))
        h, meta, counts = _mixer_layer(h, cos_t, sin_t, consts, batch, seq, token_major_in=l > 0)
        h = _moe_layer(h, t, meta, counts, w_hi, b_router, wg_b, wu_b, wd_b, l,
                       ln_ffn_g[l][None, :], ln_ffn_b[l][None, :])
    return _rows_from_token_major(h, t).reshape(batch, seq, d)
```
